```python
import jax
import jax.numpy as jnp
from jax import lax
import numpy as np

D_MODEL = 1024
BATCH = 8
SEQ = 2048
DEPTH = 2
DEC_BATCH = 128
DEC_SEQ = 1
PAST_LEN = 2048
PAGE_SIZE = 128

HEAD_DIM = 64
W_MIX = 256
N_BRANCH = 4
POOL_WINDOWS = (2, 4, 8, 16)
POOL_GROUPS = 4
POOL_GROUP_W = W_MIX // POOL_GROUPS
POOL_BUF = max(POOL_WINDOWS) - 1
H_SB = W_MIX // HEAD_DIM
H_RET = W_MIX // HEAD_DIM
H_FOX = W_MIX // HEAD_DIM
Q_BLOCK = 128
RET_CHUNK = 128
ROPE_BASE = 10000.0
FOX_BIAS_INIT = 3.0
N_EXPERTS = 16
N_GROUPS = 4
EXPERTS_PER_GROUP = N_EXPERTS // N_GROUPS
TOP_K = 2
D_EXPERT = 512
N_MOD = 6
NORM_EPS = 1e-6
IN_SPLIT_SIZES = (W_MIX,
                  W_MIX, W_MIX, W_MIX,
                  W_MIX, W_MIX, W_MIX, W_MIX,
                  W_MIX, W_MIX, W_MIX,
                  H_FOX,
                  N_BRANCH * D_MODEL)
D_IN = sum(IN_SPLIT_SIZES)

kernel_name = 'hybrid_pool_stickbreak_retention_fox_moe_step'


def rmsnorm(x, g):
    xf = x.astype(jnp.float32)
    y = xf * lax.rsqrt(jnp.mean(xf * xf, axis=-1, keepdims=True) + NORM_EPS)
    return (y * g.astype(jnp.float32)).astype(x.dtype)


def rotary(x, pos):
    half = x.shape[-1] // 2
    inv = ROPE_BASE ** (-jnp.arange(half, dtype=jnp.float32) / half)
    ang = pos.astype(jnp.float32)[:, None] * inv[None, :]
    cos = jnp.cos(ang)[None, :, None, :]
    sin = jnp.sin(ang)[None, :, None, :]
    xf = x.astype(jnp.float32)
    x1, x2 = xf[..., :half], xf[..., half:]
    return jnp.concatenate([x1 * cos - x2 * sin, x1 * sin + x2 * cos], axis=-1).astype(x.dtype)


def pool_mixer(u, buf, pos, w_pool, pool_scale):
    T = u.shape[1]
    ext_in = jnp.concatenate([buf.astype(u.dtype), u], axis=1)
    ext = ext_in.astype(jnp.float32)
    cs = jnp.pad(jnp.cumsum(ext, axis=1), ((0, 0), (1, 0), (0, 0)))
    end = POOL_BUF + 1 + jnp.arange(T)
    outs = []
    for g, w in enumerate(POOL_WINDOWS):
        sl = slice(g * POOL_GROUP_W, (g + 1) * POOL_GROUP_W)
        win_sum = cs[:, end, sl] - cs[:, end - w, sl]
        count = jnp.minimum(w, pos + 1).astype(jnp.float32)
        resid = win_sum / count[None, :, None] - ext[:, POOL_BUF:, sl]
        outs.append(jnp.einsum('btc,cd->btd', resid.astype(u.dtype), w_pool[g]))
    out = jnp.concatenate(outs, axis=-1) * pool_scale
    return out, ext_in[:, -POOL_BUF:]


def sweep_query_blocks(block_fn, q_inputs, qpos):
    T = qpos.shape[0]
    qb = min(Q_BLOCK, T)
    nb = -(-T // qb)
    pad = nb * qb - T

    def blockify(a):
        a = jnp.pad(a, [(0, 0), (0, pad)] + [(0, 0)] * (a.ndim - 2))
        return a.reshape(a.shape[0], nb, qb, *a.shape[2:]).swapaxes(0, 1)

    pos_b = jnp.pad(qpos, (0, pad), mode='edge').reshape(nb, qb)
    out = lax.map(lambda xs: block_fn(*xs[0], xs[1]),
                  (tuple(blockify(a) for a in q_inputs), pos_b))
    out = out.swapaxes(0, 1)
    out = out.reshape(out.shape[0], nb * qb, *out.shape[3:])
    return out[:, :T]


def stick_breaking_block(q, k, v, qpos, kpos):
    z = jnp.einsum('bqhd,bshd->bhqs', q, k).astype(jnp.float32) * (HEAD_DIM ** -0.5)
    valid = kpos[None, :] < qpos[:, None]
    log_not = jnp.where(valid, jax.nn.log_sigmoid(-z), 0.0)
    after = lax.cumsum(log_not, axis=3, reverse=True) - log_not
    a = jnp.where(valid, jnp.exp(jax.nn.log_sigmoid(z) + after), 0.0)
    return jnp.einsum('bhqs,bshd->bqhd', a.astype(v.dtype), v)


def forgetting_block(q, cq, k, v, ck, qpos, kpos):
    s = jnp.einsum('bqhd,bshd->bhqs', q, k).astype(jnp.float32) * (HEAD_DIM ** -0.5)
    s = s + cq.transpose(0, 2, 1)[..., None] - ck[:, :, None, :]
    valid = kpos[None, :] <= qpos[:, None]
    p = jax.nn.softmax(jnp.where(valid, s, -jnp.inf), axis=-1)
    return jnp.einsum('bhqs,bshd->bqhd', p.astype(v.dtype), v)


def retention(q, k, v, state, chunk):
    B, T, H, _ = q.shape
    n = T // chunk
    lg = jnp.log1p(-(2.0 ** (-5.0 - jnp.arange(H, dtype=jnp.float32))))
    idx = jnp.arange(chunk, dtype=jnp.float32)
    diff = idx[:, None] - idx[None, :]
    inner_decay = jnp.where(diff >= 0, jnp.exp(lg[:, None, None] * jnp.maximum(diff, 0.0)), 0.0)
    q_decay = jnp.exp(lg[None, :] * (idx[:, None] + 1.0))
    k_decay = jnp.exp(lg[None, :] * (chunk - 1.0 - idx[:, None]))
    chunk_decay = jnp.exp(lg * chunk)

    def to_chunks(a):
        return a.astype(jnp.float32).reshape(B, n, chunk, H, a.shape[-1]).swapaxes(0, 1)

    def step(S, xs):
        qc, kc, vc = xs
        att = jnp.einsum('bihd,bjhd->bhij', qc, kc) * inner_decay[None]
        o = (jnp.einsum('bhij,bjhe->bihe', att, vc)
             + jnp.einsum('bihd,bhde->bihe', qc, S) * q_decay[None, :, :, None])
        S = (S * chunk_decay[None, :, None, None]
             + jnp.einsum('bjhd,bjhe->bhde', kc * k_decay[None, :, :, None], vc))
        return S, o

    S, o = lax.scan(step, state.astype(jnp.float32), (to_chunks(q), to_chunks(k), to_chunks(v)))
    o = o.swapaxes(0, 1).reshape(B, T, H, v.shape[-1])
    return o, S


def token_mixers(h, pos, past, lw):
    sb_k_past, sb_v_past, fox_k_past, fox_v_past, fox_logf_past, pool_buf, ret_state = past
    B, T, _ = h.shape
    proj = jnp.einsum('btd,dn->btn', h, lw['w_in'])
    offs = np.cumsum(IN_SPLIT_SIZES)[:-1].tolist()
    (u, sb_q, sb_k, sb_v, r_q, r_k, r_v, r_g, f_q, f_k, f_v, f_logit, gates) = jnp.split(proj, offs, axis=-1)

    def heads(a):
        return a.reshape(B, T, -1, HEAD_DIM)

    n_past = sb_k_past.shape[1]
    kpos = jnp.arange(n_past + T)

    pool_out, pool_new = pool_mixer(u, pool_buf, pos, lw['w_pool'], lw['pool_scale'])

    sb_k, sb_v = heads(sb_k), heads(sb_v)
    sbk_all = jnp.concatenate([sb_k_past.astype(h.dtype), sb_k], axis=1)
    sbv_all = jnp.concatenate([sb_v_past.astype(h.dtype), sb_v], axis=1)
    sb_out = sweep_query_blocks(
        lambda q, qp: stick_breaking_block(q, sbk_all, sbv_all, qp, kpos), (heads(sb_q),), pos)

    rq = rotary(heads(r_q), pos)
    rk = rotary(heads(r_k), pos) * (HEAD_DIM ** -0.5)
    chunk = RET_CHUNK if T % RET_CHUNK == 0 else T
    ret_o, ret_new = retention(rq, rk, heads(r_v), ret_state, chunk)
    mu = jnp.mean(ret_o, axis=-1, keepdims=True)
    var = jnp.mean(jnp.square(ret_o - mu), axis=-1, keepdims=True)
    ret_n = ((ret_o - mu) * lax.rsqrt(var + NORM_EPS)).reshape(B, T, W_MIX)
    ret_out = (jax.nn.silu(r_g.astype(jnp.float32)) * ret_n).astype(h.dtype)

    logf = jax.nn.log_sigmoid(f_logit.astype(jnp.float32) + lw['fox_bias'].astype(jnp.float32))
    cum = jnp.cumsum(jnp.concatenate([fox_logf_past.astype(jnp.float32), logf], axis=1), axis=1)
    f_k, f_v = heads(f_k), heads(f_v)
    fk_all = jnp.concatenate([fox_k_past.astype(h.dtype), f_k], axis=1)
    fv_all = jnp.concatenate([fox_v_past.astype(h.dtype), f_v], axis=1)
    ck = cum.transpose(0, 2, 1)
    fox_out = sweep_query_blocks(
        lambda q, cq, qp: forgetting_block(q, cq, fk_all, fv_all, ck, qp, kpos),
        (heads(f_q), cum[:, n_past:]), pos)

    branches = (pool_out, sb_out.reshape(B, T, W_MIX), ret_out, fox_out.reshape(B, T, W_MIX))
    gate_parts = jnp.split(gates, N_BRANCH, axis=-1)
    merged = sum(jax.nn.sigmoid(gate_parts[i]) * jnp.einsum('btc,cd->btd', branches[i], lw['w_branch'][i])
                 for i in range(N_BRANCH))
    out = jnp.einsum('btd,de->bte', merged, lw['w_out'])
    return out, (sb_k, sb_v, f_k, f_v, logf, pool_new, ret_new)


def moe_ffn(h, router_w, router_b, w1, w3, w2):
    scores = jax.nn.sigmoid(jnp.einsum('btd,de->bte', h, router_w).astype(jnp.float32))
    biased = scores + router_b.astype(jnp.float32)
    grp = biased.reshape(*biased.shape[:-1], N_GROUPS, EXPERTS_PER_GROUP)
    group_score = jnp.sum(lax.top_k(grp, TOP_K)[0], axis=-1)
    best = jnp.argmax(group_score, axis=-1)
    in_group = (jnp.arange(N_EXPERTS) // EXPERTS_PER_GROUP) == best[..., None]
    _, idx = lax.top_k(jnp.where(in_group, biased, -jnp.inf), TOP_K)
    sel = jnp.take_along_axis(scores, idx, axis=-1)
    wts = sel / jnp.sum(sel, axis=-1, keepdims=True)
    gate = jnp.sum(jax.nn.one_hot(idx, N_EXPERTS, dtype=jnp.float32) * wts[..., None], axis=-2)
    out = jnp.zeros(h.shape, jnp.float32)
    for e in range(N_EXPERTS):
        a = jnp.einsum('btd,df->btf', h, w1[e])
        b = jnp.einsum('btd,df->btf', h, w3[e])
        y = jnp.einsum('btf,fd->btd', jax.nn.silu(a) * b, w2[e])
        out = out + gate[..., e:e + 1] * y.astype(jnp.float32)
    return out.astype(h.dtype)


def trunk(x, c, pos, past_at, params):
    (w_ada, b_ada, norm_mix, w_in, w_pool, pool_scale, fox_bias, w_branch, w_out,
     norm_ffn, router_w, router_b, w1, w3, w2, norm_final) = params
    cond = jax.nn.silu(c)
    news = []
    for l in range(DEPTH):
        mod = jnp.einsum('bd,dn->bn', cond, w_ada[l]) + b_ada[l]
        sh1, sc1, g1, sh2, sc2, g2 = [m[:, None, :] for m in jnp.split(mod, N_MOD, axis=-1)]
        h = rmsnorm(x, norm_mix[l]) * (1.0 + sc1) + sh1
        lw = {'w_in': w_in[l], 'w_pool': w_pool[l], 'pool_scale': pool_scale[l],
              'fox_bias': fox_bias[l], 'w_branch': w_branch[l], 'w_out': w_out[l]}
        mix, new = token_mixers(h, pos, past_at(l), lw)
        x = x + g1 * mix
        h = rmsnorm(x, norm_ffn[l]) * (1.0 + sc2) + sh2
        x = x + g2 * moe_ffn(h, router_w, router_b, w1[l], w3[l], w2[l])
        news.append(new)
    y = rmsnorm(x, norm_final)
    sb_kv = jnp.stack([jnp.stack([n[0], n[1]], axis=1) for n in news], axis=1)
    fox_kv = jnp.stack([jnp.stack([n[2], n[3]], axis=1) for n in news], axis=1)
    logf = jnp.stack([n[4] for n in news], axis=1)
    pool = jnp.stack([n[5] for n in news], axis=1)
    ret = jnp.stack([n[6] for n in news], axis=1)
    return y, sb_kv, fox_kv, logf, pool, ret


def setup_inputs(seed: int = 0) -> dict:
    key = jax.random.key(seed)
    ks = iter(jax.random.split(key, 40))

    def nrm(shape, scale):
        return jax.random.normal(next(ks), shape, jnp.float32) * scale

    n_pages = PAST_LEN // PAGE_SIZE
    n_used = DEC_BATCH * n_pages
    n_phys = n_used + n_used // 4
    page_table = jax.random.permutation(next(ks), n_phys)[:n_used].reshape(DEC_BATCH, n_pages).astype(jnp.int32)
    d = D_MODEL
    return {
        'x_prompt': nrm((BATCH, SEQ, d), 1.0),
        'x_sample': nrm((DEC_BATCH, DEC_SEQ, d), 1.0),
        'cache_sb_kv': nrm((n_phys, DEPTH, 2, PAGE_SIZE, H_SB, HEAD_DIM), 1.0),
        'cache_fox_kv': nrm((n_phys, DEPTH, 2, PAGE_SIZE, H_FOX, HEAD_DIM), 1.0),
        'cache_fox_logf': jax.nn.log_sigmoid(FOX_BIAS_INIT + nrm((n_phys, DEPTH, PAGE_SIZE, H_FOX), 1.0)),
        'state_pool': nrm((DEC_BATCH, DEPTH, POOL_BUF, W_MIX), 1.0),
        'state_ret': nrm((DEC_BATCH, DEPTH, H_RET, HEAD_DIM, HEAD_DIM), 1.0),
        'page_table': page_table,
        'c_prompt': nrm((BATCH, d), 1.0),
        'c_sample': nrm((DEC_BATCH, d), 1.0),
        'w_ada': nrm((DEPTH, d, N_MOD * d), 0.5 * d ** -0.5),
        'b_ada': nrm((DEPTH, N_MOD * d), 0.02),
        'norm_mix': 1.0 + nrm((DEPTH, d), 0.05),
        'w_in': nrm((DEPTH, d, D_IN), d ** -0.5),
        'w_pool': nrm((DEPTH, POOL_GROUPS, POOL_GROUP_W, POOL_GROUP_W), POOL_GROUP_W ** -0.5),
        'pool_scale': 1.0 + nrm((DEPTH, W_MIX), 0.1),
        'fox_bias': FOX_BIAS_INIT + nrm((DEPTH, H_FOX), 0.1),
        'w_branch': nrm((DEPTH, N_BRANCH, W_MIX, d), W_MIX ** -0.5),
        'w_out': nrm((DEPTH, d, d), d ** -0.5),
        'norm_ffn': 1.0 + nrm((DEPTH, d), 0.05),
        'router_w': nrm((d, N_EXPERTS), d ** -0.5),
        'router_b': nrm((N_EXPERTS,), 0.01),
        'w1': nrm((DEPTH, N_EXPERTS, d, D_EXPERT), d ** -0.5),
        'w3': nrm((DEPTH, N_EXPERTS, d, D_EXPERT), d ** -0.5),
        'w2': nrm((DEPTH, N_EXPERTS, D_EXPERT, d), D_EXPERT ** -0.5),
        'norm_final': 1.0 + nrm((d,), 0.05),
    }


def reference(x_prompt, x_sample, cache_sb_kv, cache_fox_kv, cache_fox_logf, state_pool, state_ret,
              page_table, c_prompt, c_sample, w_ada, b_ada, norm_mix, w_in, w_pool, pool_scale,
              fox_bias, w_branch, w_out, norm_ffn, router_w, router_b, w1, w3, w2, norm_final):
    params = (w_ada, b_ada, norm_mix, w_in, w_pool, pool_scale, fox_bias, w_branch, w_out,
              norm_ffn, router_w, router_b, w1, w3, w2, norm_final)

    bp, tp = x_prompt.shape[:2]
    dt = x_prompt.dtype

    def prompt_past(l):
        return (jnp.zeros((bp, 0, H_SB, HEAD_DIM), dt), jnp.zeros((bp, 0, H_SB, HEAD_DIM), dt),
                jnp.zeros((bp, 0, H_FOX, HEAD_DIM), dt), jnp.zeros((bp, 0, H_FOX, HEAD_DIM), dt),
                jnp.zeros((bp, 0, H_FOX), jnp.float32), jnp.zeros((bp, POOL_BUF, W_MIX), dt),
                jnp.zeros((bp, H_RET, HEAD_DIM, HEAD_DIM), jnp.float32))

    y_prompt, sb_p, fox_p, logf_p, pool_p, ret_p = trunk(x_prompt, c_prompt, jnp.arange(tp), prompt_past, params)

    db, ts = x_sample.shape[:2]
    n_pages = page_table.shape[1]
    past_len = n_pages * cache_sb_kv.shape[3]

    def sample_past(l):
        sb = cache_sb_kv[page_table, l]
        fx = cache_fox_kv[page_table, l]
        lf = cache_fox_logf[page_table, l]
        return (sb[:, :, 0].reshape(db, past_len, H_SB, HEAD_DIM),
                sb[:, :, 1].reshape(db, past_len, H_SB, HEAD_DIM),
                fx[:, :, 0].reshape(db, past_len, H_FOX, HEAD_DIM),
                fx[:, :, 1].reshape(db, past_len, H_FOX, HEAD_DIM),
                lf.reshape(db, past_len, H_FOX),
                state_pool[:, l], state_ret[:, l])

    y_sample, sb_s, fox_s, logf_s, pool_s, ret_s = trunk(x_sample, c_sample, past_len + jnp.arange(ts), sample_past, params)

    return (y_prompt, y_sample, sb_p, sb_s, fox_p, fox_s, logf_p, logf_s, pool_p, pool_s, ret_p, ret_s)
```

```python
import functools

import numpy as np
import jax
import jax.numpy as jnp
from jax import lax
from jax.experimental import pallas as pl
from jax.experimental.pallas import tpu as pltpu

F32 = jnp.float32
BF16 = jnp.bfloat16

D_MODEL = 1024
HEAD_DIM = 64
W_MIX = 256
N_HEADS = W_MIX // HEAD_DIM
N_BRANCH = 4
POOL_WINDOWS = (2, 4, 8, 16)
POOL_BUF = 15
RET_CHUNK = 128
ROPE_BASE = 10000.0
N_EXPERTS = 16
EXPERTS_PER_GROUP = 4
D_EXPERT = 512
N_MOD = 6
NORM_EPS = 1e-6
ATTN_SCALE = HEAD_DIM ** -0.5
LANES = 128
SUBLANES = 8
VMEM_LIMIT = 56 * 1024 * 1024

COL_U, COL_SBQ, COL_SBK, COL_SBV = 0, 1, 2, 3
COL_RQ, COL_RK, COL_RV, COL_RG = 4, 5, 6, 7
COL_FQ, COL_FK, COL_FV, COL_LOGIT = 8, 9, 10, 11
N_MAIN = 11 * W_MIX
GATE_COL0 = 12 * W_MIX
N_PROJ = GATE_COL0 + N_BRANCH * D_MODEL


def _params(*sem):
    return pltpu.CompilerParams(dimension_semantics=sem, vmem_limit_bytes=VMEM_LIMIT)


def _split2(x):
    hi = x.astype(BF16)
    lo = (x - hi.astype(F32)).astype(BF16)
    return hi, lo


def _split3(x):
    hi = x.astype(BF16)
    r = x - hi.astype(F32)
    mid = r.astype(BF16)
    lo = (r - mid.astype(F32)).astype(BF16)
    return hi, mid, lo


def _dot(a, b):
    return jnp.dot(a, b, preferred_element_type=F32)


def _dot_nt(a, b):
    return lax.dot_general(a, b, (((1,), (1,)), ((), ())), preferred_element_type=F32)


def _dot_tn(a, b):
    return lax.dot_general(a, b, (((0,), (0,)), ((), ())), preferred_element_type=F32)


def _dot_split(x, w, parts=2):
    ps = _split2(x) if parts == 2 else _split3(x)
    acc = _dot(ps[0], w)
    for p in ps[1:]:
        acc = acc + _dot(p, w)
    return acc


def _iota(shape, dim):
    return lax.broadcasted_iota(jnp.int32, shape, dim)


def _softplus(z):
    return jnp.maximum(z, 0.0) + jnp.log1p(jnp.exp(-jnp.abs(z)))


def _silu(x):
    return x * jax.nn.sigmoid(x)


def _adaln_kernel(c_ref, w_ref, b_ref, o_ref):
    cond = _silu(c_ref[...]).astype(BF16)
    o_ref[...] = _dot(cond, w_ref[...].astype(BF16)) + b_ref[...]


def adaln(c_all, w_ada, b_ada):
    rows, d = c_all.shape
    depth, _, n = w_ada.shape
    tn = 1024
    return pl.pallas_call(
        _adaln_kernel,
        grid=(depth, n // tn),
        in_specs=[pl.BlockSpec((rows, d), lambda l, j: (0, 0)),
                  pl.BlockSpec((None, d, tn), lambda l, j: (l, 0, j)),
                  pl.BlockSpec((None, 1, tn), lambda l, j: (l, 0, j))],
        out_specs=pl.BlockSpec((None, rows, tn), lambda l, j: (l, 0, j)),
        out_shape=jax.ShapeDtypeStruct((depth, rows, n), F32),
        compiler_params=_params("parallel", "parallel"),
        name="adaln",
    )(c_all, w_ada, b_ada.reshape(depth, 1, n))


class Mod:
    def __init__(self, arr, rows_per_vec, tm):
        self.per_row = rows_per_vec == 1
        self.arr = arr if self.per_row else arr.reshape(arr.shape[0], 1, arr.shape[-1])
        self.tiles_per_vec = 1 if self.per_row else rows_per_vec // tm
        self.tm = tm

    def spec(self, k):
        if self.per_row:
            return pl.BlockSpec((self.tm, D_MODEL), lambda i, *_: (i, k))
        t = self.tiles_per_vec
        return pl.BlockSpec((None, 1, D_MODEL), lambda i, *_: (i // t, 0, k))


def _norm_mod(x, g, sc, sh):
    ms = jnp.mean(x * x, axis=-1, keepdims=True)
    y = x * lax.rsqrt(ms + NORM_EPS) * g
    return y * (1.0 + sc) + sh


def _inproj_kernel(x_ref, g_ref, sh_ref, sc_ref, w_ref, o_ref, h_scr):
    @pl.when(pl.program_id(1) == 0)
    def _():
        h_scr[...] = _norm_mod(x_ref[...], g_ref[...], sc_ref[...], sh_ref[...]).astype(BF16)

    o_ref[...] = _dot(h_scr[...], w_ref[...])


def in_proj(x, g, mod, w, tm):
    m, d = x.shape
    n = w.shape[1]
    tn = 1024
    return pl.pallas_call(
        _inproj_kernel,
        grid=(m // tm, n // tn),
        in_specs=[pl.BlockSpec((tm, d), lambda i, j: (i, 0)),
                  pl.BlockSpec((1, d), lambda i, j: (0, 0)),
                  mod.spec(0), mod.spec(1),
                  pl.BlockSpec((d, tn), lambda i, j: (0, j))],
        out_specs=pl.BlockSpec((tm, tn), lambda i, j: (i, j)),
        out_shape=jax.ShapeDtypeStruct((m, n), F32),
        scratch_shapes=[pltpu.VMEM((tm, d), BF16)],
        compiler_params=_params("parallel", "arbitrary"),
        name="in_proj",
    )(x, g.reshape(1, d), mod.arr, mod.arr, w)


def _pool_tail(s2, s4, s8, s16, u, pos0, w_ref, sc_ref, o_ref):
    t = u.shape[0]
    lane = _iota((1, W_MIX), 1) // (W_MIX // len(POOL_WINDOWS))
    win = jnp.where(lane == 0, s2, jnp.where(lane == 1, s4, jnp.where(lane == 2, s8, s16)))
    width = jnp.where(lane == 0, 2, jnp.where(lane == 1, 4, jnp.where(lane == 2, 8, 16)))
    count = jnp.minimum(width, pos0 + 1 + _iota((t, W_MIX), 0)).astype(F32)
    resid = win / count - u
    o_ref[...] = (_dot(resid.astype(BF16), w_ref[...]) * sc_ref[...]).astype(o_ref.dtype)


_POOL_PAD = 32


def _pool_prompt_kernel(u_ref, w_ref, sc_ref, o_ref, a_scr, b_scr, c_scr):
    t = u_ref.shape[0]
    p = _POOL_PAD
    u = u_ref[...]
    a_scr[0:p, :] = jnp.zeros((p, W_MIX), F32)
    a_scr[p:p + t, :] = u

    def stage(src, dst, k, lo):
        n = p + t - lo
        dst[lo:lo + n, :] = src[lo:lo + n, :] + src[lo - k:lo - k + n, :]

    stage(a_scr, b_scr, 1, 8)
    stage(b_scr, c_scr, 2, 16)
    stage(c_scr, a_scr, 4, 24)
    s8 = a_scr[p:p + t, :]
    s16 = s8 + a_scr[p - 8:p - 8 + t, :]
    _pool_tail(b_scr[p:p + t, :], c_scr[p:p + t, :], s8, s16, u, 0, w_ref, sc_ref, o_ref)


def pool_prompt(proj, w_bd, scale, b, t):
    return pl.pallas_call(
        _pool_prompt_kernel,
        grid=(b,),
        in_specs=[pl.BlockSpec((t, W_MIX), lambda i: (i, COL_U)),
                  pl.BlockSpec((W_MIX, W_MIX), lambda i: (0, 0)),
                  pl.BlockSpec((1, W_MIX), lambda i: (0, 0))],
        out_specs=pl.BlockSpec((t, W_MIX), lambda i: (i, 0)),
        out_shape=jax.ShapeDtypeStruct((b * t, W_MIX), BF16),
        scratch_shapes=[pltpu.VMEM((t + _POOL_PAD, W_MIX), F32)] * 3,
        compiler_params=_params("parallel"),
        name="pool_prompt",
    )(proj, w_bd, scale)


def _pool_sample_kernel(buf_ref, u_ref, w_ref, sc_ref, o_ref, *, pos0):
    u = u_ref[...]
    s2 = u + buf_ref[14]
    s4 = s2 + buf_ref[13] + buf_ref[12]
    s8 = s4 + buf_ref[11] + buf_ref[10] + buf_ref[9] + buf_ref[8]
    s16 = s8
    for r in range(7, -1, -1):
        s16 = s16 + buf_ref[r]
    _pool_tail(s2, s4, s8, s16, u, pos0, w_ref, sc_ref, o_ref)


def pool_sample(buf_t, proj, w_bd, scale, pos0):
    rows = proj.shape[0]
    return pl.pallas_call(
        functools.partial(_pool_sample_kernel, pos0=pos0),
        grid=(1,),
        in_specs=[pl.BlockSpec((POOL_BUF, rows, W_MIX), lambda i: (0, 0, 0)),
                  pl.BlockSpec((rows, W_MIX), lambda i: (0, COL_U)),
                  pl.BlockSpec((W_MIX, W_MIX), lambda i: (0, 0)),
                  pl.BlockSpec((1, W_MIX), lambda i: (0, 0))],
        out_specs=pl.BlockSpec((rows, W_MIX), lambda i: (0, 0)),
        out_shape=jax.ShapeDtypeStruct((rows, W_MIX), BF16),
        compiler_params=_params("arbitrary"),
        name="pool_sample",
    )(buf_t, proj, w_bd, scale)


_ATT_TILE = 256


def _head_of_lane():
    return _iota((1, W_MIX), 1) // HEAD_DIM


def _load_kv_once(k_ref, v_ref, k_scr, v_scr):
    @pl.when(pl.program_id(1) == 0)
    def _():
        k_scr[...] = k_ref[...].astype(BF16)
        v_scr[...] = v_ref[...].astype(BF16)


def _sb_prompt_kernel(q_ref, k_ref, v_ref, o_ref, k_scr, v_scr):
    tq = tk = _ATT_TILE
    i = pl.program_id(1)
    _load_kv_once(k_ref, v_ref, k_scr, v_scr)
    q = q_ref[...] * ATTN_SCALE
    head = _head_of_lane()
    row = i * tq + _iota((tq, tk), 0)
    col = _iota((tq, tk), 1)
    upper = (_iota((tk, tk), 0) > _iota((tk, tk), 1)).astype(BF16)
    out = jnp.zeros((tq, W_MIX), F32)
    for h in range(N_HEADS):
        qh = jnp.where(head == h, q, 0.0).astype(BF16)

        def body(s, carry, qh=qh):
            acc, run = carry
            off = pl.multiple_of((i - s) * tk, tk)
            kb = k_scr[pl.ds(off, tk), :]
            vb = v_scr[pl.ds(off, tk), :]
            z = _dot_nt(qh, kb)
            valid = (off + col) < row
            sp = _softplus(z)
            log_not = jnp.where(valid, -sp, 0.0)
            suffix = _dot_split(log_not, upper)
            a = jnp.where(valid, jnp.exp(z - sp + suffix + run), 0.0)
            acc = acc + _dot(a.astype(BF16), vb)
            run = run + suffix[:, 0:1] + log_not[:, 0:1]
            return acc, run

        acc, _ = lax.fori_loop(0, i + 1, body,
                               (jnp.zeros((tq, W_MIX), F32), jnp.zeros((tq, 1), F32)))
        out = out + jnp.where(head == h, acc, 0.0)
    o_ref[...] = out.astype(o_ref.dtype)


def _attn_prompt_specs(t, col_q, col_k, col_v):
    tq = _ATT_TILE
    nq = t // tq
    return [pl.BlockSpec((tq, W_MIX), lambda b, i: (b * nq + i, col_q)),
            pl.BlockSpec((t, W_MIX), lambda b, i: (b, col_k)),
            pl.BlockSpec((t, W_MIX), lambda b, i: (b, col_v))]


def sb_prompt(proj, b, t):
    tq = _ATT_TILE
    nq = t // tq
    return pl.pallas_call(
        _sb_prompt_kernel,
        grid=(b, nq),
        in_specs=_attn_prompt_specs(t, COL_SBQ, COL_SBK, COL_SBV),
        out_specs=pl.BlockSpec((tq, W_MIX), lambda bb, i: (bb * nq + i, 0)),
        out_shape=jax.ShapeDtypeStruct((b * t, W_MIX), BF16),
        scratch_shapes=[pltpu.VMEM((t, W_MIX), BF16)] * 2,
        compiler_params=_params("parallel", "arbitrary"),
        name="sb_prompt",
    )(proj, proj, proj)


def _logf_cum_kernel(x_ref, bias_ref, logf_ref, cum_ref, cumt_ref):
    t = x_ref.shape[0]
    blk = 256
    logf = jax.nn.log_sigmoid(x_ref[...] + bias_ref[...])
    logf_ref[...] = logf
    lower = (_iota((blk, blk), 1) <= _iota((blk, blk), 0)).astype(BF16)
    carry = jnp.zeros((1, LANES), F32)
    for c in range(t // blk):
        part = logf[c * blk:(c + 1) * blk]
        hi, mid, lo = _split3(part)
        cum = _dot(lower, hi) + _dot(lower, mid) + _dot(lower, lo) + carry
        cum_ref[c * blk:(c + 1) * blk, :] = cum
        carry = cum[blk - 1:blk, :]
    cumt_ref[...] = jnp.transpose(cum_ref[...])[0:SUBLANES, :]


def logf_cum_prompt(proj, bias_pad, b, t):
    m = b * t
    return pl.pallas_call(
        _logf_cum_kernel,
        grid=(b,),
        in_specs=[pl.BlockSpec((t, LANES), lambda i: (i, COL_LOGIT * W_MIX // LANES)),
                  pl.BlockSpec((1, LANES), lambda i: (0, 0))],
        out_specs=[pl.BlockSpec((t, LANES), lambda i: (i, 0)),
                   pl.BlockSpec((t, LANES), lambda i: (i, 0)),
                   pl.BlockSpec((None, SUBLANES, t), lambda i: (i, 0, 0))],
        out_shape=[jax.ShapeDtypeStruct((m, LANES), F32),
                   jax.ShapeDtypeStruct((m, LANES), F32),
                   jax.ShapeDtypeStruct((b, SUBLANES, t), F32)],
        compiler_params=_params("parallel"),
        name="logf_cum",
    )(proj, bias_pad)


def _fox_prompt_kernel(q_ref, k_ref, v_ref, cq_ref, ck_ref, o_ref, k_scr, v_scr):
    tq = tk = _ATT_TILE
    i = pl.program_id(1)
    _load_kv_once(k_ref, v_ref, k_scr, v_scr)
    q = q_ref[...] * ATTN_SCALE
    head = _head_of_lane()
    row = i * tq + _iota((tq, tk), 0)
    col = _iota((tq, tk), 1)
    cq = cq_ref[...]
    out = jnp.zeros((tq, W_MIX), F32)
    for h in range(N_HEADS):
        qh = jnp.where(head == h, q, 0.0).astype(BF16)
        cqh = cq[:, h:h + 1]

        def body(s, carry, qh=qh, cqh=cqh, h=h):
            acc, m, l = carry
            off = pl.multiple_of((i - s) * tk, tk)
            kb = k_scr[pl.ds(off, tk), :]
            vb = v_scr[pl.ds(off, tk), :]
            ckh = ck_ref[h:h + 1, pl.ds(off, tk)]
            sc = _dot_nt(qh, kb) + (cqh - ckh)
            sc = jnp.where((off + col) <= row, sc, -jnp.inf)
            m_new = jnp.maximum(m, jnp.max(sc, axis=1, keepdims=True))
            alpha = jnp.exp(m - m_new)
            p = jnp.exp(sc - m_new)
            l = alpha * l + jnp.sum(p, axis=1, keepdims=True)
            acc = alpha * acc + _dot(p.astype(BF16), vb)
            return acc, m_new, l

        acc, _, l = lax.fori_loop(0, i + 1, body,
                                  (jnp.zeros((tq, W_MIX), F32),
                                   jnp.full((tq, 1), -jnp.inf, F32),
                                   jnp.zeros((tq, 1), F32)))
        out = out + jnp.where(head == h, acc / l, 0.0)
    o_ref[...] = out.astype(o_ref.dtype)


def fox_prompt(proj, cum, cum_t, b, t):
    tq = _ATT_TILE
    nq = t // tq
    specs = _attn_prompt_specs(t, COL_FQ, COL_FK, COL_FV)
    specs += [pl.BlockSpec((tq, LANES), lambda bb, i: (bb * nq + i, 0)),
              pl.BlockSpec((None, SUBLANES, t), lambda bb, i: (bb, 0, 0))]
    return pl.pallas_call(
        _fox_prompt_kernel,
        grid=(b, nq),
        in_specs=specs,
        out_specs=pl.BlockSpec((tq, W_MIX), lambda bb, i: (bb * nq + i, 0)),
        out_shape=jax.ShapeDtypeStruct((b * t, W_MIX), BF16),
        scratch_shapes=[pltpu.VMEM((t, W_MIX), BF16)] * 2,
        compiler_params=_params("parallel", "arbitrary"),
        name="fox_prompt",
    )(proj, proj, proj, cum, cum_t)


def _rotary(x, cos, sin_signed):
    half = HEAD_DIM // 2
    first = (_iota((1, W_MIX), 1) % HEAD_DIM) < half
    swapped = jnp.where(first, pltpu.roll(x, W_MIX - half, axis=1), pltpu.roll(x, half, axis=1))
    return x * cos + swapped * sin_signed


def _head_mean_matrix():
    same = (_iota((W_MIX, W_MIX), 0) // HEAD_DIM) == (_iota((W_MIX, W_MIX), 1) // HEAD_DIM)
    return jnp.where(same, 1.0 / HEAD_DIM, 0.0).astype(BF16)


def _head_norm_gate(o, gate_logit):
    avg = _head_mean_matrix()
    mu = _dot_split(o, avg)
    dev = o - mu
    var = _dot_split(dev * dev, avg)
    return _silu(gate_logit) * (dev * lax.rsqrt(var + NORM_EPS))


def _ret_prompt_kernel(q_ref, k_ref, v_ref, g_ref, cos_ref, sin_ref, inner_ref, qd_ref, kd_ref,
                       cd_ref, o_ref, s_ref):
    c = pl.program_id(1)

    @pl.when(c == 0)
    def _():
        s_ref[...] = jnp.zeros_like(s_ref)

    cos = cos_ref[...]
    sin = sin_ref[...]
    q = _rotary(q_ref[...], cos, sin)
    k = _rotary(k_ref[...], cos, sin) * ATTN_SCALE
    qb = q.astype(BF16)
    kb = k.astype(BF16)
    vb = v_ref[...].astype(BF16)
    head = _head_of_lane()
    state = s_ref[...]
    o = _dot(qb, state.astype(BF16)) * qd_ref[...]
    for h in range(N_HEADS):
        qh = jnp.where(head == h, q, 0.0).astype(BF16)
        att = _dot_nt(qh, kb) * inner_ref[h]
        o = o + jnp.where(head == h, _dot(att.astype(BF16), vb), 0.0)
    same = (_iota((W_MIX, W_MIX), 0) // HEAD_DIM) == (_iota((W_MIX, W_MIX), 1) // HEAD_DIM)
    kv = _dot_tn((k * kd_ref[...]).astype(BF16), vb)
    s_ref[...] = state * cd_ref[...] + jnp.where(same, kv, 0.0)
    o_ref[...] = _head_norm_gate(o, g_ref[...]).astype(o_ref.dtype)


def ret_prompt(proj, tabs, b, t):
    ch = RET_CHUNK
    nc = t // ch

    def col(cb):
        return pl.BlockSpec((ch, W_MIX), lambda bb, c: (bb * nc + c, cb))

    full = lambda shape: pl.BlockSpec(shape, lambda bb, c: (0,) * len(shape))
    return pl.pallas_call(
        _ret_prompt_kernel,
        grid=(b, nc),
        in_specs=[col(COL_RQ), col(COL_RK), col(COL_RV), col(COL_RG),
                  pl.BlockSpec((ch, W_MIX), lambda bb, c: (c, 0)),
                  pl.BlockSpec((ch, W_MIX), lambda bb, c: (c, 0)),
                  full((N_HEADS, ch, ch)), full((ch, W_MIX)), full((ch, W_MIX)), full((1, W_MIX))],
        out_specs=[pl.BlockSpec((ch, W_MIX), lambda bb, c: (bb * nc + c, 0)),
                   pl.BlockSpec((None, W_MIX, W_MIX), lambda bb, c: (bb, 0, 0))],
        out_shape=[jax.ShapeDtypeStruct((b * t, W_MIX), BF16),
                   jax.ShapeDtypeStruct((b, W_MIX, W_MIX), F32)],
        compiler_params=_params("parallel", "arbitrary"),
        name="ret_prompt",
    )(proj, proj, proj, proj, tabs["cos"], tabs["sin"], tabs["inner"], tabs["q_decay"],
      tabs["k_decay"], tabs["chunk_decay"])


def _ret_sample_kernel(gam_ref, q_ref, k_ref, v_ref, g_ref, cos_ref, sin_ref, qd_ref, s_ref,
                       o_ref, sn_ref):
    h = pl.program_id(0)
    hd2 = HEAD_DIM * HEAD_DIM
    cos = cos_ref[...]
    sin = sin_ref[...]
    q = _rotary(q_ref[...], cos, sin)
    k = _rotary(k_ref[...], cos, sin) * ATTN_SCALE
    v = v_ref[...]
    r = _iota((W_MIX, hd2), 0)
    cidx = _iota((W_MIX, hd2), 1)
    rep_d = (r == h * HEAD_DIM + cidx // HEAD_DIM).astype(BF16)
    rep_e = (r == h * HEAD_DIM + cidx % HEAD_DIM).astype(BF16)
    state = s_ref[...]
    qe = _dot_split(q, rep_d)
    ke = _dot_split(k, rep_d)
    ve = _dot_split(v, rep_e)
    hi, lo = _split2(qe * state)
    cross = (_dot_nt(hi, rep_e) + _dot_nt(lo, rep_e)) * qd_ref[...]
    sn_ref[...] = gam_ref[h] * state + ke * ve
    head_r = _iota((W_MIX, W_MIX), 0) // HEAD_DIM
    head_c = _iota((W_MIX, W_MIX), 1) // HEAD_DIM
    ind = ((head_r == h) & (head_c == h)).astype(BF16)
    qk = _dot_split(q * k, ind)
    contrib = qk * v + cross

    @pl.when(h == 0)
    def _():
        o_ref[...] = contrib

    @pl.when(h > 0)
    def _():
        o_ref[...] = o_ref[...] + contrib

    @pl.when(h == N_HEADS - 1)
    def _():
        o_ref[...] = _head_norm_gate(o_ref[...], g_ref[...])


def ret_sample(proj, state_flat, layer, tabs):
    rows = proj.shape[0]
    hd2 = HEAD_DIM * HEAD_DIM

    def col(cb):
        return pl.BlockSpec((rows, W_MIX), lambda h, *_: (0, cb))

    row_tab = pl.BlockSpec((1, W_MIX), lambda h, *_: (0, 0))
    grid_spec = pltpu.PrefetchScalarGridSpec(
        num_scalar_prefetch=0,
        grid=(N_HEADS,),
        in_specs=[pl.BlockSpec(memory_space=pltpu.SMEM),
                  col(COL_RQ), col(COL_RK), col(COL_RV), col(COL_RG), row_tab, row_tab, row_tab,
                  pl.BlockSpec((rows, hd2), lambda h, *_: (0, layer * N_HEADS + h))],
        out_specs=[pl.BlockSpec((rows, W_MIX), lambda h, *_: (0, 0)),
                   pl.BlockSpec((rows, hd2), lambda h, *_: (0, h))],
    )
    return pl.pallas_call(
        _ret_sample_kernel,
        grid_spec=grid_spec,
        out_shape=[jax.ShapeDtypeStruct((rows, W_MIX), F32),
                   jax.ShapeDtypeStruct((rows, N_HEADS * hd2), F32)],
        compiler_params=_params("arbitrary"),
        name="ret_sample",
    )(tabs["gamma"], proj, proj, proj, proj, tabs["cos"], tabs["sin"], tabs["q_decay"], state_flat)


def _suffix_sums(x3, upper):
    n_pages, rows, page = x3.shape
    flat = x3.reshape(n_pages * rows, page)
    within = _dot_split(flat, upper)
    total = (within[:, 0:1] + flat[:, 0:1]).reshape(n_pages, rows, 1)
    within = within.reshape(n_pages, rows, page)
    carry = jnp.zeros((rows, 1), F32)
    outs = [None] * n_pages
    for j in range(n_pages - 1, -1, -1):
        outs[j] = within[j] + carry
        carry = carry + total[j]
    return jnp.stack(outs)


def _rows_to_col(row):
    pick = _iota((SUBLANES, W_MIX), 0) == _iota((SUBLANES, W_MIX), 1)
    return jnp.sum(jnp.where(pick, row, 0.0), axis=1, keepdims=True)


def _decode_kernel(pt_ref, qs_ref, qf_ref, kf_ref, vf_ref, lg_ref, bias_ref, *refs, n_pages):
    sb_pages = refs[0:n_pages]
    fx_pages = refs[n_pages:2 * n_pages]
    lf_pages = refs[2 * n_pages:3 * n_pages]
    sb_o, fx_o, lf_o = refs[3 * n_pages:]
    page = sb_pages[0].shape[1]
    head_rows = _iota((SUBLANES, W_MIX), 0) == (_iota((SUBLANES, W_MIX), 1) // HEAD_DIM)
    upper = (_iota((page, page), 0) > _iota((page, page), 1)).astype(BF16)

    def per_head_rows(row):
        return jnp.where(head_rows, row, 0.0)

    def collapse(x):
        return jnp.sum(jnp.where(head_rows, x, 0.0), axis=0, keepdims=True)

    qs = per_head_rows(qs_ref[...] * ATTN_SCALE).astype(BF16)
    z = jnp.stack([_dot_nt(qs, sb_pages[j][0].astype(BF16)) for j in range(n_pages)])
    sp = _softplus(z)
    after = _suffix_sums(-sp, upper)
    a = jnp.exp(z - sp + after).astype(BF16)
    acc = jnp.zeros((SUBLANES, W_MIX), F32)
    for j in range(n_pages):
        acc = acc + _dot(a[j], sb_pages[j][1].astype(BF16))
    sb_o[...] = collapse(acc)

    logf_new = jax.nn.log_sigmoid(lg_ref[...] + bias_ref[...])
    lf_o[...] = logf_new
    qf = per_head_rows(qf_ref[...] * ATTN_SCALE).astype(BF16)
    decay = _suffix_sums(jnp.stack([lf_pages[j][...] for j in range(n_pages)]), upper)
    s = jnp.stack([_dot_nt(qf, fx_pages[j][0].astype(BF16)) for j in range(n_pages)])
    s = s + decay + _rows_to_col(logf_new)
    k_new = jnp.broadcast_to(kf_ref[...], (SUBLANES, W_MIX)).astype(BF16)
    s_self = _dot_nt(qf, k_new)[:, 0:1]
    m = jnp.max(jnp.max(s, axis=0), axis=1, keepdims=True)
    m = jnp.maximum(m, s_self)
    p = jnp.exp(s - m)
    p_self = jnp.exp(s_self - m)
    l = jnp.sum(jnp.sum(p, axis=0), axis=1, keepdims=True) + p_self
    pb = p.astype(BF16)
    acc = p_self.astype(BF16).astype(F32) * vf_ref[...].astype(BF16).astype(F32)
    for j in range(n_pages):
        acc = acc + _dot(pb[j], fx_pages[j][1].astype(BF16))
    fx_o[...] = collapse(acc / l)


def decode_attn(page_table, proj, sb_cache, fox_cache, logf_t, bias_row, layer):
    rows, n_pages = page_table.shape
    page = sb_cache.shape[3]
    proj3 = proj.reshape(rows, 1, proj.shape[1])

    def col(cb):
        return pl.BlockSpec((None, 1, W_MIX), lambda b, pt: (b, 0, cb))

    def kv_page(j):
        return pl.BlockSpec((None, None, 2, page, W_MIX),
                            lambda b, pt, j=j: (pt[b * n_pages + j], layer, 0, 0, 0))

    def lf_page(j):
        return pl.BlockSpec((None, None, SUBLANES, page),
                            lambda b, pt, j=j: (pt[b * n_pages + j], layer, 0, 0))

    out_row = pl.BlockSpec((None, 1, W_MIX), lambda b, pt: (b, 0, 0))
    grid_spec = pltpu.PrefetchScalarGridSpec(
        num_scalar_prefetch=1,
        grid=(rows,),
        in_specs=[col(COL_SBQ), col(COL_FQ), col(COL_FK), col(COL_FV), col(COL_LOGIT),
                  pl.BlockSpec((1, W_MIX), lambda b, pt: (0, 0))]
                 + [kv_page(j) for j in range(n_pages)]
                 + [kv_page(j) for j in range(n_pages)]
                 + [lf_page(j) for j in range(n_pages)],
        out_specs=[out_row, out_row, out_row],
    )
    sb_o, fx_o, lf_o = pl.pallas_call(
        functools.partial(_decode_kernel, n_pages=n_pages),
        grid_spec=grid_spec,
        out_shape=[jax.ShapeDtypeStruct((rows, 1, W_MIX), F32)] * 3,
        compiler_params=_params("arbitrary"),
        name="decode_attn",
    )(page_table.reshape(-1), proj3, proj3, proj3, proj3, proj3, bias_row,
      *([sb_cache] * n_pages), *([fox_cache] * n_pages), *([logf_t] * n_pages))
    return sb_o.reshape(rows, W_MIX), fx_o.reshape(rows, W_MIX), lf_o.reshape(rows, W_MIX)


def _merge_kernel(b0_ref, b1_ref, b2_ref, b3_ref, g0_ref, g1_ref, g2_ref, g3_ref, wb_ref, wo_ref,
                  x_ref, gate_ref, o_ref):
    merged = None
    for br, gl, i in ((b0_ref, g0_ref, 0), (b1_ref, g1_ref, 1), (b2_ref, g2_ref, 2), (b3_ref, g3_ref, 3)):
        y = jax.nn.sigmoid(gl[...]) * _dot(br[...].astype(BF16), wb_ref[i])
        merged = y if merged is None else merged + y
    out = _dot(merged.astype(BF16), wo_ref[...])
    o_ref[...] = x_ref[...] + gate_ref[...] * out


def merge_out(branches, proj, w_branch, w_out, x, mod, tm):
    m, d = x.shape
    gate0 = GATE_COL0 // d
    br_spec = pl.BlockSpec((tm, W_MIX), lambda i: (i, 0))
    return pl.pallas_call(
        _merge_kernel,
        grid=(m // tm,),
        in_specs=[br_spec] * N_BRANCH
                 + [pl.BlockSpec((tm, d), lambda i, k=k: (i, gate0 + k)) for k in range(N_BRANCH)]
                 + [pl.BlockSpec((N_BRANCH, W_MIX, d), lambda i: (0, 0, 0)),
                    pl.BlockSpec((d, d), lambda i: (0, 0)),
                    pl.BlockSpec((tm, d), lambda i: (i, 0)),
                    mod.spec(2)],
        out_specs=pl.BlockSpec((tm, d), lambda i: (i, 0)),
        out_shape=jax.ShapeDtypeStruct((m, d), F32),
        compiler_params=_params("parallel"),
        name="merge_out",
    )(*branches, proj, proj, proj, proj, w_branch, w_out, x, mod.arr)


def _route(scores, biased):
    lane_i = _iota(scores.shape, 1)
    real = lane_i < N_EXPERTS
    pos = lane_i % EXPERTS_PER_GROUP
    lane = lane_i.astype(F32)
    group = (lane_i // EXPERTS_PER_GROUP).astype(F32)
    neg = -jnp.inf
    n = scores.shape[1]
    mates = [biased]
    for k in range(1, EXPERTS_PER_GROUP):
        fwd = pltpu.roll(biased, n - k, axis=1)
        back = pltpu.roll(biased, EXPERTS_PER_GROUP - k, axis=1)
        mates.append(jnp.where(pos + k < EXPERTS_PER_GROUP, fwd, back))
    group_score = None
    for a in range(EXPERTS_PER_GROUP):
        for b in range(a + 1, EXPERTS_PER_GROUP):
            pair = mates[a] + mates[b]
            group_score = pair if group_score is None else jnp.maximum(group_score, pair)
    group_score = jnp.where(real, group_score, neg)
    best_score = jnp.max(group_score, axis=1, keepdims=True)
    best = jnp.min(jnp.where(group_score == best_score, group, float(n)), axis=1, keepdims=True)
    cand = jnp.where(real & (group == best), biased, neg)
    top0 = jnp.max(cand, axis=1, keepdims=True)
    idx0 = jnp.min(jnp.where(cand == top0, lane, float(n)), axis=1, keepdims=True)
    cand = jnp.where(lane == idx0, neg, cand)
    top1 = jnp.max(cand, axis=1, keepdims=True)
    idx1 = jnp.min(jnp.where(cand == top1, lane, float(n)), axis=1, keepdims=True)
    chosen = (lane == idx0) | (lane == idx1)
    sel = jnp.where(chosen, scores, 0.0)
    return sel / jnp.sum(sel, axis=1, keepdims=True)


def _moe_kernel(x_ref, g_ref, sh_ref, sc_ref, gate_ref, rw_ref, rb_ref, w1_ref, w3_ref, w2_ref,
                o_ref, h_scr, comb_scr, acc_scr):
    e = pl.program_id(1)

    @pl.when(e == 0)
    def _():
        h = _norm_mod(x_ref[...], g_ref[...], sc_ref[...], sh_ref[...])
        h_scr[...] = h.astype(BF16)
        h_hi, h_lo = _split2(h)
        w_hi, w_lo = _split2(rw_ref[...])
        logits = _dot(h_hi, w_hi) + _dot(h_hi, w_lo) + _dot(h_lo, w_hi)
        scores = jax.nn.sigmoid(logits)
        comb_scr[...] = _route(scores, scores + rb_ref[...])
        acc_scr[...] = jnp.zeros_like(acc_scr)

    h = h_scr[...]
    a = _dot(h, w1_ref[...])
    b = _dot(h, w3_ref[...])
    y = _dot((_silu(a) * b).astype(BF16), w2_ref[...])
    lane = _iota(comb_scr.shape, 1)
    w_e = jnp.sum(jnp.where(lane == e, comb_scr[...], 0.0), axis=1, keepdims=True)
    acc_scr[...] = acc_scr[...] + w_e * y

    @pl.when(e == N_EXPERTS - 1)
    def _():
        o_ref[...] = x_ref[...] + gate_ref[...] * acc_scr[...]


def moe(x, g, mod, router_w_pad, router_b_pad, w1, w3, w2, tm):
    m, d = x.shape
    f = w1.shape[-1]
    return pl.pallas_call(
        _moe_kernel,
        grid=(m // tm, N_EXPERTS),
        in_specs=[pl.BlockSpec((tm, d), lambda i, e: (i, 0)),
                  pl.BlockSpec((1, d), lambda i, e: (0, 0)),
                  mod.spec(3), mod.spec(4), mod.spec(5),
                  pl.BlockSpec((d, LANES), lambda i, e: (0, 0)),
                  pl.BlockSpec((1, LANES), lambda i, e: (0, 0)),
                  pl.BlockSpec((None, d, f), lambda i, e: (e, 0, 0)),
                  pl.BlockSpec((None, d, f), lambda i, e: (e, 0, 0)),
                  pl.BlockSpec((None, f, d), lambda i, e: (e, 0, 0))],
        out_specs=pl.BlockSpec((tm, d), lambda i, e: (i, 0)),
        out_shape=jax.ShapeDtypeStruct((m, d), F32),
        scratch_shapes=[pltpu.VMEM((tm, d), BF16), pltpu.VMEM((tm, LANES), F32),
                        pltpu.VMEM((tm, d), F32)],
        compiler_params=_params("parallel", "arbitrary"),
        name="moe",
    )(x, g.reshape(1, d), mod.arr, mod.arr, mod.arr, router_w_pad, router_b_pad, w1, w3, w2)


def _final_norm_kernel(x_ref, g_ref, o_ref):
    x = x_ref[...]
    ms = jnp.mean(x * x, axis=-1, keepdims=True)
    o_ref[...] = x * lax.rsqrt(ms + NORM_EPS) * g_ref[...]


def final_norm(x, g, tm):
    m, d = x.shape
    return pl.pallas_call(
        _final_norm_kernel,
        grid=(m // tm,),
        in_specs=[pl.BlockSpec((tm, d), lambda i: (i, 0)), pl.BlockSpec((1, d), lambda i: (0, 0))],
        out_specs=pl.BlockSpec((tm, d), lambda i: (i, 0)),
        out_shape=jax.ShapeDtypeStruct((m, d), F32),
        compiler_params=_params("parallel"),
        name="final_norm",
    )(x, g.reshape(1, d))


def _rope_tables(pos):
    half = HEAD_DIM // 2
    inv = ROPE_BASE ** (-jnp.arange(half, dtype=F32) / half)
    ang = pos.astype(F32)[:, None] * inv[None, :]
    cos, sin = jnp.cos(ang), jnp.sin(ang)
    cos_row = jnp.tile(jnp.concatenate([cos, cos], axis=1), (1, N_HEADS))
    sin_row = jnp.tile(jnp.concatenate([-sin, sin], axis=1), (1, N_HEADS))
    return cos_row, sin_row


def _retention_tables(pos, chunk):
    lg = jnp.log1p(-(2.0 ** (-5.0 - jnp.arange(N_HEADS, dtype=F32))))
    idx = jnp.arange(chunk, dtype=F32)
    diff = idx[:, None] - idx[None, :]
    inner = jnp.where(diff >= 0, jnp.exp(lg[:, None, None] * jnp.maximum(diff, 0.0)), 0.0)
    q_decay = jnp.exp(lg[None, :] * (idx[:, None] + 1.0))
    k_decay = jnp.exp(lg[None, :] * (chunk - 1.0 - idx[:, None]))
    chunk_decay = jnp.exp(lg * chunk)
    lanes = lambda a: jnp.repeat(a, HEAD_DIM, axis=-1)
    cos, sin = _rope_tables(pos)
    return {"cos": cos, "sin": sin, "inner": inner, "q_decay": lanes(q_decay),
            "k_decay": lanes(k_decay), "chunk_decay": lanes(chunk_decay[None, :]),
            "gamma": chunk_decay}


def _pack_w_in(w_in_l):
    logit0 = N_MAIN
    n_logit = N_HEADS
    pad = jnp.zeros((w_in_l.shape[0], W_MIX - n_logit), w_in_l.dtype)
    return jnp.concatenate([w_in_l[:, :logit0], w_in_l[:, logit0:logit0 + n_logit], pad,
                            w_in_l[:, logit0 + n_logit:]], axis=1).astype(BF16)


def _block_diag_pool(w_pool_l):
    g, c, _ = w_pool_l.shape
    out = jnp.zeros((g * c, g * c), w_pool_l.dtype)
    for i in range(g):
        out = out.at[i * c:(i + 1) * c, i * c:(i + 1) * c].set(w_pool_l[i])
    return out.astype(BF16)


def _pad_lanes(row, n):
    return jnp.pad(row, ((0, 0), (0, n - row.shape[1])))


def _diag_blocks(s_bd):
    return jnp.stack([s_bd[:, h * HEAD_DIM:(h + 1) * HEAD_DIM, h * HEAD_DIM:(h + 1) * HEAD_DIM]
                      for h in range(N_HEADS)], axis=1)


def kernel(x_prompt, x_sample, cache_sb_kv, cache_fox_kv, cache_fox_logf, state_pool, state_ret,
           page_table, c_prompt, c_sample, w_ada, b_ada, norm_mix, w_in, w_pool, pool_scale,
           fox_bias, w_branch, w_out, norm_ffn, router_w, router_b, w1, w3, w2, norm_final):
    bp, tp, d = x_prompt.shape
    db, ts, _ = x_sample.shape
    assert ts == 1 and d == D_MODEL
    depth = w_in.shape[0]
    n_phys, _, _, page, _, _ = cache_sb_kv.shape
    n_pages = page_table.shape[1]
    past_len = n_pages * page
    mp = bp * tp

    mod_all = adaln(jnp.concatenate([c_prompt, c_sample], axis=0), w_ada, b_ada)

    sb_cache = cache_sb_kv.reshape(n_phys, depth, 2, page, W_MIX)
    fox_cache = cache_fox_kv.reshape(n_phys, depth, 2, page, W_MIX)
    logf_t = jnp.pad(jnp.swapaxes(cache_fox_logf, 2, 3), ((0, 0), (0, 0), (0, SUBLANES - N_HEADS), (0, 0)))
    state_flat = state_ret.reshape(db, depth * N_HEADS * HEAD_DIM * HEAD_DIM)
    router_w_pad = _pad_lanes(router_w, LANES)
    router_b_pad = _pad_lanes(router_b[None, :], LANES)
    tabs_p = _retention_tables(jnp.arange(tp), RET_CHUNK if tp % RET_CHUNK == 0 else tp)
    tabs_s = _retention_tables(past_len + jnp.arange(ts), ts)

    tm_p = min(1024, tp)
    tm_mrg = min(512, tp)
    xp = x_prompt.reshape(mp, d)
    xs = x_sample.reshape(db, d)
    outs_p = {k: [] for k in ("sb", "fox", "logf", "pool", "ret")}
    outs_s = {k: [] for k in ("sb", "fox", "logf", "pool", "ret")}

    for l in range(depth):
        w_cat = _pack_w_in(w_in[l])
        w_bd = _block_diag_pool(w_pool[l])
        scale_row = pool_scale[l][None, :]
        bias_lanes = _pad_lanes(fox_bias[l][None, :], LANES)
        bias_row = _pad_lanes(fox_bias[l][None, :], W_MIX)
        wb = w_branch[l].astype(BF16)
        wo = w_out[l].astype(BF16)
        w1l, w3l, w2l = w1[l].astype(BF16), w3[l].astype(BF16), w2[l].astype(BF16)

        mod_in = Mod(mod_all[l, :bp], tp, tm_p)
        mod_mrg = Mod(mod_all[l, :bp], tp, tm_mrg)
        proj = in_proj(xp, norm_mix[l], mod_in, w_cat, tm_p)
        pool_o = pool_prompt(proj, w_bd, scale_row, bp, tp)
        sb_o = sb_prompt(proj, bp, tp)
        ret_o, ret_state = ret_prompt(proj, tabs_p, bp, tp)
        logf, cum, cum_t = logf_cum_prompt(proj, bias_lanes, bp, tp)
        fox_o = fox_prompt(proj, cum, cum_t, bp, tp)
        xp = merge_out((pool_o, sb_o, ret_o, fox_o), proj, wb, wo, xp, mod_mrg, tm_mrg)
        xp = moe(xp, norm_ffn[l], mod_in, router_w_pad, router_b_pad, w1l, w3l, w2l, tm_p)
        p3 = proj.reshape(bp, tp, N_PROJ)

        def kv(cb, p3=p3):
            return jnp.stack([p3[:, :, cb * W_MIX:(cb + 1) * W_MIX],
                              p3[:, :, (cb + 1) * W_MIX:(cb + 2) * W_MIX]], axis=1)

        outs_p["sb"].append(kv(COL_SBK))
        outs_p["fox"].append(kv(COL_FK))
        outs_p["logf"].append(logf.reshape(bp, tp, LANES)[:, :, :N_HEADS])
        outs_p["pool"].append(p3[:, tp - POOL_BUF:, :W_MIX])
        outs_p["ret"].append(_diag_blocks(ret_state))

        mod_s = Mod(mod_all[l, bp:], 1, db)
        proj_s = in_proj(xs, norm_mix[l], mod_s, w_cat, db)
        buf_t = jnp.swapaxes(state_pool[:, l], 0, 1)
        pool_s = pool_sample(buf_t, proj_s, w_bd, scale_row, past_len)
        sb_s, fox_s, logf_s = decode_attn(page_table, proj_s, sb_cache, fox_cache, logf_t, bias_row, l)
        ret_s, state_new = ret_sample(proj_s, state_flat, l, tabs_s)
        xs = merge_out((pool_s, sb_s, ret_s, fox_s), proj_s, wb, wo, xs, mod_s, db)
        xs = moe(xs, norm_ffn[l], mod_s, router_w_pad, router_b_pad, w1l, w3l, w2l, db)

        def kv_s(cb, proj_s=proj_s):
            return jnp.stack([proj_s[:, cb * W_MIX:(cb + 1) * W_MIX],
                              proj_s[:, (cb + 1) * W_MIX:(cb + 2) * W_MIX]], axis=1)

        outs_s["sb"].append(kv_s(COL_SBK))
        outs_s["fox"].append(kv_s(COL_FK))
        outs_s["logf"].append(logf_s[:, :N_HEADS])
        outs_s["pool"].append(jnp.concatenate([state_pool[:, l, 1:], proj_s[:, None, :W_MIX]], axis=1))
        outs_s["ret"].append(state_new)

    y_p = final_norm(xp, norm_final, tm_p).reshape(bp, tp, d)
    y_s = final_norm(xs, norm_final, db).reshape(db, ts, d)

    def heads(a, t):
        return a.reshape(a.shape[0], depth, 2, t, N_HEADS, HEAD_DIM)

    return (y_p, y_s,
            heads(jnp.stack(outs_p["sb"], axis=1), tp),
            heads(jnp.stack(outs_s["sb"], axis=1), ts),
            heads(jnp.stack(outs_p["fox"], axis=1), tp),
            heads(jnp.stack(outs_s["fox"], axis=1), ts),
            jnp.stack(outs_p["logf"], axis=1),
            jnp.stack(outs_s["logf"], axis=1).reshape(db, depth, ts, N_HEADS),
            jnp.stack(outs_p["pool"], axis=1),
            jnp.stack(outs_s["pool"], axis=1),
            jnp.stack(outs_p["ret"], axis=1),
            jnp.stack(outs_s["ret"], axis=1).reshape(db, depth, N_HEADS, HEAD_DIM, HEAD_DIM))
```

```python
import functools

import numpy as np
import jax
import jax.numpy as jnp
from jax import lax
from jax.experimental import pallas as pl
from jax.experimental.pallas import tpu as pltpu

F32 = jnp.float32
BF16 = jnp.bfloat16

D_MODEL = 1024
HEAD_DIM = 64
W_MIX = 256
N_HEADS = W_MIX // HEAD_DIM
N_BRANCH = 4
POOL_WINDOWS = (2, 4, 8, 16)
POOL_BUF = 15
RET_CHUNK = 128
ROPE_BASE = 10000.0
N_EXPERTS = 16
EXPERTS_PER_GROUP = 4
D_EXPERT = 512
N_MOD = 6
NORM_EPS = 1e-6
ATTN_SCALE = HEAD_DIM ** -0.5
LANES = 128
SUBLANES = 8
VMEM_LIMIT = 56 * 1024 * 1024

COL_U, COL_SBQ, COL_SBK, COL_SBV = 0, 1, 2, 3
COL_RQ, COL_RK, COL_RV, COL_RG = 4, 5, 6, 7
COL_FQ, COL_FK, COL_FV, COL_LOGIT = 8, 9, 10, 11
N_MAIN = 11 * W_MIX
GATE_COL0 = 12 * W_MIX
N_PROJ = GATE_COL0 + N_BRANCH * D_MODEL


def _params(*sem):
    return pltpu.CompilerParams(dimension_semantics=sem, vmem_limit_bytes=VMEM_LIMIT)


def _split2(x):
    hi = x.astype(BF16)
    lo = (x - hi.astype(F32)).astype(BF16)
    return hi, lo


def _split3(x):
    hi = x.astype(BF16)
    r = x - hi.astype(F32)
    mid = r.astype(BF16)
    lo = (r - mid.astype(F32)).astype(BF16)
    return hi, mid, lo


def _dot(a, b):
    return jnp.dot(a, b, preferred_element_type=F32)


def _dot_nt(a, b):
    return lax.dot_general(a, b, (((1,), (1,)), ((), ())), preferred_element_type=F32)


def _dot_tn(a, b):
    return lax.dot_general(a, b, (((0,), (0,)), ((), ())), preferred_element_type=F32)


def _dot_split(x, w, parts=2):
    ps = _split2(x) if parts == 2 else _split3(x)
    acc = _dot(ps[0], w)
    for p in ps[1:]:
        acc = acc + _dot(p, w)
    return acc


def _iota(shape, dim):
    return lax.broadcasted_iota(jnp.int32, shape, dim)


def _softplus(z):
    return jnp.maximum(z, 0.0) + jnp.log1p(jnp.exp(-jnp.abs(z)))


def _silu(x):
    return x * jax.nn.sigmoid(x)


def _adaln_kernel(c_ref, w_ref, b_ref, o_ref):
    cond = _silu(c_ref[...]).astype(BF16)
    o_ref[...] = _dot(cond, w_ref[...].astype(BF16)) + b_ref[...]


def adaln(c_all, w_ada, b_ada):
    rows, d = c_all.shape
    depth, _, n = w_ada.shape
    tn = 1024
    return pl.pallas_call(
        _adaln_kernel,
        grid=(depth, n // tn),
        in_specs=[pl.BlockSpec((rows, d), lambda l, j: (0, 0)),
                  pl.BlockSpec((None, d, tn), lambda l, j: (l, 0, j)),
                  pl.BlockSpec((None, 1, tn), lambda l, j: (l, 0, j))],
        out_specs=pl.BlockSpec((None, rows, tn), lambda l, j: (l, 0, j)),
        out_shape=jax.ShapeDtypeStruct((depth, rows, n), F32),
        compiler_params=_params("parallel", "parallel"),
        name="adaln",
    )(c_all, w_ada, b_ada.reshape(depth, 1, n))


class Mod:
    def __init__(self, arr, rows_per_vec, tm):
        self.per_row = rows_per_vec == 1
        self.arr = arr if self.per_row else arr.reshape(arr.shape[0], 1, arr.shape[-1])
        self.tiles_per_vec = 1 if self.per_row else rows_per_vec // tm
        self.tm = tm

    def spec(self, k):
        if self.per_row:
            return pl.BlockSpec((self.tm, D_MODEL), lambda i, *_: (i, k))
        t = self.tiles_per_vec
        return pl.BlockSpec((None, 1, D_MODEL), lambda i, *_: (i // t, 0, k))


def _norm_mod(x, g, sc, sh):
    ms = jnp.mean(x * x, axis=-1, keepdims=True)
    y = x * lax.rsqrt(ms + NORM_EPS) * g
    return y * (1.0 + sc) + sh


def _inproj_kernel(x_ref, g_ref, sh_ref, sc_ref, w_ref, o_ref, h_scr):
    @pl.when(pl.program_id(1) == 0)
    def _():
        h_scr[...] = _norm_mod(x_ref[...], g_ref[...], sc_ref[...], sh_ref[...]).astype(BF16)

    o_ref[...] = _dot(h_scr[...], w_ref[...])


def in_proj(x, g, mod, w_all, layer, tm):
    m, d = x.shape
    n = w_all.shape[2]
    tn = 1024
    return pl.pallas_call(
        _inproj_kernel,
        grid=(m // tm, n // tn),
        in_specs=[pl.BlockSpec((tm, d), lambda i, j: (i, 0)),
                  pl.BlockSpec((1, d), lambda i, j: (0, 0)),
                  mod.spec(0), mod.spec(1),
                  pl.BlockSpec((None, d, tn), lambda i, j: (layer, 0, j))],
        out_specs=pl.BlockSpec((tm, tn), lambda i, j: (i, j)),
        out_shape=jax.ShapeDtypeStruct((m, n), F32),
        scratch_shapes=[pltpu.VMEM((tm, d), BF16)],
        compiler_params=_params("parallel", "arbitrary"),
        name="in_proj",
    )(x, g.reshape(1, d), mod.arr, mod.arr, w_all)


def _pool_tail(s2, s4, s8, s16, u, pos0, w_ref, sc_ref, o_ref):
    t = u.shape[0]
    lane = _iota((1, W_MIX), 1) // (W_MIX // len(POOL_WINDOWS))
    win = jnp.where(lane == 0, s2, jnp.where(lane == 1, s4, jnp.where(lane == 2, s8, s16)))
    width = jnp.where(lane == 0, 2, jnp.where(lane == 1, 4, jnp.where(lane == 2, 8, 16)))
    count = jnp.minimum(width, pos0 + 1 + _iota((t, W_MIX), 0)).astype(F32)
    resid = win / count - u
    o_ref[...] = (_dot(resid.astype(BF16), w_ref[...]) * sc_ref[...]).astype(o_ref.dtype)


_POOL_PAD = 32


def _pool_prompt_kernel(u_ref, w_ref, sc_ref, o_ref, a_scr, b_scr, c_scr):
    t = u_ref.shape[0]
    p = _POOL_PAD
    u = u_ref[...]
    a_scr[0:p, :] = jnp.zeros((p, W_MIX), F32)
    a_scr[p:p + t, :] = u

    def stage(src, dst, k, lo):
        n = p + t - lo
        dst[lo:lo + n, :] = src[lo:lo + n, :] + src[lo - k:lo - k + n, :]

    stage(a_scr, b_scr, 1, 8)
    stage(b_scr, c_scr, 2, 16)
    stage(c_scr, a_scr, 4, 24)
    s8 = a_scr[p:p + t, :]
    s16 = s8 + a_scr[p - 8:p - 8 + t, :]
    _pool_tail(b_scr[p:p + t, :], c_scr[p:p + t, :], s8, s16, u, 0, w_ref, sc_ref, o_ref)


def pool_prompt(proj, w_bd, scale, b, t):
    return pl.pallas_call(
        _pool_prompt_kernel,
        grid=(b,),
        in_specs=[pl.BlockSpec((t, W_MIX), lambda i: (i, COL_U)),
                  pl.BlockSpec((W_MIX, W_MIX), lambda i: (0, 0)),
                  pl.BlockSpec((1, W_MIX), lambda i: (0, 0))],
        out_specs=pl.BlockSpec((t, W_MIX), lambda i: (i, 0)),
        out_shape=jax.ShapeDtypeStruct((b * t, W_MIX), BF16),
        scratch_shapes=[pltpu.VMEM((t + _POOL_PAD, W_MIX), F32)] * 3,
        compiler_params=_params("parallel"),
        name="pool_prompt",
    )(proj, w_bd, scale)


def _pool_sample_kernel(buf_ref, u_ref, w_ref, sc_ref, o_ref, *, pos0):
    u = u_ref[...]
    s2 = u + buf_ref[14]
    s4 = s2 + buf_ref[13] + buf_ref[12]
    s8 = s4 + buf_ref[11] + buf_ref[10] + buf_ref[9] + buf_ref[8]
    s16 = s8
    for r in range(7, -1, -1):
        s16 = s16 + buf_ref[r]
    _pool_tail(s2, s4, s8, s16, u, pos0, w_ref, sc_ref, o_ref)


def pool_sample(buf_t, proj, w_bd, scale, pos0):
    rows = proj.shape[0]
    return pl.pallas_call(
        functools.partial(_pool_sample_kernel, pos0=pos0),
        grid=(1,),
        in_specs=[pl.BlockSpec((POOL_BUF, rows, W_MIX), lambda i: (0, 0, 0)),
                  pl.BlockSpec((rows, W_MIX), lambda i: (0, COL_U)),
                  pl.BlockSpec((W_MIX, W_MIX), lambda i: (0, 0)),
                  pl.BlockSpec((1, W_MIX), lambda i: (0, 0))],
        out_specs=pl.BlockSpec((rows, W_MIX), lambda i: (0, 0)),
        out_shape=jax.ShapeDtypeStruct((rows, W_MIX), BF16),
        compiler_params=_params("arbitrary"),
        name="pool_sample",
    )(buf_t, proj, w_bd, scale)


_ATT_TILE = 256


def _head_of_lane():
    return _iota((1, W_MIX), 1) // HEAD_DIM


def _load_kv_once(k_ref, v_ref, k_scr, v_scr):
    @pl.when(pl.program_id(1) == 0)
    def _():
        k_scr[...] = k_ref[...].astype(BF16)
        v_scr[...] = v_ref[...].astype(BF16)


def _sb_prompt_kernel(q_ref, k_ref, v_ref, o_ref, k_scr, v_scr):
    tq = tk = _ATT_TILE
    i = pl.program_id(1)
    _load_kv_once(k_ref, v_ref, k_scr, v_scr)
    q = q_ref[...] * ATTN_SCALE
    head = _head_of_lane()
    row = i * tq + _iota((tq, tk), 0)
    col = _iota((tq, tk), 1)
    upper = (_iota((tk, tk), 0) > _iota((tk, tk), 1)).astype(BF16)
    out = jnp.zeros((tq, W_MIX), F32)
    for h in range(N_HEADS):
        qh = jnp.where(head == h, q, 0.0).astype(BF16)

        def body(s, carry, qh=qh):
            acc, run = carry
            off = pl.multiple_of((i - s) * tk, tk)
            kb = k_scr[pl.ds(off, tk), :]
            vb = v_scr[pl.ds(off, tk), :]
            z = _dot_nt(qh, kb)
            valid = (off + col) < row
            sp = _softplus(z)
            log_not = jnp.where(valid, -sp, 0.0)
            suffix = _dot_split(log_not, upper)
            a = jnp.where(valid, jnp.exp(z - sp + suffix + run), 0.0)
            acc = acc + _dot(a.astype(BF16), vb)
            run = run + suffix[:, 0:1] + log_not[:, 0:1]
            return acc, run

        acc, _ = lax.fori_loop(0, i + 1, body,
                               (jnp.zeros((tq, W_MIX), F32), jnp.zeros((tq, 1), F32)))
        out = out + jnp.where(head == h, acc, 0.0)
    o_ref[...] = out.astype(o_ref.dtype)


def _attn_prompt_specs(t, col_q, col_k, col_v):
    tq = _ATT_TILE
    nq = t // tq
    return [pl.BlockSpec((tq, W_MIX), lambda b, i: (b * nq + i, col_q)),
            pl.BlockSpec((t, W_MIX), lambda b, i: (b, col_k)),
            pl.BlockSpec((t, W_MIX), lambda b, i: (b, col_v))]


def sb_prompt(proj, b, t):
    tq = _ATT_TILE
    nq = t // tq
    return pl.pallas_call(
        _sb_prompt_kernel,
        grid=(b, nq),
        in_specs=_attn_prompt_specs(t, COL_SBQ, COL_SBK, COL_SBV),
        out_specs=pl.BlockSpec((tq, W_MIX), lambda bb, i: (bb * nq + i, 0)),
        out_shape=jax.ShapeDtypeStruct((b * t, W_MIX), BF16),
        scratch_shapes=[pltpu.VMEM((t, W_MIX), BF16)] * 2,
        compiler_params=_params("parallel", "arbitrary"),
        name="sb_prompt",
    )(proj, proj, proj)


def _logf_cum_kernel(x_ref, bias_ref, logf_ref, cum_ref, cumt_ref):
    t = x_ref.shape[0]
    blk = 256
    logf = jax.nn.log_sigmoid(x_ref[...] + bias_ref[...])
    logf_ref[...] = logf
    lower = (_iota((blk, blk), 1) <= _iota((blk, blk), 0)).astype(BF16)
    carry = jnp.zeros((1, LANES), F32)
    for c in range(t // blk):
        part = logf[c * blk:(c + 1) * blk]
        hi, mid, lo = _split3(part)
        cum = _dot(lower, hi) + _dot(lower, mid) + _dot(lower, lo) + carry
        cum_ref[c * blk:(c + 1) * blk, :] = cum
        carry = cum[blk - 1:blk, :]
    cumt_ref[...] = jnp.transpose(cum_ref[...])[0:SUBLANES, :]


def logf_cum_prompt(proj, bias_pad, b, t):
    m = b * t
    return pl.pallas_call(
        _logf_cum_kernel,
        grid=(b,),
        in_specs=[pl.BlockSpec((t, LANES), lambda i: (i, COL_LOGIT * W_MIX // LANES)),
                  pl.BlockSpec((1, LANES), lambda i: (0, 0))],
        out_specs=[pl.BlockSpec((t, LANES), lambda i: (i, 0)),
                   pl.BlockSpec((t, LANES), lambda i: (i, 0)),
                   pl.BlockSpec((None, SUBLANES, t), lambda i: (i, 0, 0))],
        out_shape=[jax.ShapeDtypeStruct((m, LANES), F32),
                   jax.ShapeDtypeStruct((m, LANES), F32),
                   jax.ShapeDtypeStruct((b, SUBLANES, t), F32)],
        compiler_params=_params("parallel"),
        name="logf_cum",
    )(proj, bias_pad)


def _fox_prompt_kernel(q_ref, k_ref, v_ref, cq_ref, ck_ref, o_ref, k_scr, v_scr):
    tq = tk = _ATT_TILE
    i = pl.program_id(1)
    _load_kv_once(k_ref, v_ref, k_scr, v_scr)
    q = q_ref[...] * ATTN_SCALE
    head = _head_of_lane()
    row = i * tq + _iota((tq, tk), 0)
    col = _iota((tq, tk), 1)
    cq = cq_ref[...]
    out = jnp.zeros((tq, W_MIX), F32)
    for h in range(N_HEADS):
        qh = jnp.where(head == h, q, 0.0).astype(BF16)
        cqh = cq[:, h:h + 1]

        def body(s, carry, qh=qh, cqh=cqh, h=h):
            acc, m, l = carry
            off = pl.multiple_of((i - s) * tk, tk)
            kb = k_scr[pl.ds(off, tk), :]
            vb = v_scr[pl.ds(off, tk), :]
            ckh = ck_ref[h:h + 1, pl.ds(off, tk)]
            sc = _dot_nt(qh, kb) + (cqh - ckh)
            sc = jnp.where((off + col) <= row, sc, -jnp.inf)
            m_new = jnp.maximum(m, jnp.max(sc, axis=1, keepdims=True))
            alpha = jnp.exp(m - m_new)
            p = jnp.exp(sc - m_new)
            l = alpha * l + jnp.sum(p, axis=1, keepdims=True)
            acc = alpha * acc + _dot(p.astype(BF16), vb)
            return acc, m_new, l

        acc, _, l = lax.fori_loop(0, i + 1, body,
                                  (jnp.zeros((tq, W_MIX), F32),
                                   jnp.full((tq, 1), -jnp.inf, F32),
                                   jnp.zeros((tq, 1), F32)))
        out = out + jnp.where(head == h, acc / l, 0.0)
    o_ref[...] = out.astype(o_ref.dtype)


def fox_prompt(proj, cum, cum_t, b, t):
    tq = _ATT_TILE
    nq = t // tq
    specs = _attn_prompt_specs(t, COL_FQ, COL_FK, COL_FV)
    specs += [pl.BlockSpec((tq, LANES), lambda bb, i: (bb * nq + i, 0)),
              pl.BlockSpec((None, SUBLANES, t), lambda bb, i: (bb, 0, 0))]
    return pl.pallas_call(
        _fox_prompt_kernel,
        grid=(b, nq),
        in_specs=specs,
        out_specs=pl.BlockSpec((tq, W_MIX), lambda bb, i: (bb * nq + i, 0)),
        out_shape=jax.ShapeDtypeStruct((b * t, W_MIX), BF16),
        scratch_shapes=[pltpu.VMEM((t, W_MIX), BF16)] * 2,
        compiler_params=_params("parallel", "arbitrary"),
        name="fox_prompt",
    )(proj, proj, proj, cum, cum_t)


def _rotary(x, cos, sin_signed):
    half = HEAD_DIM // 2
    first = (_iota((1, W_MIX), 1) % HEAD_DIM) < half
    swapped = jnp.where(first, pltpu.roll(x, W_MIX - half, axis=1), pltpu.roll(x, half, axis=1))
    return x * cos + swapped * sin_signed


def _head_mean_matrix():
    same = (_iota((W_MIX, W_MIX), 0) // HEAD_DIM) == (_iota((W_MIX, W_MIX), 1) // HEAD_DIM)
    return jnp.where(same, 1.0 / HEAD_DIM, 0.0).astype(BF16)


def _head_norm_gate(o, gate_logit):
    avg = _head_mean_matrix()
    mu = _dot_split(o, avg)
    dev = o - mu
    var = _dot_split(dev * dev, avg)
    return _silu(gate_logit) * (dev * lax.rsqrt(var + NORM_EPS))


def _ret_prompt_kernel(q_ref, k_ref, v_ref, g_ref, cos_ref, sin_ref, inner_ref, qd_ref, kd_ref,
                       cd_ref, o_ref, s_ref):
    c = pl.program_id(1)

    @pl.when(c == 0)
    def _():
        s_ref[...] = jnp.zeros_like(s_ref)

    cos = cos_ref[...]
    sin = sin_ref[...]
    q = _rotary(q_ref[...], cos, sin)
    k = _rotary(k_ref[...], cos, sin) * ATTN_SCALE
    qb = q.astype(BF16)
    kb = k.astype(BF16)
    vb = v_ref[...].astype(BF16)
    head = _head_of_lane()
    state = s_ref[...]
    o = _dot(qb, state.astype(BF16)) * qd_ref[...]
    for h in range(N_HEADS):
        qh = jnp.where(head == h, q, 0.0).astype(BF16)
        att = _dot_nt(qh, kb) * inner_ref[h]
        o = o + jnp.where(head == h, _dot(att.astype(BF16), vb), 0.0)
    same = (_iota((W_MIX, W_MIX), 0) // HEAD_DIM) == (_iota((W_MIX, W_MIX), 1) // HEAD_DIM)
    kv = _dot_tn((k * kd_ref[...]).astype(BF16), vb)
    s_ref[...] = state * cd_ref[...] + jnp.where(same, kv, 0.0)
    o_ref[...] = _head_norm_gate(o, g_ref[...]).astype(o_ref.dtype)


def ret_prompt(proj, tabs, b, t):
    ch = RET_CHUNK
    nc = t // ch

    def col(cb):
        return pl.BlockSpec((ch, W_MIX), lambda bb, c: (bb * nc + c, cb))

    full = lambda shape: pl.BlockSpec(shape, lambda bb, c: (0,) * len(shape))
    return pl.pallas_call(
        _ret_prompt_kernel,
        grid=(b, nc),
        in_specs=[col(COL_RQ), col(COL_RK), col(COL_RV), col(COL_RG),
                  pl.BlockSpec((ch, W_MIX), lambda bb, c: (c, 0)),
                  pl.BlockSpec((ch, W_MIX), lambda bb, c: (c, 0)),
                  full((N_HEADS, ch, ch)), full((ch, W_MIX)), full((ch, W_MIX)), full((1, W_MIX))],
        out_specs=[pl.BlockSpec((ch, W_MIX), lambda bb, c: (bb * nc + c, 0)),
                   pl.BlockSpec((None, W_MIX, W_MIX), lambda bb, c: (bb, 0, 0))],
        out_shape=[jax.ShapeDtypeStruct((b * t, W_MIX), BF16),
                   jax.ShapeDtypeStruct((b, W_MIX, W_MIX), F32)],
        compiler_params=_params("parallel", "arbitrary"),
        name="ret_prompt",
    )(proj, proj, proj, proj, tabs["cos"], tabs["sin"], tabs["inner"], tabs["q_decay"],
      tabs["k_decay"], tabs["chunk_decay"])


def _ret_sample_kernel(gam_ref, q_ref, k_ref, v_ref, g_ref, cos_ref, sin_ref, s_ref,
                       o_ref, sn_ref, qt_scr, kt_scr, vt_scr, ot_scr):
    h = pl.program_id(0)
    rows = q_ref.shape[0]

    @pl.when(h == 0)
    def _():
        cos = cos_ref[...]
        sin = sin_ref[...]
        qt_scr[...] = jnp.transpose(_rotary(q_ref[...], cos, sin))
        kt_scr[...] = jnp.transpose(_rotary(k_ref[...], cos, sin) * ATTN_SCALE)
        vt_scr[...] = jnp.transpose(v_ref[...])

    row0 = pl.multiple_of(h * HEAD_DIM, HEAD_DIM)
    gamma = gam_ref[h]
    vh = vt_scr[pl.ds(row0, HEAD_DIM), :]

    def body(d, cross):
        qd = qt_scr[pl.ds(row0 + d, 1), :]
        kd = kt_scr[pl.ds(row0 + d, 1), :]
        s_d = s_ref[d]
        sn_ref[d] = gamma * s_d + kd * vh
        return cross + qd * s_d

    cross = lax.fori_loop(0, HEAD_DIM, body, jnp.zeros((HEAD_DIM, rows), F32))
    qk = jnp.sum(qt_scr[pl.ds(row0, HEAD_DIM), :] * kt_scr[pl.ds(row0, HEAD_DIM), :],
                 axis=0, keepdims=True)
    ot_scr[pl.ds(row0, HEAD_DIM), :] = qk * vh + cross * gamma

    @pl.when(h == N_HEADS - 1)
    def _():
        o_ref[...] = _head_norm_gate(jnp.transpose(ot_scr[...]), g_ref[...])


def ret_sample(proj, state_t, layer, tabs):
    rows = proj.shape[0]

    def col(cb):
        return pl.BlockSpec((rows, W_MIX), lambda h: (0, cb))

    row_tab = pl.BlockSpec((1, W_MIX), lambda h: (0, 0))
    state_blk = (None, HEAD_DIM, HEAD_DIM, rows)
    return pl.pallas_call(
        _ret_sample_kernel,
        grid=(N_HEADS,),
        in_specs=[pl.BlockSpec(memory_space=pltpu.SMEM),
                  col(COL_RQ), col(COL_RK), col(COL_RV), col(COL_RG), row_tab, row_tab,
                  pl.BlockSpec(state_blk, lambda h: (layer * N_HEADS + h, 0, 0, 0))],
        out_specs=[pl.BlockSpec((rows, W_MIX), lambda h: (0, 0)),
                   pl.BlockSpec(state_blk, lambda h: (h, 0, 0, 0))],
        out_shape=[jax.ShapeDtypeStruct((rows, W_MIX), F32),
                   jax.ShapeDtypeStruct((N_HEADS, HEAD_DIM, HEAD_DIM, rows), F32)],
        scratch_shapes=[pltpu.VMEM((W_MIX, rows), F32)] * 4,
        compiler_params=_params("arbitrary"),
        name="ret_sample",
    )(tabs["gamma"], proj, proj, proj, proj, tabs["cos"], tabs["sin"], state_t)


def _suffix_sums(x3, upper):
    n_pages, rows, page = x3.shape
    flat = x3.reshape(n_pages * rows, page)
    within = _dot_split(flat, upper)
    total = (within[:, 0:1] + flat[:, 0:1]).reshape(n_pages, rows, 1)
    within = within.reshape(n_pages, rows, page)
    carry = jnp.zeros((rows, 1), F32)
    outs = [None] * n_pages
    for j in range(n_pages - 1, -1, -1):
        outs[j] = within[j] + carry
        carry = carry + total[j]
    return jnp.stack(outs)


def _rows_to_col(row):
    pick = _iota((SUBLANES, W_MIX), 0) == _iota((SUBLANES, W_MIX), 1)
    return jnp.sum(jnp.where(pick, row, 0.0), axis=1, keepdims=True)


def _decode_kernel(pt_ref, qs_ref, qf_ref, kf_ref, vf_ref, lg_ref, bias_ref, *refs, n_pages):
    sb_pages = refs[0:n_pages]
    fx_pages = refs[n_pages:2 * n_pages]
    lf_pages = refs[2 * n_pages:3 * n_pages]
    sb_o, fx_o, lf_o, lf_scr = refs[3 * n_pages:]
    page = sb_pages[0].shape[2]
    head_rows = _iota((SUBLANES, W_MIX), 0) == (_iota((SUBLANES, W_MIX), 1) // HEAD_DIM)
    upper = (_iota((page, page), 0) > _iota((page, page), 1)).astype(BF16)

    def per_head_rows(row):
        return jnp.where(head_rows, row, 0.0)

    def collapse(x):
        return jnp.sum(jnp.where(head_rows, x, 0.0), axis=0, keepdims=True)

    qs = per_head_rows(qs_ref[...] * ATTN_SCALE).astype(BF16)
    z = jnp.stack([_dot(qs, sb_pages[j][0].astype(BF16)) for j in range(n_pages)])
    sp = _softplus(z)
    after = _suffix_sums(-sp, upper)
    a = jnp.exp(z - sp + after).astype(BF16)
    acc = jnp.zeros((SUBLANES, W_MIX), F32)
    for j in range(n_pages):
        acc = acc + _dot_nt(a[j], sb_pages[j][1].astype(BF16))
    sb_o[...] = collapse(acc)

    logf_new = jax.nn.log_sigmoid(lg_ref[...] + bias_ref[...])
    lf_o[...] = logf_new
    qf = per_head_rows(qf_ref[...] * ATTN_SCALE).astype(BF16)
    lf_scr[...] = jnp.zeros_like(lf_scr)
    for j in range(n_pages):
        lf_scr[j, 0:N_HEADS, :] = lf_pages[j][...]
    decay = _suffix_sums(lf_scr[...], upper)
    s = jnp.stack([_dot(qf, fx_pages[j][0].astype(BF16)) for j in range(n_pages)])
    s = s + decay + _rows_to_col(logf_new)
    k_new = jnp.broadcast_to(kf_ref[...], (SUBLANES, W_MIX)).astype(BF16)
    s_self = _dot_nt(qf, k_new)[:, 0:1]
    m = jnp.max(jnp.max(s, axis=0), axis=1, keepdims=True)
    m = jnp.maximum(m, s_self)
    p = jnp.exp(s - m)
    p_self = jnp.exp(s_self - m)
    l = jnp.sum(jnp.sum(p, axis=0), axis=1, keepdims=True) + p_self
    pb = (p / l).astype(BF16)
    acc = (p_self / l).astype(BF16).astype(F32) * vf_ref[...].astype(BF16).astype(F32)
    for j in range(n_pages):
        acc = acc + _dot_nt(pb[j], fx_pages[j][1].astype(BF16))
    fx_o[...] = collapse(acc)


def decode_attn(page_table, proj, sb_cache, fox_cache, logf_t, bias_row, layer):
    rows, n_pages = page_table.shape
    page = sb_cache.shape[4]
    proj3 = proj.reshape(rows, 1, proj.shape[1])

    def col(cb):
        return pl.BlockSpec((None, 1, W_MIX), lambda b, pt: (b, 0, cb))

    def kv_page(j):
        return pl.BlockSpec((None, None, 2, W_MIX, page),
                            lambda b, pt, j=j: (pt[b * n_pages + j], layer, 0, 0, 0))

    def lf_page(j):
        return pl.BlockSpec((None, None, N_HEADS, page),
                            lambda b, pt, j=j: (pt[b * n_pages + j], layer, 0, 0))

    out_row = pl.BlockSpec((None, 1, W_MIX), lambda b, pt: (b, 0, 0))
    grid_spec = pltpu.PrefetchScalarGridSpec(
        num_scalar_prefetch=1,
        grid=(rows,),
        in_specs=[col(COL_SBQ), col(COL_FQ), col(COL_FK), col(COL_FV), col(COL_LOGIT),
                  pl.BlockSpec((1, W_MIX), lambda b, pt: (0, 0))]
                 + [kv_page(j) for j in range(n_pages)]
                 + [kv_page(j) for j in range(n_pages)]
                 + [lf_page(j) for j in range(n_pages)],
        out_specs=[out_row, out_row, out_row],
        scratch_shapes=[pltpu.VMEM((n_pages, SUBLANES, page), F32)],
    )
    sb_o, fx_o, lf_o = pl.pallas_call(
        functools.partial(_decode_kernel, n_pages=n_pages),
        grid_spec=grid_spec,
        out_shape=[jax.ShapeDtypeStruct((rows, 1, W_MIX), F32)] * 3,
        compiler_params=_params("arbitrary"),
        name="decode_attn",
    )(page_table.reshape(-1), proj3, proj3, proj3, proj3, proj3, bias_row,
      *([sb_cache] * n_pages), *([fox_cache] * n_pages), *([logf_t] * n_pages))
    return sb_o.reshape(rows, W_MIX), fx_o.reshape(rows, W_MIX), lf_o.reshape(rows, W_MIX)


def _merge_kernel(b0_ref, b1_ref, b2_ref, b3_ref, g0_ref, g1_ref, g2_ref, g3_ref, wb_ref, wo_ref,
                  x_ref, gate_ref, o_ref):
    merged = None
    for br, gl, i in ((b0_ref, g0_ref, 0), (b1_ref, g1_ref, 1), (b2_ref, g2_ref, 2), (b3_ref, g3_ref, 3)):
        y = jax.nn.sigmoid(gl[...]) * _dot(br[...].astype(BF16), wb_ref[i])
        merged = y if merged is None else merged + y
    out = _dot(merged.astype(BF16), wo_ref[...])
    o_ref[...] = x_ref[...] + gate_ref[...] * out


def merge_out(branches, proj, w_branch, w_out, x, mod, tm):
    m, d = x.shape
    gate0 = GATE_COL0 // d
    br_spec = pl.BlockSpec((tm, W_MIX), lambda i: (i, 0))
    return pl.pallas_call(
        _merge_kernel,
        grid=(m // tm,),
        in_specs=[br_spec] * N_BRANCH
                 + [pl.BlockSpec((tm, d), lambda i, k=k: (i, gate0 + k)) for k in range(N_BRANCH)]
                 + [pl.BlockSpec((N_BRANCH, W_MIX, d), lambda i: (0, 0, 0)),
                    pl.BlockSpec((d, d), lambda i: (0, 0)),
                    pl.BlockSpec((tm, d), lambda i: (i, 0)),
                    mod.spec(2)],
        out_specs=pl.BlockSpec((tm, d), lambda i: (i, 0)),
        out_shape=jax.ShapeDtypeStruct((m, d), F32),
        compiler_params=_params("parallel"),
        name="merge_out",
    )(*branches, proj, proj, proj, proj, w_branch, w_out, x, mod.arr)


def _route(scores, biased):
    lane_i = _iota(scores.shape, 1)
    real = lane_i < N_EXPERTS
    pos = lane_i % EXPERTS_PER_GROUP
    lane = lane_i.astype(F32)
    group = (lane_i // EXPERTS_PER_GROUP).astype(F32)
    neg = -jnp.inf
    n = scores.shape[1]
    mates = [biased]
    for k in range(1, EXPERTS_PER_GROUP):
        fwd = pltpu.roll(biased, n - k, axis=1)
        back = pltpu.roll(biased, EXPERTS_PER_GROUP - k, axis=1)
        mates.append(jnp.where(pos + k < EXPERTS_PER_GROUP, fwd, back))
    group_score = None
    for a in range(EXPERTS_PER_GROUP):
        for b in range(a + 1, EXPERTS_PER_GROUP):
            pair = mates[a] + mates[b]
            group_score = pair if group_score is None else jnp.maximum(group_score, pair)
    group_score = jnp.where(real, group_score, neg)
    best_score = jnp.max(group_score, axis=1, keepdims=True)
    best = jnp.min(jnp.where(group_score == best_score, group, float(n)), axis=1, keepdims=True)
    cand = jnp.where(real & (group == best), biased, neg)
    top0 = jnp.max(cand, axis=1, keepdims=True)
    idx0 = jnp.min(jnp.where(cand == top0, lane, float(n)), axis=1, keepdims=True)
    cand = jnp.where(lane == idx0, neg, cand)
    top1 = jnp.max(cand, axis=1, keepdims=True)
    idx1 = jnp.min(jnp.where(cand == top1, lane, float(n)), axis=1, keepdims=True)
    chosen = (lane == idx0) | (lane == idx1)
    sel = jnp.where(chosen, scores, 0.0)
    return sel / jnp.sum(sel, axis=1, keepdims=True)


def _moe_kernel(x_ref, g_ref, sh_ref, sc_ref, gate_ref, rw_ref, rb_ref, w1_ref, w3_ref, w2_ref,
                o_ref, h_scr, comb_scr, acc_scr):
    e = pl.program_id(1)

    @pl.when(e == 0)
    def _():
        h = _norm_mod(x_ref[...], g_ref[...], sc_ref[...], sh_ref[...])
        h_scr[...] = h.astype(BF16)
        scores = jax.nn.sigmoid(_dot(h_scr[...], rw_ref[...].astype(BF16)))
        comb_scr[...] = _route(scores, scores + rb_ref[...])
        acc_scr[...] = jnp.zeros_like(acc_scr)

    h = h_scr[...]
    a = _dot(h, w1_ref[...])
    b = _dot(h, w3_ref[...])
    y = _dot((_silu(a) * b).astype(BF16), w2_ref[...])
    lane = _iota(comb_scr.shape, 1)
    w_e = jnp.sum(jnp.where(lane == e, comb_scr[...], 0.0), axis=1, keepdims=True)
    acc_scr[...] = acc_scr[...] + w_e * y

    @pl.when(e == N_EXPERTS - 1)
    def _():
        o_ref[...] = x_ref[...] + gate_ref[...] * acc_scr[...]


def moe(x, g, mod, router_w_pad, router_b_pad, w1, w3, w2, tm):
    m, d = x.shape
    f = w1.shape[-1]
    return pl.pallas_call(
        _moe_kernel,
        grid=(m // tm, N_EXPERTS),
        in_specs=[pl.BlockSpec((tm, d), lambda i, e: (i, 0)),
                  pl.BlockSpec((1, d), lambda i, e: (0, 0)),
                  mod.spec(3), mod.spec(4), mod.spec(5),
                  pl.BlockSpec((d, LANES), lambda i, e: (0, 0)),
                  pl.BlockSpec((1, LANES), lambda i, e: (0, 0)),
                  pl.BlockSpec((None, d, f), lambda i, e: (e, 0, 0)),
                  pl.BlockSpec((None, d, f), lambda i, e: (e, 0, 0)),
                  pl.BlockSpec((None, f, d), lambda i, e: (e, 0, 0))],
        out_specs=pl.BlockSpec((tm, d), lambda i, e: (i, 0)),
        out_shape=jax.ShapeDtypeStruct((m, d), F32),
        scratch_shapes=[pltpu.VMEM((tm, d), BF16), pltpu.VMEM((tm, LANES), F32),
                        pltpu.VMEM((tm, d), F32)],
        compiler_params=_params("parallel", "arbitrary"),
        name="moe",
    )(x, g.reshape(1, d), mod.arr, mod.arr, mod.arr, router_w_pad, router_b_pad, w1, w3, w2)


def _final_norm_kernel(x_ref, g_ref, o_ref):
    x = x_ref[...]
    ms = jnp.mean(x * x, axis=-1, keepdims=True)
    o_ref[...] = x * lax.rsqrt(ms + NORM_EPS) * g_ref[...]


def final_norm(x, g, tm):
    m, d = x.shape
    return pl.pallas_call(
        _final_norm_kernel,
        grid=(m // tm,),
        in_specs=[pl.BlockSpec((tm, d), lambda i: (i, 0)), pl.BlockSpec((1, d), lambda i: (0, 0))],
        out_specs=pl.BlockSpec((tm, d), lambda i: (i, 0)),
        out_shape=jax.ShapeDtypeStruct((m, d), F32),
        compiler_params=_params("parallel"),
        name="final_norm",
    )(x, g.reshape(1, d))


def _rope_tables(pos):
    half = HEAD_DIM // 2
    inv = ROPE_BASE ** (-jnp.arange(half, dtype=F32) / half)
    ang = pos.astype(F32)[:, None] * inv[None, :]
    cos, sin = jnp.cos(ang), jnp.sin(ang)
    cos_row = jnp.tile(jnp.concatenate([cos, cos], axis=1), (1, N_HEADS))
    sin_row = jnp.tile(jnp.concatenate([-sin, sin], axis=1), (1, N_HEADS))
    return cos_row, sin_row


def _retention_tables(pos, chunk):
    lg = jnp.log1p(-(2.0 ** (-5.0 - jnp.arange(N_HEADS, dtype=F32))))
    idx = jnp.arange(chunk, dtype=F32)
    diff = idx[:, None] - idx[None, :]
    inner = jnp.where(diff >= 0, jnp.exp(lg[:, None, None] * jnp.maximum(diff, 0.0)), 0.0)
    q_decay = jnp.exp(lg[None, :] * (idx[:, None] + 1.0))
    k_decay = jnp.exp(lg[None, :] * (chunk - 1.0 - idx[:, None]))
    chunk_decay = jnp.exp(lg * chunk)
    lanes = lambda a: jnp.repeat(a, HEAD_DIM, axis=-1)
    cos, sin = _rope_tables(pos)
    return {"cos": cos, "sin": sin, "inner": inner, "q_decay": lanes(q_decay),
            "k_decay": lanes(k_decay), "chunk_decay": lanes(chunk_decay[None, :]),
            "gamma": chunk_decay}


def _pack_w_kernel(w_ref, o_ref):
    rows = w_ref.shape[0]
    o_ref[:, 0:N_MAIN] = w_ref[:, 0:N_MAIN].astype(BF16)
    lane = _iota((rows, W_MIX), 1)
    o_ref[:, N_MAIN:GATE_COL0] = jnp.where(lane < N_HEADS, w_ref[:, N_MAIN:GATE_COL0], 0.0).astype(BF16)
    n_gate = N_BRANCH * D_MODEL
    step = 1024
    for c in range(0, n_gate, step):
        width = min(step + LANES, n_gate + N_HEADS - c)
        win = w_ref[:, N_MAIN + c:N_MAIN + c + width]
        o_ref[:, GATE_COL0 + c:GATE_COL0 + c + step] = win[:, N_HEADS:N_HEADS + step].astype(BF16)


def pack_w_in(w_in):
    depth, d, n_in = w_in.shape
    assert n_in == N_MAIN + N_HEADS + N_BRANCH * D_MODEL
    tr = 256
    return pl.pallas_call(
        _pack_w_kernel,
        grid=(depth, d // tr),
        in_specs=[pl.BlockSpec((None, tr, n_in), lambda l, r: (l, r, 0))],
        out_specs=pl.BlockSpec((None, tr, N_PROJ), lambda l, r: (l, r, 0)),
        out_shape=jax.ShapeDtypeStruct((depth, d, N_PROJ), BF16),
        compiler_params=_params("parallel", "parallel"),
        name="pack_w_in",
    )(w_in)


def _block_diag_pool(w_pool_l):
    g, c, _ = w_pool_l.shape
    out = jnp.zeros((g * c, g * c), w_pool_l.dtype)
    for i in range(g):
        out = out.at[i * c:(i + 1) * c, i * c:(i + 1) * c].set(w_pool_l[i])
    return out.astype(BF16)


def _pad_lanes(row, n):
    return jnp.pad(row, ((0, 0), (0, n - row.shape[1])))


def _diag_blocks(s_bd):
    return jnp.stack([s_bd[:, h * HEAD_DIM:(h + 1) * HEAD_DIM, h * HEAD_DIM:(h + 1) * HEAD_DIM]
                      for h in range(N_HEADS)], axis=1)


def kernel(x_prompt, x_sample, cache_sb_kv, cache_fox_kv, cache_fox_logf, state_pool, state_ret,
           page_table, c_prompt, c_sample, w_ada, b_ada, norm_mix, w_in, w_pool, pool_scale,
           fox_bias, w_branch, w_out, norm_ffn, router_w, router_b, w1, w3, w2, norm_final):
    bp, tp, d = x_prompt.shape
    db, ts, _ = x_sample.shape
    assert ts == 1 and d == D_MODEL
    depth = w_in.shape[0]
    n_phys, _, _, page, _, _ = cache_sb_kv.shape
    n_pages = page_table.shape[1]
    past_len = n_pages * page
    mp = bp * tp

    mod_all = adaln(jnp.concatenate([c_prompt, c_sample], axis=0), w_ada, b_ada)

    sb_cache = jnp.transpose(cache_sb_kv, (0, 1, 2, 4, 5, 3)).reshape(n_phys, depth, 2, W_MIX, page)
    fox_cache = jnp.transpose(cache_fox_kv, (0, 1, 2, 4, 5, 3)).reshape(n_phys, depth, 2, W_MIX, page)
    logf_t = jnp.swapaxes(cache_fox_logf, 2, 3)
    state_t = jnp.transpose(state_ret, (1, 2, 3, 4, 0)).reshape(depth * N_HEADS, HEAD_DIM, HEAD_DIM, db)
    w_cat_all = pack_w_in(w_in)
    router_w_pad = _pad_lanes(router_w, LANES)
    router_b_pad = _pad_lanes(router_b[None, :], LANES)
    tabs_p = _retention_tables(jnp.arange(tp), RET_CHUNK if tp % RET_CHUNK == 0 else tp)
    tabs_s = _retention_tables(past_len + jnp.arange(ts), ts)

    tm_p = min(1024, tp)
    tm_mrg = min(512, tp)
    xp = x_prompt.reshape(mp, d)
    xs = x_sample.reshape(db, d)
    outs_p = {k: [] for k in ("sb", "fox", "logf", "pool", "ret")}
    outs_s = {k: [] for k in ("sb", "fox", "logf", "pool", "ret")}

    for l in range(depth):
        w_bd = _block_diag_pool(w_pool[l])
        scale_row = pool_scale[l][None, :]
        bias_lanes = _pad_lanes(fox_bias[l][None, :], LANES)
        bias_row = _pad_lanes(fox_bias[l][None, :], W_MIX)
        wb = w_branch[l].astype(BF16)
        wo = w_out[l].astype(BF16)
        w1l, w3l, w2l = w1[l].astype(BF16), w3[l].astype(BF16), w2[l].astype(BF16)

        mod_in = Mod(mod_all[l, :bp], tp, tm_p)
        mod_mrg = Mod(mod_all[l, :bp], tp, tm_mrg)
        proj = in_proj(xp, norm_mix[l], mod_in, w_cat_all, l, tm_p)
        pool_o = pool_prompt(proj, w_bd, scale_row, bp, tp)
        sb_o = sb_prompt(proj, bp, tp)
        ret_o, ret_state = ret_prompt(proj, tabs_p, bp, tp)
        logf, cum, cum_t = logf_cum_prompt(proj, bias_lanes, bp, tp)
        fox_o = fox_prompt(proj, cum, cum_t, bp, tp)
        xp = merge_out((pool_o, sb_o, ret_o, fox_o), proj, wb, wo, xp, mod_mrg, tm_mrg)
        xp = moe(xp, norm_ffn[l], mod_in, router_w_pad, router_b_pad, w1l, w3l, w2l, tm_p)
        p3 = proj.reshape(bp, tp, N_PROJ)

        def kv(cb, p3=p3):
            return jnp.stack([p3[:, :, cb * W_MIX:(cb + 1) * W_MIX],
                              p3[:, :, (cb + 1) * W_MIX:(cb + 2) * W_MIX]], axis=1)

        outs_p["sb"].append(kv(COL_SBK))
        outs_p["fox"].append(kv(COL_FK))
        outs_p["logf"].append(logf.reshape(bp, tp, LANES)[:, :, :N_HEADS])
        outs_p["pool"].append(p3[:, tp - POOL_BUF:, :W_MIX])
        outs_p["ret"].append(_diag_blocks(ret_state))

        mod_s = Mod(mod_all[l, bp:], 1, db)
        proj_s = in_proj(xs, norm_mix[l], mod_s, w_cat_all, l, db)
        buf_t = jnp.swapaxes(state_pool[:, l], 0, 1)
        pool_s = pool_sample(buf_t, proj_s, w_bd, scale_row, past_len)
        sb_s, fox_s, logf_s = decode_attn(page_table, proj_s, sb_cache, fox_cache, logf_t, bias_row, l)
        ret_s, state_new = ret_sample(proj_s, state_t, l, tabs_s)
        xs = merge_out((pool_s, sb_s, ret_s, fox_s), proj_s, wb, wo, xs, mod_s, db)
        xs = moe(xs, norm_ffn[l], mod_s, router_w_pad, router_b_pad, w1l, w3l, w2l, db)

        def kv_s(cb, proj_s=proj_s):
            return jnp.stack([proj_s[:, cb * W_MIX:(cb + 1) * W_MIX],
                              proj_s[:, (cb + 1) * W_MIX:(cb + 2) * W_MIX]], axis=1)

        outs_s["sb"].append(kv_s(COL_SBK))
        outs_s["fox"].append(kv_s(COL_FK))
        outs_s["logf"].append(logf_s[:, :N_HEADS])
        outs_s["pool"].append(jnp.concatenate([state_pool[:, l, 1:], proj_s[:, None, :W_MIX]], axis=1))
        outs_s["ret"].append(state_new)

    y_p = final_norm(xp, norm_final, tm_p).reshape(bp, tp, d)
    y_s = final_norm(xs, norm_final, db).reshape(db, ts, d)

    def heads(a, t):
        return a.reshape(a.shape[0], depth, 2, t, N_HEADS, HEAD_DIM)

    return (y_p, y_s,
            heads(jnp.stack(outs_p["sb"], axis=1), tp),
            heads(jnp.stack(outs_s["sb"], axis=1), ts),
            heads(jnp.stack(outs_p["fox"], axis=1), tp),
            heads(jnp.stack(outs_s["fox"], axis=1), ts),
            jnp.stack(outs_p["logf"], axis=1),
            jnp.stack(outs_s["logf"], axis=1).reshape(db, depth, ts, N_HEADS),
            jnp.stack(outs_p["pool"], axis=1),
            jnp.stack(outs_s["pool"], axis=1),
            jnp.stack(outs_p["ret"], axis=1),
            jnp.transpose(jnp.stack(outs_s["ret"], axis=0), (4, 0, 1, 2, 3)))
```

```python
import functools

import numpy as np
import jax
import jax.numpy as jnp
from jax import lax
from jax.experimental import pallas as pl
from jax.experimental.pallas import tpu as pltpu

F32 = jnp.float32
BF16 = jnp.bfloat16

D_MODEL = 1024
HEAD_DIM = 64
W_MIX = 256
N_HEADS = W_MIX // HEAD_DIM
N_BRANCH = 4
POOL_WINDOWS = (2, 4, 8, 16)
POOL_BUF = 15
RET_CHUNK = 128
ROPE_BASE = 10000.0
N_EXPERTS = 16
EXPERTS_PER_GROUP = 4
D_EXPERT = 512
N_MOD = 6
NORM_EPS = 1e-6
ATTN_SCALE = HEAD_DIM ** -0.5
LANES = 128
SUBLANES = 8
VMEM_LIMIT = 56 * 1024 * 1024

COL_U, COL_SBQ, COL_SBK, COL_SBV = 0, 1, 2, 3
COL_RQ, COL_RK, COL_RV, COL_RG = 4, 5, 6, 7
COL_FQ, COL_FK, COL_FV, COL_LOGIT = 8, 9, 10, 11
N_MAIN = 11 * W_MIX
GATE_COL0 = 12 * W_MIX
N_PROJ = GATE_COL0 + N_BRANCH * D_MODEL


def _params(*sem):
    return pltpu.CompilerParams(dimension_semantics=sem, vmem_limit_bytes=VMEM_LIMIT)


def _split2(x):
    hi = x.astype(BF16)
    lo = (x - hi.astype(F32)).astype(BF16)
    return hi, lo


def _split3(x):
    hi = x.astype(BF16)
    r = x - hi.astype(F32)
    mid = r.astype(BF16)
    lo = (r - mid.astype(F32)).astype(BF16)
    return hi, mid, lo


def _dot(a, b):
    return jnp.dot(a, b, preferred_element_type=F32)


def _dot_nt(a, b):
    return lax.dot_general(a, b, (((1,), (1,)), ((), ())), preferred_element_type=F32)


def _dot_tn(a, b):
    return lax.dot_general(a, b, (((0,), (0,)), ((), ())), preferred_element_type=F32)


def _dot_split(x, w, parts=2):
    ps = _split2(x) if parts == 2 else _split3(x)
    acc = _dot(ps[0], w)
    for p in ps[1:]:
        acc = acc + _dot(p, w)
    return acc


def _iota(shape, dim):
    return lax.broadcasted_iota(jnp.int32, shape, dim)


def _softplus(z):
    return jnp.maximum(z, 0.0) + jnp.log1p(jnp.exp(-jnp.abs(z)))


def _silu(x):
    return x * jax.nn.sigmoid(x)


def _adaln_kernel(c_ref, w_ref, b_ref, o_ref):
    cond = _silu(c_ref[...]).astype(BF16)
    o_ref[...] = _dot(cond, w_ref[...].astype(BF16)) + b_ref[...]


def adaln(c_all, w_ada, b_ada):
    rows, d = c_all.shape
    depth, _, n = w_ada.shape
    tn = 1024
    return pl.pallas_call(
        _adaln_kernel,
        grid=(depth, n // tn),
        in_specs=[pl.BlockSpec((rows, d), lambda l, j: (0, 0)),
                  pl.BlockSpec((None, d, tn), lambda l, j: (l, 0, j)),
                  pl.BlockSpec((None, 1, tn), lambda l, j: (l, 0, j))],
        out_specs=pl.BlockSpec((None, rows, tn), lambda l, j: (l, 0, j)),
        out_shape=jax.ShapeDtypeStruct((depth, rows, n), F32),
        compiler_params=_params("parallel", "parallel"),
        name="adaln",
    )(c_all, w_ada, b_ada.reshape(depth, 1, n))


class Mod:
    def __init__(self, arr, rows_per_vec, tm):
        self.per_row = rows_per_vec == 1
        self.arr = arr if self.per_row else arr.reshape(arr.shape[0], 1, arr.shape[-1])
        self.tiles_per_vec = 1 if self.per_row else rows_per_vec // tm
        self.tm = tm

    def spec(self, k):
        if self.per_row:
            return pl.BlockSpec((self.tm, D_MODEL), lambda i, *_: (i, k))
        t = self.tiles_per_vec
        return pl.BlockSpec((None, 1, D_MODEL), lambda i, *_: (i // t, 0, k))


def _norm_mod(x, g, sc, sh):
    ms = jnp.mean(x * x, axis=-1, keepdims=True)
    y = x * lax.rsqrt(ms + NORM_EPS) * g
    return y * (1.0 + sc) + sh


_PROJ_TN = 1024


def _inproj_kernel(x_ref, g_ref, sh_ref, sc_ref, w_ref, *refs, kv_out):
    j = pl.program_id(1)
    o_ref, h_scr = refs[-2], refs[-1]

    @pl.when(j == 0)
    def _():
        h_scr[...] = _norm_mod(x_ref[...], g_ref[...], sc_ref[...], sh_ref[...]).astype(BF16)

    res = _dot(h_scr[...], w_ref[...])
    o_ref[...] = res
    if kv_out:
        per_tile = _PROJ_TN // W_MIX
        for col_k, kv_ref in ((COL_SBK, refs[-4]), (COL_FK, refs[-3])):
            c0 = (col_k % per_tile) * W_MIX

            @pl.when(j == col_k // per_tile)
            def _(c0=c0, kv_ref=kv_ref):
                kv_ref[0] = jnp.transpose(res[:, c0:c0 + W_MIX])
                kv_ref[1] = jnp.transpose(res[:, c0 + W_MIX:c0 + 2 * W_MIX])


def in_proj(x, g, mod, w_all, layer, tm, kv_seq=None, kv_prev=None):
    m, d = x.shape
    depth, _, n = w_all.shape
    tn = _PROJ_TN
    in_specs = [pl.BlockSpec((tm, d), lambda i, j: (i, 0)),
                pl.BlockSpec((1, d), lambda i, j: (0, 0)),
                mod.spec(0), mod.spec(1),
                pl.BlockSpec((None, d, tn), lambda i, j: (layer, 0, j))]
    args = [x, g.reshape(1, d), mod.arr, mod.arr, w_all]
    out_specs = [pl.BlockSpec((tm, tn), lambda i, j: (i, j))]
    out_shape = [jax.ShapeDtypeStruct((m, n), F32)]
    aliases = {}
    if kv_seq is not None:
        b, t = kv_seq
        per_b = t // tm
        kv_spec = pl.BlockSpec((None, 2, None, W_MIX, tm),
                               lambda i, j: (layer, 0, i // per_b, 0, i % per_b))
        kv_shape = jax.ShapeDtypeStruct((depth, 2, b, W_MIX, t), F32)
        out_specs = [kv_spec, kv_spec] + out_specs
        out_shape = [kv_shape, kv_shape] + out_shape
        if kv_prev is not None:
            aliases = {len(args): 0, len(args) + 1: 1}
            in_specs += [pl.BlockSpec(memory_space=pl.ANY)] * 2
            args += list(kv_prev)
    outs = pl.pallas_call(
        functools.partial(_inproj_kernel, kv_out=kv_seq is not None),
        grid=(m // tm, n // tn),
        in_specs=in_specs,
        out_specs=out_specs,
        out_shape=out_shape,
        scratch_shapes=[pltpu.VMEM((tm, d), BF16)],
        input_output_aliases=aliases,
        compiler_params=_params("parallel", "arbitrary"),
        name="in_proj",
    )(*args)
    return outs if kv_seq is not None else outs[0]


def _pool_tail(s2, s4, s8, s16, u, pos0, w_ref, sc_ref, o_ref):
    t = u.shape[0]
    lane = _iota((1, W_MIX), 1) // (W_MIX // len(POOL_WINDOWS))
    win = jnp.where(lane == 0, s2, jnp.where(lane == 1, s4, jnp.where(lane == 2, s8, s16)))
    width = jnp.where(lane == 0, 2, jnp.where(lane == 1, 4, jnp.where(lane == 2, 8, 16)))
    count = jnp.minimum(width, pos0 + 1 + _iota((t, W_MIX), 0)).astype(F32)
    resid = win / count - u
    o_ref[...] = (_dot(resid.astype(BF16), w_ref[...]) * sc_ref[...]).astype(o_ref.dtype)


_POOL_PAD = 32


def _pool_prompt_kernel(u_ref, w_ref, sc_ref, o_ref, a_scr, b_scr, c_scr):
    t = u_ref.shape[0]
    p = _POOL_PAD
    u = u_ref[...]
    a_scr[0:p, :] = jnp.zeros((p, W_MIX), F32)
    a_scr[p:p + t, :] = u

    def stage(src, dst, k, lo):
        n = p + t - lo
        dst[lo:lo + n, :] = src[lo:lo + n, :] + src[lo - k:lo - k + n, :]

    stage(a_scr, b_scr, 1, 8)
    stage(b_scr, c_scr, 2, 16)
    stage(c_scr, a_scr, 4, 24)
    s8 = a_scr[p:p + t, :]
    s16 = s8 + a_scr[p - 8:p - 8 + t, :]
    _pool_tail(b_scr[p:p + t, :], c_scr[p:p + t, :], s8, s16, u, 0, w_ref, sc_ref, o_ref)


def pool_prompt(proj, w_bd, scale, b, t):
    return pl.pallas_call(
        _pool_prompt_kernel,
        grid=(b,),
        in_specs=[pl.BlockSpec((t, W_MIX), lambda i: (i, COL_U)),
                  pl.BlockSpec((W_MIX, W_MIX), lambda i: (0, 0)),
                  pl.BlockSpec((1, W_MIX), lambda i: (0, 0))],
        out_specs=pl.BlockSpec((t, W_MIX), lambda i: (i, 0)),
        out_shape=jax.ShapeDtypeStruct((b * t, W_MIX), BF16),
        scratch_shapes=[pltpu.VMEM((t + _POOL_PAD, W_MIX), F32)] * 3,
        compiler_params=_params("parallel"),
        name="pool_prompt",
    )(proj, w_bd, scale)


def _pool_sample_kernel(buf_ref, u_ref, w_ref, sc_ref, o_ref, *, pos0):
    u = u_ref[...]
    s2 = u + buf_ref[14]
    s4 = s2 + buf_ref[13] + buf_ref[12]
    s8 = s4 + buf_ref[11] + buf_ref[10] + buf_ref[9] + buf_ref[8]
    s16 = s8
    for r in range(7, -1, -1):
        s16 = s16 + buf_ref[r]
    _pool_tail(s2, s4, s8, s16, u, pos0, w_ref, sc_ref, o_ref)


def pool_sample(buf_t, proj, w_bd, scale, pos0):
    rows = proj.shape[0]
    return pl.pallas_call(
        functools.partial(_pool_sample_kernel, pos0=pos0),
        grid=(1,),
        in_specs=[pl.BlockSpec((POOL_BUF, rows, W_MIX), lambda i: (0, 0, 0)),
                  pl.BlockSpec((rows, W_MIX), lambda i: (0, COL_U)),
                  pl.BlockSpec((W_MIX, W_MIX), lambda i: (0, 0)),
                  pl.BlockSpec((1, W_MIX), lambda i: (0, 0))],
        out_specs=pl.BlockSpec((rows, W_MIX), lambda i: (0, 0)),
        out_shape=jax.ShapeDtypeStruct((rows, W_MIX), BF16),
        compiler_params=_params("arbitrary"),
        name="pool_sample",
    )(buf_t, proj, w_bd, scale)


_ATT_TILE = 256


def _head_of_lane():
    return _iota((1, W_MIX), 1) // HEAD_DIM


def _stack_heads(q):
    head = _head_of_lane()
    return jnp.concatenate([jnp.where(head == h, q, 0.0) for h in range(N_HEADS)], axis=0).astype(BF16)


def _unstack_heads(acc, tq):
    head = _head_of_lane()
    out = jnp.where(head == 0, acc[0:tq], 0.0)
    for h in range(1, N_HEADS):
        out = out + jnp.where(head == h, acc[h * tq:(h + 1) * tq], 0.0)
    return out


def _load_kv_once(k_ref, v_ref, k_scr, v_scr):
    @pl.when(pl.program_id(1) == 0)
    def _():
        k_scr[...] = k_ref[...].astype(BF16)
        v_scr[...] = v_ref[...].astype(BF16)


def _sb_prompt_kernel(q_ref, k_ref, v_ref, o_ref, k_scr, v_scr):
    tq = tk = _ATT_TILE
    i = pl.program_id(1)
    _load_kv_once(k_ref, v_ref, k_scr, v_scr)
    q4 = _stack_heads(q_ref[...] * ATTN_SCALE)
    rows = N_HEADS * tq
    below_diag = _iota((rows, tk), 1) < (_iota((rows, tk), 0) % tq)
    upper = (_iota((tk, tk), 0) > _iota((tk, tk), 1)).astype(BF16)

    def block(off, diagonal, carry):
        acc, run = carry
        kb = k_scr[pl.ds(off, tk), :]
        vb = v_scr[pl.ds(off, tk), :]
        z = _dot_nt(q4, kb)
        sp = jnp.maximum(z, 0.0) + jnp.log(1.0 + jnp.exp(-jnp.abs(z)))
        log_not = jnp.where(below_diag, -sp, 0.0) if diagonal else -sp
        suffix = _dot_split(log_not, upper)
        a = jnp.exp(z - sp + suffix + run)
        if diagonal:
            a = jnp.where(below_diag, a, 0.0)
        return acc + _dot(a.astype(BF16), vb), run + suffix[:, 0:1] + log_not[:, 0:1]

    carry = block(pl.multiple_of(i * tk, tk), True,
                  (jnp.zeros((rows, W_MIX), F32), jnp.zeros((rows, 1), F32)))
    acc, _ = lax.fori_loop(
        0, i, lambda s, c: block(pl.multiple_of((i - 1 - s) * tk, tk), False, c), carry)
    o_ref[...] = _unstack_heads(acc, tq).astype(o_ref.dtype)


def _attn_prompt_specs(t, col_q, col_k, col_v):
    tq = _ATT_TILE
    nq = t // tq
    return [pl.BlockSpec((tq, W_MIX), lambda b, i: (b * nq + i, col_q)),
            pl.BlockSpec((t, W_MIX), lambda b, i: (b, col_k)),
            pl.BlockSpec((t, W_MIX), lambda b, i: (b, col_v))]


def sb_prompt(proj, b, t):
    tq = _ATT_TILE
    nq = t // tq
    return pl.pallas_call(
        _sb_prompt_kernel,
        grid=(b, nq),
        in_specs=_attn_prompt_specs(t, COL_SBQ, COL_SBK, COL_SBV),
        out_specs=pl.BlockSpec((tq, W_MIX), lambda bb, i: (bb * nq + i, 0)),
        out_shape=jax.ShapeDtypeStruct((b * t, W_MIX), BF16),
        scratch_shapes=[pltpu.VMEM((t, W_MIX), BF16)] * 2,
        compiler_params=_params("parallel", "arbitrary"),
        name="sb_prompt",
    )(proj, proj, proj)


def _logf_cum_kernel(x_ref, bias_ref, logf_ref, cum_ref, cumt_ref):
    t = x_ref.shape[0]
    blk = 256
    logf = jax.nn.log_sigmoid(x_ref[...] + bias_ref[...])
    logf_ref[...] = logf
    lower = (_iota((blk, blk), 1) <= _iota((blk, blk), 0)).astype(BF16)
    carry = jnp.zeros((1, LANES), F32)
    for c in range(t // blk):
        part = logf[c * blk:(c + 1) * blk]
        hi, mid, lo = _split3(part)
        cum = _dot(lower, hi) + _dot(lower, mid) + _dot(lower, lo) + carry
        cum_ref[c * blk:(c + 1) * blk, :] = cum
        carry = cum[blk - 1:blk, :]
    cumt_ref[...] = jnp.transpose(cum_ref[...])[0:SUBLANES, :]


def logf_cum_prompt(proj, bias_pad, b, t):
    m = b * t
    return pl.pallas_call(
        _logf_cum_kernel,
        grid=(b,),
        in_specs=[pl.BlockSpec((t, LANES), lambda i: (i, COL_LOGIT * W_MIX // LANES)),
                  pl.BlockSpec((1, LANES), lambda i: (0, 0))],
        out_specs=[pl.BlockSpec((t, LANES), lambda i: (i, 0)),
                   pl.BlockSpec((t, LANES), lambda i: (i, 0)),
                   pl.BlockSpec((None, SUBLANES, t), lambda i: (i, 0, 0))],
        out_shape=[jax.ShapeDtypeStruct((m, LANES), F32),
                   jax.ShapeDtypeStruct((m, LANES), F32),
                   jax.ShapeDtypeStruct((b, SUBLANES, t), F32)],
        compiler_params=_params("parallel"),
        name="logf_cum",
    )(proj, bias_pad)


def _fox_prompt_kernel(q_ref, k_ref, v_ref, cq_ref, ck_ref, o_ref, k_scr, v_scr):
    tq = tk = _ATT_TILE
    i = pl.program_id(1)
    _load_kv_once(k_ref, v_ref, k_scr, v_scr)
    q4 = _stack_heads(q_ref[...] * ATTN_SCALE)
    rows = N_HEADS * tq
    on_or_below_diag = _iota((rows, tk), 1) <= (_iota((rows, tk), 0) % tq)
    cq = cq_ref[...]
    cq4 = jnp.concatenate([cq[:, h:h + 1] for h in range(N_HEADS)], axis=0)

    def block(off, diagonal, carry):
        acc, m, l = carry
        kb = k_scr[pl.ds(off, tk), :]
        vb = v_scr[pl.ds(off, tk), :]
        ck4 = jnp.concatenate([jnp.broadcast_to(ck_ref[h:h + 1, pl.ds(off, tk)], (tq, tk))
                               for h in range(N_HEADS)], axis=0)
        sc = _dot_nt(q4, kb) + (cq4 - ck4)
        if diagonal:
            sc = jnp.where(on_or_below_diag, sc, -jnp.inf)
        m_new = jnp.maximum(m, jnp.max(sc, axis=1, keepdims=True))
        alpha = jnp.exp(m - m_new)
        p = jnp.exp(sc - m_new)
        return (alpha * acc + _dot(p.astype(BF16), vb), m_new,
                alpha * l + jnp.sum(p, axis=1, keepdims=True))

    carry = block(pl.multiple_of(i * tk, tk), True,
                  (jnp.zeros((rows, W_MIX), F32), jnp.full((rows, 1), -jnp.inf, F32),
                   jnp.zeros((rows, 1), F32)))
    acc, _, l = lax.fori_loop(
        0, i, lambda s, c: block(pl.multiple_of((i - 1 - s) * tk, tk), False, c), carry)
    o_ref[...] = _unstack_heads(acc / l, tq).astype(o_ref.dtype)


def fox_prompt(proj, cum, cum_t, b, t):
    tq = _ATT_TILE
    nq = t // tq
    specs = _attn_prompt_specs(t, COL_FQ, COL_FK, COL_FV)
    specs += [pl.BlockSpec((tq, LANES), lambda bb, i: (bb * nq + i, 0)),
              pl.BlockSpec((None, SUBLANES, t), lambda bb, i: (bb, 0, 0))]
    return pl.pallas_call(
        _fox_prompt_kernel,
        grid=(b, nq),
        in_specs=specs,
        out_specs=pl.BlockSpec((tq, W_MIX), lambda bb, i: (bb * nq + i, 0)),
        out_shape=jax.ShapeDtypeStruct((b * t, W_MIX), BF16),
        scratch_shapes=[pltpu.VMEM((t, W_MIX), BF16)] * 2,
        compiler_params=_params("parallel", "arbitrary"),
        name="fox_prompt",
    )(proj, proj, proj, cum, cum_t)


def _rotary(x, cos, sin_signed):
    half = HEAD_DIM // 2
    first = (_iota((1, W_MIX), 1) % HEAD_DIM) < half
    swapped = jnp.where(first, pltpu.roll(x, W_MIX - half, axis=1), pltpu.roll(x, half, axis=1))
    return x * cos + swapped * sin_signed


def _head_mean_matrix():
    same = (_iota((W_MIX, W_MIX), 0) // HEAD_DIM) == (_iota((W_MIX, W_MIX), 1) // HEAD_DIM)
    return jnp.where(same, 1.0 / HEAD_DIM, 0.0).astype(BF16)


def _head_norm_gate(o, gate_logit):
    avg = _head_mean_matrix()
    mu = _dot_split(o, avg)
    dev = o - mu
    var = _dot_split(dev * dev, avg)
    return _silu(gate_logit) * (dev * lax.rsqrt(var + NORM_EPS))


def _ret_prompt_kernel(q_ref, k_ref, v_ref, g_ref, cos_ref, sin_ref, inner_ref, qd_ref, kd_ref,
                       cd_ref, o_ref, s_ref):
    c = pl.program_id(1)

    @pl.when(c == 0)
    def _():
        s_ref[...] = jnp.zeros_like(s_ref)

    cos = cos_ref[...]
    sin = sin_ref[...]
    q = _rotary(q_ref[...], cos, sin)
    k = _rotary(k_ref[...], cos, sin) * ATTN_SCALE
    qb = q.astype(BF16)
    kb = k.astype(BF16)
    vb = v_ref[...].astype(BF16)
    head = _head_of_lane()
    state = s_ref[...]
    o = _dot(qb, state.astype(BF16)) * qd_ref[...]
    for h in range(N_HEADS):
        qh = jnp.where(head == h, q, 0.0).astype(BF16)
        att = _dot_nt(qh, kb) * inner_ref[h]
        o = o + jnp.where(head == h, _dot(att.astype(BF16), vb), 0.0)
    same = (_iota((W_MIX, W_MIX), 0) // HEAD_DIM) == (_iota((W_MIX, W_MIX), 1) // HEAD_DIM)
    kv = _dot_tn((k * kd_ref[...]).astype(BF16), vb)
    s_ref[...] = state * cd_ref[...] + jnp.where(same, kv, 0.0)
    o_ref[...] = _head_norm_gate(o, g_ref[...]).astype(o_ref.dtype)


def ret_prompt(proj, tabs, b, t):
    ch = RET_CHUNK
    nc = t // ch

    def col(cb):
        return pl.BlockSpec((ch, W_MIX), lambda bb, c: (bb * nc + c, cb))

    full = lambda shape: pl.BlockSpec(shape, lambda bb, c: (0,) * len(shape))
    return pl.pallas_call(
        _ret_prompt_kernel,
        grid=(b, nc),
        in_specs=[col(COL_RQ), col(COL_RK), col(COL_RV), col(COL_RG),
                  pl.BlockSpec((ch, W_MIX), lambda bb, c: (c, 0)),
                  pl.BlockSpec((ch, W_MIX), lambda bb, c: (c, 0)),
                  full((N_HEADS, ch, ch)), full((ch, W_MIX)), full((ch, W_MIX)), full((1, W_MIX))],
        out_specs=[pl.BlockSpec((ch, W_MIX), lambda bb, c: (bb * nc + c, 0)),
                   pl.BlockSpec((None, W_MIX, W_MIX), lambda bb, c: (bb, 0, 0))],
        out_shape=[jax.ShapeDtypeStruct((b * t, W_MIX), BF16),
                   jax.ShapeDtypeStruct((b, W_MIX, W_MIX), F32)],
        compiler_params=_params("parallel", "arbitrary"),
        name="ret_prompt",
    )(proj, proj, proj, proj, tabs["cos"], tabs["sin"], tabs["inner"], tabs["q_decay"],
      tabs["k_decay"], tabs["chunk_decay"])


def _ret_sample_kernel(gam_ref, q_ref, k_ref, v_ref, g_ref, cos_ref, sin_ref, s_ref,
                       o_ref, sn_ref, qt_scr, kt_scr, vt_scr, ot_scr):
    h = pl.program_id(0)
    rows = q_ref.shape[0]

    @pl.when(h == 0)
    def _():
        cos = cos_ref[...]
        sin = sin_ref[...]
        qt_scr[...] = jnp.transpose(_rotary(q_ref[...], cos, sin))
        kt_scr[...] = jnp.transpose(_rotary(k_ref[...], cos, sin) * ATTN_SCALE)
        vt_scr[...] = jnp.transpose(v_ref[...])

    row0 = pl.multiple_of(h * HEAD_DIM, HEAD_DIM)
    gamma = gam_ref[h]
    vh = vt_scr[pl.ds(row0, HEAD_DIM), :]

    def body(d, cross):
        qd = qt_scr[pl.ds(row0 + d, 1), :]
        kd = kt_scr[pl.ds(row0 + d, 1), :]
        s_d = s_ref[d]
        sn_ref[d] = gamma * s_d + kd * vh
        return cross + qd * s_d

    cross = lax.fori_loop(0, HEAD_DIM, body, jnp.zeros((HEAD_DIM, rows), F32))
    qk = jnp.sum(qt_scr[pl.ds(row0, HEAD_DIM), :] * kt_scr[pl.ds(row0, HEAD_DIM), :],
                 axis=0, keepdims=True)
    ot_scr[pl.ds(row0, HEAD_DIM), :] = qk * vh + cross * gamma

    @pl.when(h == N_HEADS - 1)
    def _():
        o_ref[...] = _head_norm_gate(jnp.transpose(ot_scr[...]), g_ref[...])


def ret_sample(proj, state_t, layer, tabs):
    rows = proj.shape[0]

    def col(cb):
        return pl.BlockSpec((rows, W_MIX), lambda h: (0, cb))

    row_tab = pl.BlockSpec((1, W_MIX), lambda h: (0, 0))
    state_blk = (None, HEAD_DIM, HEAD_DIM, rows)
    return pl.pallas_call(
        _ret_sample_kernel,
        grid=(N_HEADS,),
        in_specs=[pl.BlockSpec(memory_space=pltpu.SMEM),
                  col(COL_RQ), col(COL_RK), col(COL_RV), col(COL_RG), row_tab, row_tab,
                  pl.BlockSpec(state_blk, lambda h: (layer * N_HEADS + h, 0, 0, 0))],
        out_specs=[pl.BlockSpec((rows, W_MIX), lambda h: (0, 0)),
                   pl.BlockSpec(state_blk, lambda h: (h, 0, 0, 0))],
        out_shape=[jax.ShapeDtypeStruct((rows, W_MIX), F32),
                   jax.ShapeDtypeStruct((N_HEADS, HEAD_DIM, HEAD_DIM, rows), F32)],
        scratch_shapes=[pltpu.VMEM((W_MIX, rows), F32)] * 4,
        compiler_params=_params("arbitrary"),
        name="ret_sample",
    )(tabs["gamma"], proj, proj, proj, proj, tabs["cos"], tabs["sin"], state_t)


def _suffix_sums(x3, upper):
    n_pages, rows, page = x3.shape
    flat = x3.reshape(n_pages * rows, page)
    within = _dot_split(flat, upper)
    total = (within[:, 0:1] + flat[:, 0:1]).reshape(n_pages, rows, 1)
    within = within.reshape(n_pages, rows, page)
    carry = jnp.zeros((rows, 1), F32)
    outs = [None] * n_pages
    for j in range(n_pages - 1, -1, -1):
        outs[j] = within[j] + carry
        carry = carry + total[j]
    return jnp.stack(outs)


def _rows_to_col(row):
    pick = _iota((SUBLANES, W_MIX), 0) == _iota((SUBLANES, W_MIX), 1)
    return jnp.sum(jnp.where(pick, row, 0.0), axis=1, keepdims=True)


def _decode_kernel(pt_ref, qs_ref, qf_ref, kf_ref, vf_ref, lg_ref, bias_ref, *refs, n_pages):
    sb_pages = refs[0:n_pages]
    fx_pages = refs[n_pages:2 * n_pages]
    lf_pages = refs[2 * n_pages:3 * n_pages]
    sb_o, fx_o, lf_o, lf_scr = refs[3 * n_pages:]
    page = sb_pages[0].shape[2]
    head_rows = _iota((SUBLANES, W_MIX), 0) == (_iota((SUBLANES, W_MIX), 1) // HEAD_DIM)
    upper = (_iota((page, page), 0) > _iota((page, page), 1)).astype(BF16)

    def per_head_rows(row):
        return jnp.where(head_rows, row, 0.0)

    def collapse(x):
        return jnp.sum(jnp.where(head_rows, x, 0.0), axis=0, keepdims=True)

    qs = per_head_rows(qs_ref[...] * ATTN_SCALE).astype(BF16)
    z = jnp.stack([_dot(qs, sb_pages[j][0].astype(BF16)) for j in range(n_pages)])
    sp = _softplus(z)
    after = _suffix_sums(-sp, upper)
    a = jnp.exp(z - sp + after).astype(BF16)
    acc = jnp.zeros((SUBLANES, W_MIX), F32)
    for j in range(n_pages):
        acc = acc + _dot_nt(a[j], sb_pages[j][1].astype(BF16))
    sb_o[...] = collapse(acc)

    logf_new = jax.nn.log_sigmoid(lg_ref[...] + bias_ref[...])
    lf_o[...] = logf_new
    qf = per_head_rows(qf_ref[...] * ATTN_SCALE).astype(BF16)
    lf_scr[...] = jnp.zeros_like(lf_scr)
    for j in range(n_pages):
        lf_scr[j, 0:N_HEADS, :] = lf_pages[j][...]
    decay = _suffix_sums(lf_scr[...], upper)
    s = jnp.stack([_dot(qf, fx_pages[j][0].astype(BF16)) for j in range(n_pages)])
    s = s + decay + _rows_to_col(logf_new)
    k_new = jnp.broadcast_to(kf_ref[...], (SUBLANES, W_MIX)).astype(BF16)
    s_self = _dot_nt(qf, k_new)[:, 0:1]
    m = jnp.max(jnp.max(s, axis=0), axis=1, keepdims=True)
    m = jnp.maximum(m, s_self)
    p = jnp.exp(s - m)
    p_self = jnp.exp(s_self - m)
    l = jnp.sum(jnp.sum(p, axis=0), axis=1, keepdims=True) + p_self
    pb = (p / l).astype(BF16)
    acc = (p_self / l).astype(BF16).astype(F32) * vf_ref[...].astype(BF16).astype(F32)
    for j in range(n_pages):
        acc = acc + _dot_nt(pb[j], fx_pages[j][1].astype(BF16))
    fx_o[...] = collapse(acc)


def decode_attn(page_table, proj, sb_cache, fox_cache, logf_t, bias_row, layer):
    rows, n_pages = page_table.shape
    page = sb_cache.shape[4]
    proj3 = proj.reshape(rows, 1, proj.shape[1])

    def col(cb):
        return pl.BlockSpec((None, 1, W_MIX), lambda b, pt: (b, 0, cb))

    def kv_page(j):
        return pl.BlockSpec((None, None, 2, W_MIX, page),
                            lambda b, pt, j=j: (pt[b * n_pages + j], layer, 0, 0, 0))

    def lf_page(j):
        return pl.BlockSpec((None, None, N_HEADS, page),
                            lambda b, pt, j=j: (pt[b * n_pages + j], layer, 0, 0))

    out_row = pl.BlockSpec((None, 1, W_MIX), lambda b, pt: (b, 0, 0))
    grid_spec = pltpu.PrefetchScalarGridSpec(
        num_scalar_prefetch=1,
        grid=(rows,),
        in_specs=[col(COL_SBQ), col(COL_FQ), col(COL_FK), col(COL_FV), col(COL_LOGIT),
                  pl.BlockSpec((1, W_MIX), lambda b, pt: (0, 0))]
                 + [kv_page(j) for j in range(n_pages)]
                 + [kv_page(j) for j in range(n_pages)]
                 + [lf_page(j) for j in range(n_pages)],
        out_specs=[out_row, out_row, out_row],
        scratch_shapes=[pltpu.VMEM((n_pages, SUBLANES, page), F32)],
    )
    sb_o, fx_o, lf_o = pl.pallas_call(
        functools.partial(_decode_kernel, n_pages=n_pages),
        grid_spec=grid_spec,
        out_shape=[jax.ShapeDtypeStruct((rows, 1, W_MIX), F32)] * 3,
        compiler_params=_params("arbitrary"),
        name="decode_attn",
    )(page_table.reshape(-1), proj3, proj3, proj3, proj3, proj3, bias_row,
      *([sb_cache] * n_pages), *([fox_cache] * n_pages), *([logf_t] * n_pages))
    return sb_o.reshape(rows, W_MIX), fx_o.reshape(rows, W_MIX), lf_o.reshape(rows, W_MIX)


def _merge_kernel(b0_ref, b1_ref, b2_ref, b3_ref, g0_ref, g1_ref, g2_ref, g3_ref, wb_ref, wo_ref,
                  x_ref, gate_ref, o_ref):
    merged = None
    for br, gl, i in ((b0_ref, g0_ref, 0), (b1_ref, g1_ref, 1), (b2_ref, g2_ref, 2), (b3_ref, g3_ref, 3)):
        y = jax.nn.sigmoid(gl[...]) * _dot(br[...].astype(BF16), wb_ref[i])
        merged = y if merged is None else merged + y
    out = _dot(merged.astype(BF16), wo_ref[...])
    o_ref[...] = x_ref[...] + gate_ref[...] * out


def merge_out(branches, proj, w_branch, w_out, layer, x, mod, tm):
    m, d = x.shape
    gate0 = GATE_COL0 // d
    br_spec = pl.BlockSpec((tm, W_MIX), lambda i: (i, 0))
    return pl.pallas_call(
        _merge_kernel,
        grid=(m // tm,),
        in_specs=[br_spec] * N_BRANCH
                 + [pl.BlockSpec((tm, d), lambda i, k=k: (i, gate0 + k)) for k in range(N_BRANCH)]
                 + [pl.BlockSpec((None, N_BRANCH, W_MIX, d), lambda i: (layer, 0, 0, 0)),
                    pl.BlockSpec((None, d, d), lambda i: (layer, 0, 0)),
                    pl.BlockSpec((tm, d), lambda i: (i, 0)),
                    mod.spec(2)],
        out_specs=pl.BlockSpec((tm, d), lambda i: (i, 0)),
        out_shape=jax.ShapeDtypeStruct((m, d), F32),
        compiler_params=_params("parallel"),
        name="merge_out",
    )(*branches, proj, proj, proj, proj, w_branch, w_out, x, mod.arr)


def _route(scores, biased):
    lane_i = _iota(scores.shape, 1)
    real = lane_i < N_EXPERTS
    pos = lane_i % EXPERTS_PER_GROUP
    lane = lane_i.astype(F32)
    group = (lane_i // EXPERTS_PER_GROUP).astype(F32)
    neg = -jnp.inf
    n = scores.shape[1]
    mates = [biased]
    for k in range(1, EXPERTS_PER_GROUP):
        fwd = pltpu.roll(biased, n - k, axis=1)
        back = pltpu.roll(biased, EXPERTS_PER_GROUP - k, axis=1)
        mates.append(jnp.where(pos + k < EXPERTS_PER_GROUP, fwd, back))
    group_score = None
    for a in range(EXPERTS_PER_GROUP):
        for b in range(a + 1, EXPERTS_PER_GROUP):
            pair = mates[a] + mates[b]
            group_score = pair if group_score is None else jnp.maximum(group_score, pair)
    group_score = jnp.where(real, group_score, neg)
    best_score = jnp.max(group_score, axis=1, keepdims=True)
    best = jnp.min(jnp.where(group_score == best_score, group, float(n)), axis=1, keepdims=True)
    cand = jnp.where(real & (group == best), biased, neg)
    top0 = jnp.max(cand, axis=1, keepdims=True)
    idx0 = jnp.min(jnp.where(cand == top0, lane, float(n)), axis=1, keepdims=True)
    cand = jnp.where(lane == idx0, neg, cand)
    top1 = jnp.max(cand, axis=1, keepdims=True)
    idx1 = jnp.min(jnp.where(cand == top1, lane, float(n)), axis=1, keepdims=True)
    chosen = (lane == idx0) | (lane == idx1)
    sel = jnp.where(chosen, scores, 0.0)
    return sel / jnp.sum(sel, axis=1, keepdims=True)


def _moe_kernel(x_ref, g_ref, sh_ref, sc_ref, gate_ref, rw_ref, rb_ref, w1_ref, w3_ref, w2_ref,
                o_ref, h_scr, comb_scr, acc_scr):
    e = pl.program_id(1)

    @pl.when(e == 0)
    def _():
        h = _norm_mod(x_ref[...], g_ref[...], sc_ref[...], sh_ref[...])
        h_scr[...] = h.astype(BF16)
        scores = jax.nn.sigmoid(_dot(h_scr[...], rw_ref[...].astype(BF16)))
        comb_scr[...] = _route(scores, scores + rb_ref[...])
        acc_scr[...] = jnp.zeros_like(acc_scr)

    h = h_scr[...]
    a = _dot(h, w1_ref[...])
    b = _dot(h, w3_ref[...])
    y = _dot((_silu(a) * b).astype(BF16), w2_ref[...])
    lane = _iota(comb_scr.shape, 1)
    w_e = jnp.sum(jnp.where(lane == e, comb_scr[...], 0.0), axis=1, keepdims=True)
    acc_scr[...] = acc_scr[...] + w_e * y

    @pl.when(e == N_EXPERTS - 1)
    def _():
        o_ref[...] = x_ref[...] + gate_ref[...] * acc_scr[...]


def moe(x, g, mod, router_w_pad, router_b_pad, w1, w3, w2, layer, tm):
    m, d = x.shape
    f = w1.shape[-1]
    expert = lambda i, e: (layer, e, 0, 0)
    return pl.pallas_call(
        _moe_kernel,
        grid=(m // tm, N_EXPERTS),
        in_specs=[pl.BlockSpec((tm, d), lambda i, e: (i, 0)),
                  pl.BlockSpec((1, d), lambda i, e: (0, 0)),
                  mod.spec(3), mod.spec(4), mod.spec(5),
                  pl.BlockSpec((d, LANES), lambda i, e: (0, 0)),
                  pl.BlockSpec((1, LANES), lambda i, e: (0, 0)),
                  pl.BlockSpec((None, None, d, f), expert),
                  pl.BlockSpec((None, None, d, f), expert),
                  pl.BlockSpec((None, None, f, d), expert)],
        out_specs=pl.BlockSpec((tm, d), lambda i, e: (i, 0)),
        out_shape=jax.ShapeDtypeStruct((m, d), F32),
        scratch_shapes=[pltpu.VMEM((tm, d), BF16), pltpu.VMEM((tm, LANES), F32),
                        pltpu.VMEM((tm, d), F32)],
        compiler_params=_params("parallel", "arbitrary"),
        name="moe",
    )(x, g.reshape(1, d), mod.arr, mod.arr, mod.arr, router_w_pad, router_b_pad, w1, w3, w2)


def _final_norm_kernel(x_ref, g_ref, o_ref):
    x = x_ref[...]
    ms = jnp.mean(x * x, axis=-1, keepdims=True)
    o_ref[...] = x * lax.rsqrt(ms + NORM_EPS) * g_ref[...]


def final_norm(x, g, tm):
    m, d = x.shape
    return pl.pallas_call(
        _final_norm_kernel,
        grid=(m // tm,),
        in_specs=[pl.BlockSpec((tm, d), lambda i: (i, 0)), pl.BlockSpec((1, d), lambda i: (0, 0))],
        out_specs=pl.BlockSpec((tm, d), lambda i: (i, 0)),
        out_shape=jax.ShapeDtypeStruct((m, d), F32),
        compiler_params=_params("parallel"),
        name="final_norm",
    )(x, g.reshape(1, d))


def _rope_tables(pos):
    half = HEAD_DIM // 2
    inv = ROPE_BASE ** (-jnp.arange(half, dtype=F32) / half)
    ang = pos.astype(F32)[:, None] * inv[None, :]
    cos, sin = jnp.cos(ang), jnp.sin(ang)
    cos_row = jnp.tile(jnp.concatenate([cos, cos], axis=1), (1, N_HEADS))
    sin_row = jnp.tile(jnp.concatenate([-sin, sin], axis=1), (1, N_HEADS))
    return cos_row, sin_row


def _retention_tables(pos, chunk):
    lg = jnp.log1p(-(2.0 ** (-5.0 - jnp.arange(N_HEADS, dtype=F32))))
    idx = jnp.arange(chunk, dtype=F32)
    diff = idx[:, None] - idx[None, :]
    inner = jnp.where(diff >= 0, jnp.exp(lg[:, None, None] * jnp.maximum(diff, 0.0)), 0.0)
    q_decay = jnp.exp(lg[None, :] * (idx[:, None] + 1.0))
    k_decay = jnp.exp(lg[None, :] * (chunk - 1.0 - idx[:, None]))
    chunk_decay = jnp.exp(lg * chunk)
    lanes = lambda a: jnp.repeat(a, HEAD_DIM, axis=-1)
    cos, sin = _rope_tables(pos)
    return {"cos": cos, "sin": sin, "inner": inner, "q_decay": lanes(q_decay),
            "k_decay": lanes(k_decay), "chunk_decay": lanes(chunk_decay[None, :]),
            "gamma": chunk_decay}


def _pack_w_kernel(w_ref, o_ref):
    rows = w_ref.shape[0]
    o_ref[:, 0:N_MAIN] = w_ref[:, 0:N_MAIN].astype(BF16)
    lane = _iota((rows, W_MIX), 1)
    o_ref[:, N_MAIN:GATE_COL0] = jnp.where(lane < N_HEADS, w_ref[:, N_MAIN:GATE_COL0], 0.0).astype(BF16)
    n_gate = N_BRANCH * D_MODEL
    step = 1024
    for c in range(0, n_gate, step):
        width = min(step + LANES, n_gate + N_HEADS - c)
        win = w_ref[:, N_MAIN + c:N_MAIN + c + width]
        o_ref[:, GATE_COL0 + c:GATE_COL0 + c + step] = win[:, N_HEADS:N_HEADS + step].astype(BF16)


def pack_w_in(w_in):
    depth, d, n_in = w_in.shape
    assert n_in == N_MAIN + N_HEADS + N_BRANCH * D_MODEL
    tr = 256
    return pl.pallas_call(
        _pack_w_kernel,
        grid=(depth, d // tr),
        in_specs=[pl.BlockSpec((None, tr, n_in), lambda l, r: (l, r, 0))],
        out_specs=pl.BlockSpec((None, tr, N_PROJ), lambda l, r: (l, r, 0)),
        out_shape=jax.ShapeDtypeStruct((depth, d, N_PROJ), BF16),
        compiler_params=_params("parallel", "parallel"),
        name="pack_w_in",
    )(w_in)


def _block_diag_pool(w_pool_l):
    g, c, _ = w_pool_l.shape
    out = jnp.zeros((g * c, g * c), w_pool_l.dtype)
    for i in range(g):
        out = out.at[i * c:(i + 1) * c, i * c:(i + 1) * c].set(w_pool_l[i])
    return out.astype(BF16)


def _pad_lanes(row, n):
    return jnp.pad(row, ((0, 0), (0, n - row.shape[1])))


def _diag_blocks(s_bd):
    return jnp.stack([s_bd[:, h * HEAD_DIM:(h + 1) * HEAD_DIM, h * HEAD_DIM:(h + 1) * HEAD_DIM]
                      for h in range(N_HEADS)], axis=1)


def kernel(x_prompt, x_sample, cache_sb_kv, cache_fox_kv, cache_fox_logf, state_pool, state_ret,
           page_table, c_prompt, c_sample, w_ada, b_ada, norm_mix, w_in, w_pool, pool_scale,
           fox_bias, w_branch, w_out, norm_ffn, router_w, router_b, w1, w3, w2, norm_final):
    bp, tp, d = x_prompt.shape
    db, ts, _ = x_sample.shape
    assert ts == 1 and d == D_MODEL
    depth = w_in.shape[0]
    n_phys, _, _, page, _, _ = cache_sb_kv.shape
    n_pages = page_table.shape[1]
    past_len = n_pages * page
    mp = bp * tp

    mod_all = adaln(jnp.concatenate([c_prompt, c_sample], axis=0), w_ada, b_ada)

    sb_cache = jnp.transpose(cache_sb_kv, (0, 1, 2, 4, 5, 3)).reshape(n_phys, depth, 2, W_MIX, page)
    fox_cache = jnp.transpose(cache_fox_kv, (0, 1, 2, 4, 5, 3)).reshape(n_phys, depth, 2, W_MIX, page)
    logf_t = jnp.swapaxes(cache_fox_logf, 2, 3)
    state_t = jnp.transpose(state_ret, (1, 2, 3, 4, 0)).reshape(depth * N_HEADS, HEAD_DIM, HEAD_DIM, db)
    w_cat_all = pack_w_in(w_in)
    router_w_pad = _pad_lanes(router_w, LANES)
    router_b_pad = _pad_lanes(router_b[None, :], LANES)
    tabs_p = _retention_tables(jnp.arange(tp), RET_CHUNK if tp % RET_CHUNK == 0 else tp)
    tabs_s = _retention_tables(past_len + jnp.arange(ts), ts)
    wb, wo = w_branch.astype(BF16), w_out.astype(BF16)
    w1b, w3b, w2b = w1.astype(BF16), w3.astype(BF16), w2.astype(BF16)

    tm_p = min(1024, tp)
    tm_mrg = min(512, tp)
    xp = x_prompt.reshape(mp, d)
    xs = x_sample.reshape(db, d)
    outs_p = {k: [] for k in ("logf", "pool", "ret")}
    outs_s = {k: [] for k in ("sb", "fox", "logf", "pool", "ret")}
    kv_prev = None

    for l in range(depth):
        w_bd = _block_diag_pool(w_pool[l])
        scale_row = pool_scale[l][None, :]
        bias_lanes = _pad_lanes(fox_bias[l][None, :], LANES)
        bias_row = _pad_lanes(fox_bias[l][None, :], W_MIX)

        mod_in = Mod(mod_all[l, :bp], tp, tm_p)
        mod_mrg = Mod(mod_all[l, :bp], tp, tm_mrg)
        kv_sb, kv_fox, proj = in_proj(xp, norm_mix[l], mod_in, w_cat_all, l, tm_p,
                                      kv_seq=(bp, tp), kv_prev=kv_prev)
        kv_prev = (kv_sb, kv_fox)
        pool_o = pool_prompt(proj, w_bd, scale_row, bp, tp)
        sb_o = sb_prompt(proj, bp, tp)
        ret_o, ret_state = ret_prompt(proj, tabs_p, bp, tp)
        logf, cum, cum_t = logf_cum_prompt(proj, bias_lanes, bp, tp)
        fox_o = fox_prompt(proj, cum, cum_t, bp, tp)
        xp = merge_out((pool_o, sb_o, ret_o, fox_o), proj, wb, wo, l, xp, mod_mrg, tm_mrg)
        xp = moe(xp, norm_ffn[l], mod_in, router_w_pad, router_b_pad, w1b, w3b, w2b, l, tm_p)
        p3 = proj.reshape(bp, tp, N_PROJ)
        outs_p["logf"].append(logf.reshape(bp, tp, LANES)[:, :, :N_HEADS])
        outs_p["pool"].append(p3[:, tp - POOL_BUF:, :W_MIX])
        outs_p["ret"].append(_diag_blocks(ret_state))

        mod_s = Mod(mod_all[l, bp:], 1, db)
        proj_s = in_proj(xs, norm_mix[l], mod_s, w_cat_all, l, db)
        buf_t = jnp.swapaxes(state_pool[:, l], 0, 1)
        pool_s = pool_sample(buf_t, proj_s, w_bd, scale_row, past_len)
        sb_s, fox_s, logf_s = decode_attn(page_table, proj_s, sb_cache, fox_cache, logf_t, bias_row, l)
        ret_s, state_new = ret_sample(proj_s, state_t, l, tabs_s)
        xs = merge_out((pool_s, sb_s, ret_s, fox_s), proj_s, wb, wo, l, xs, mod_s, db)
        xs = moe(xs, norm_ffn[l], mod_s, router_w_pad, router_b_pad, w1b, w3b, w2b, l, db)

        def kv_s(cb, proj_s=proj_s):
            return jnp.stack([proj_s[:, cb * W_MIX:(cb + 1) * W_MIX],
                              proj_s[:, (cb + 1) * W_MIX:(cb + 2) * W_MIX]], axis=1)

        outs_s["sb"].append(kv_s(COL_SBK))
        outs_s["fox"].append(kv_s(COL_FK))
        outs_s["logf"].append(logf_s[:, :N_HEADS])
        outs_s["pool"].append(jnp.concatenate([state_pool[:, l, 1:], proj_s[:, None, :W_MIX]], axis=1))
        outs_s["ret"].append(state_new)

    y_p = final_norm(xp, norm_final, tm_p).reshape(bp, tp, d)
    y_s = final_norm(xs, norm_final, db).reshape(db, ts, d)

    def heads(a, t):
        return a.reshape(a.shape[0], depth, 2, t, N_HEADS, HEAD_DIM)

    def from_transposed(kv):
        return heads(jnp.transpose(kv, (2, 0, 1, 4, 3)), tp)

    return (y_p, y_s,
            from_transposed(kv_prev[0]),
            heads(jnp.stack(outs_s["sb"], axis=1), ts),
            from_transposed(kv_prev[1]),
            heads(jnp.stack(outs_s["fox"], axis=1), ts),
            jnp.stack(outs_p["logf"], axis=1),
            jnp.stack(outs_s["logf"], axis=1).reshape(db, depth, ts, N_HEADS),
            jnp.stack(outs_p["pool"], axis=1),
            jnp.stack(outs_s["pool"], axis=1),
            jnp.stack(outs_p["ret"], axis=1),
            jnp.transpose(jnp.stack(outs_s["ret"], axis=0), (4, 0, 1, 2, 3)))
```

```python
import functools

import numpy as np
import jax
import jax.numpy as jnp
from jax import lax
from jax.experimental import pallas as pl
from jax.experimental.pallas import tpu as pltpu

F32 = jnp.float32
BF16 = jnp.bfloat16

D_MODEL = 1024
HEAD_DIM = 64
W_MIX = 256
N_HEADS = W_MIX // HEAD_DIM
N_BRANCH = 4
POOL_WINDOWS = (2, 4, 8, 16)
POOL_BUF = 15
RET_CHUNK = 128
ROPE_BASE = 10000.0
N_EXPERTS = 16
EXPERTS_PER_GROUP = 4
D_EXPERT = 512
N_MOD = 6
NORM_EPS = 1e-6
ATTN_SCALE = HEAD_DIM ** -0.5
LANES = 128
SUBLANES = 8
VMEM_LIMIT = 56 * 1024 * 1024

COL_U, COL_SBQ, COL_SBK, COL_SBV = 0, 1, 2, 3
COL_RQ, COL_RK, COL_RV, COL_RG = 4, 5, 6, 7
COL_FQ, COL_FK, COL_FV, COL_LOGIT = 8, 9, 10, 11
N_MAIN = 11 * W_MIX
GATE_COL0 = 12 * W_MIX
N_GATE = N_BRANCH * D_MODEL


def _params(*sem):
    return pltpu.CompilerParams(dimension_semantics=sem, vmem_limit_bytes=VMEM_LIMIT)


def _split2(x):
    hi = x.astype(BF16)
    lo = (x - hi.astype(F32)).astype(BF16)
    return hi, lo


def _split3(x):
    hi = x.astype(BF16)
    r = x - hi.astype(F32)
    mid = r.astype(BF16)
    lo = (r - mid.astype(F32)).astype(BF16)
    return hi, mid, lo


def _dot(a, b):
    return jnp.dot(a, b, preferred_element_type=F32)


def _dot_nt(a, b):
    return lax.dot_general(a, b, (((1,), (1,)), ((), ())), preferred_element_type=F32)


def _dot_tn(a, b):
    return lax.dot_general(a, b, (((0,), (0,)), ((), ())), preferred_element_type=F32)


def _dot_split(x, w, parts=2):
    ps = _split2(x) if parts == 2 else _split3(x)
    acc = _dot(ps[0], w)
    for p in ps[1:]:
        acc = acc + _dot(p, w)
    return acc


def _dot3(x, w_hi, w_lo):
    x_hi, x_lo = _split2(x)
    return _dot(x_hi, w_hi) + (_dot(x_lo, w_hi) + _dot(x_hi, w_lo))


def _iota(shape, dim):
    return lax.broadcasted_iota(jnp.int32, shape, dim)


def _softplus(z):
    return jnp.maximum(z, 0.0) + jnp.log1p(jnp.exp(-jnp.abs(z)))


def _silu(x):
    return x * jax.nn.sigmoid(x)


def _adaln_kernel(c_ref, w_ref, b_ref, o_ref):
    w_hi, w_lo = _split2(w_ref[...])
    o_ref[...] = _dot3(_silu(c_ref[...]), w_hi, w_lo) + b_ref[...]


def adaln(c_all, w_ada, b_ada):
    rows, d = c_all.shape
    depth, _, n = w_ada.shape
    tn = 1024
    return pl.pallas_call(
        _adaln_kernel,
        grid=(depth, n // tn),
        in_specs=[pl.BlockSpec((rows, d), lambda l, j: (0, 0)),
                  pl.BlockSpec((None, d, tn), lambda l, j: (l, 0, j)),
                  pl.BlockSpec((None, 1, tn), lambda l, j: (l, 0, j))],
        out_specs=pl.BlockSpec((None, rows, tn), lambda l, j: (l, 0, j)),
        out_shape=jax.ShapeDtypeStruct((depth, rows, n), F32),
        compiler_params=_params("parallel", "parallel"),
        name="adaln",
    )(c_all, w_ada, b_ada.reshape(depth, 1, n))


class Mod:
    def __init__(self, arr, rows_per_vec, tm):
        self.per_row = rows_per_vec == 1
        self.arr = arr if self.per_row else arr.reshape(arr.shape[0], 1, arr.shape[-1])
        self.tiles_per_vec = 1 if self.per_row else rows_per_vec // tm
        self.tm = tm

    def spec(self, k):
        if self.per_row:
            return pl.BlockSpec((self.tm, D_MODEL), lambda i, *_: (i, k))
        t = self.tiles_per_vec
        return pl.BlockSpec((None, 1, D_MODEL), lambda i, *_: (i // t, 0, k))


def _norm_mod(x, g, sc, sh):
    ms = jnp.mean(x * x, axis=-1, keepdims=True)
    y = x * lax.rsqrt(ms + NORM_EPS) * g
    return y * (1.0 + sc) + sh


_PROJ_TN = 1024


def _inproj_kernel(x_ref, g_ref, sh_ref, sc_ref, w_ref, *refs, kv_out, precise):
    j = pl.program_id(1)
    o_ref, h_ref = refs[-2], refs[-1]

    @pl.when(j == 0)
    def _():
        h_ref[...] = _norm_mod(x_ref[...], g_ref[...], sc_ref[...], sh_ref[...]).astype(h_ref.dtype)

    if precise:
        res = _dot3(h_ref[...], w_ref[...], refs[0][...])
    else:
        res = _dot(h_ref[...], w_ref[...])
    o_ref[...] = res
    if kv_out:
        per_tile = _PROJ_TN // W_MIX
        for col_k, kv_ref in ((COL_SBK, refs[-4]), (COL_FK, refs[-3])):
            c0 = (col_k % per_tile) * W_MIX

            @pl.when(j == col_k // per_tile)
            def _(c0=c0, kv_ref=kv_ref):
                kv_ref[0] = jnp.transpose(res[:, c0:c0 + W_MIX])
                kv_ref[1] = jnp.transpose(res[:, c0 + W_MIX:c0 + 2 * W_MIX])


def in_proj(x, g, mod, w_all, layer, tm, kv_seq=None, kv_prev=None, w_lo=None):
    m, d = x.shape
    depth, _, n = w_all.shape
    tn = _PROJ_TN
    precise = w_lo is not None
    w_spec = pl.BlockSpec((None, d, tn), lambda i, j: (layer, 0, j))
    in_specs = [pl.BlockSpec((tm, d), lambda i, j: (i, 0)),
                pl.BlockSpec((1, d), lambda i, j: (0, 0)),
                mod.spec(0), mod.spec(1), w_spec]
    args = [x, g.reshape(1, d), mod.arr, mod.arr, w_all]
    if precise:
        in_specs.append(w_spec)
        args.append(w_lo)
    out_specs = [pl.BlockSpec((tm, tn), lambda i, j: (i, j)),
                 pl.BlockSpec((tm, d), lambda i, j: (i, 0))]
    out_shape = [jax.ShapeDtypeStruct((m, n), F32),
                 jax.ShapeDtypeStruct((m, d), F32 if precise else BF16)]
    aliases = {}
    if kv_seq is not None:
        b, t = kv_seq
        per_b = t // tm
        kv_spec = pl.BlockSpec((None, 2, None, W_MIX, tm),
                               lambda i, j: (layer, 0, i // per_b, 0, i % per_b))
        kv_shape = jax.ShapeDtypeStruct((depth, 2, b, W_MIX, t), F32)
        out_specs = [kv_spec, kv_spec] + out_specs
        out_shape = [kv_shape, kv_shape] + out_shape
        if kv_prev is not None:
            aliases = {len(args): 0, len(args) + 1: 1}
            in_specs += [pl.BlockSpec(memory_space=pl.ANY)] * 2
            args += list(kv_prev)
    outs = pl.pallas_call(
        functools.partial(_inproj_kernel, kv_out=kv_seq is not None, precise=precise),
        grid=(m // tm, n // tn),
        in_specs=in_specs,
        out_specs=out_specs,
        out_shape=out_shape,
        input_output_aliases=aliases,
        compiler_params=_params("parallel", "arbitrary"),
        name="in_proj",
    )(*args)
    return outs


def _pool_tail(s2, s4, s8, s16, u, pos0, w_ref, sc_ref, o_ref):
    t = u.shape[0]
    lane = _iota((1, W_MIX), 1) // (W_MIX // len(POOL_WINDOWS))
    win = jnp.where(lane == 0, s2, jnp.where(lane == 1, s4, jnp.where(lane == 2, s8, s16)))
    width = jnp.where(lane == 0, 2, jnp.where(lane == 1, 4, jnp.where(lane == 2, 8, 16)))
    count = jnp.minimum(width, pos0 + 1 + _iota((t, W_MIX), 0)).astype(F32)
    resid = win / count - u
    if w_ref.dtype == F32:
        w_hi, w_lo = _split2(w_ref[...])
        mixed = _dot3(resid, w_hi, w_lo)
    else:
        mixed = _dot(resid.astype(BF16), w_ref[...])
    o_ref[...] = (mixed * sc_ref[...]).astype(o_ref.dtype)


_POOL_PAD = 32


def _pool_prompt_kernel(u_ref, w_ref, sc_ref, o_ref, a_scr, b_scr, c_scr):
    t = u_ref.shape[0]
    p = _POOL_PAD
    u = u_ref[...]
    a_scr[0:p, :] = jnp.zeros((p, W_MIX), F32)
    a_scr[p:p + t, :] = u

    def stage(src, dst, k, lo):
        n = p + t - lo
        dst[lo:lo + n, :] = src[lo:lo + n, :] + src[lo - k:lo - k + n, :]

    stage(a_scr, b_scr, 1, 8)
    stage(b_scr, c_scr, 2, 16)
    stage(c_scr, a_scr, 4, 24)
    s8 = a_scr[p:p + t, :]
    s16 = s8 + a_scr[p - 8:p - 8 + t, :]
    _pool_tail(b_scr[p:p + t, :], c_scr[p:p + t, :], s8, s16, u, 0, w_ref, sc_ref, o_ref)


def pool_prompt(proj, w_bd, scale, b, t):
    return pl.pallas_call(
        _pool_prompt_kernel,
        grid=(b,),
        in_specs=[pl.BlockSpec((t, W_MIX), lambda i: (i, COL_U)),
                  pl.BlockSpec((W_MIX, W_MIX), lambda i: (0, 0)),
                  pl.BlockSpec((1, W_MIX), lambda i: (0, 0))],
        out_specs=pl.BlockSpec((t, W_MIX), lambda i: (i, 0)),
        out_shape=jax.ShapeDtypeStruct((b * t, W_MIX), BF16),
        scratch_shapes=[pltpu.VMEM((t + _POOL_PAD, W_MIX), F32)] * 3,
        compiler_params=_params("parallel"),
        name="pool_prompt",
    )(proj, w_bd, scale)


def _pool_sample_kernel(buf_ref, u_ref, w_ref, sc_ref, o_ref, *, pos0):
    u = u_ref[...]
    s2 = u + buf_ref[14]
    s4 = s2 + buf_ref[13] + buf_ref[12]
    s8 = s4 + buf_ref[11] + buf_ref[10] + buf_ref[9] + buf_ref[8]
    s16 = s8
    for r in range(7, -1, -1):
        s16 = s16 + buf_ref[r]
    _pool_tail(s2, s4, s8, s16, u, pos0, w_ref, sc_ref, o_ref)


def pool_sample(buf_t, proj, w_bd, scale, pos0):
    rows = proj.shape[0]
    return pl.pallas_call(
        functools.partial(_pool_sample_kernel, pos0=pos0),
        grid=(1,),
        in_specs=[pl.BlockSpec((POOL_BUF, rows, W_MIX), lambda i: (0, 0, 0)),
                  pl.BlockSpec((rows, W_MIX), lambda i: (0, COL_U)),
                  pl.BlockSpec((W_MIX, W_MIX), lambda i: (0, 0)),
                  pl.BlockSpec((1, W_MIX), lambda i: (0, 0))],
        out_specs=pl.BlockSpec((rows, W_MIX), lambda i: (0, 0)),
        out_shape=jax.ShapeDtypeStruct((rows, W_MIX), F32),
        compiler_params=_params("arbitrary"),
        name="pool_sample",
    )(buf_t, proj, w_bd, scale)


_ATT_TILE = 256


def _head_of_lane():
    return _iota((1, W_MIX), 1) // HEAD_DIM


def _stack_heads(q):
    head = _head_of_lane()
    return jnp.concatenate([jnp.where(head == h, q, 0.0) for h in range(N_HEADS)], axis=0).astype(BF16)


def _unstack_heads(acc, tq):
    head = _head_of_lane()
    out = jnp.where(head == 0, acc[0:tq], 0.0)
    for h in range(1, N_HEADS):
        out = out + jnp.where(head == h, acc[h * tq:(h + 1) * tq], 0.0)
    return out


def _load_kv_once(k_ref, v_ref, k_scr, v_scr):
    @pl.when(pl.program_id(1) == 0)
    def _():
        k_scr[...] = k_ref[...].astype(BF16)
        v_scr[...] = v_ref[...].astype(BF16)


def _sb_prompt_kernel(q_ref, k_ref, v_ref, o_ref, k_scr, v_scr):
    tq = tk = _ATT_TILE
    i = pl.program_id(1)
    _load_kv_once(k_ref, v_ref, k_scr, v_scr)
    q4 = _stack_heads(q_ref[...] * ATTN_SCALE)
    rows = N_HEADS * tq
    below_diag = _iota((rows, tk), 1) < (_iota((rows, tk), 0) % tq)
    upper = (_iota((tk, tk), 0) > _iota((tk, tk), 1)).astype(BF16)

    def block(off, diagonal, carry):
        acc, run = carry
        kb = k_scr[pl.ds(off, tk), :]
        vb = v_scr[pl.ds(off, tk), :]
        z = _dot_nt(q4, kb)
        sp = jnp.maximum(z, 0.0) + jnp.log(1.0 + jnp.exp(-jnp.abs(z)))
        log_not = jnp.where(below_diag, -sp, 0.0) if diagonal else -sp
        suffix = _dot_split(log_not, upper)
        a = jnp.exp(z - sp + suffix + run)
        if diagonal:
            a = jnp.where(below_diag, a, 0.0)
        return acc + _dot(a.astype(BF16), vb), run + suffix[:, 0:1] + log_not[:, 0:1]

    carry = block(pl.multiple_of(i * tk, tk), True,
                  (jnp.zeros((rows, W_MIX), F32), jnp.zeros((rows, 1), F32)))
    acc, _ = lax.fori_loop(
        0, i, lambda s, c: block(pl.multiple_of((i - 1 - s) * tk, tk), False, c), carry)
    o_ref[...] = _unstack_heads(acc, tq).astype(o_ref.dtype)


def _attn_prompt_specs(t, col_q, col_k, col_v):
    tq = _ATT_TILE
    nq = t // tq
    return [pl.BlockSpec((tq, W_MIX), lambda b, i: (b * nq + i, col_q)),
            pl.BlockSpec((t, W_MIX), lambda b, i: (b, col_k)),
            pl.BlockSpec((t, W_MIX), lambda b, i: (b, col_v))]


def sb_prompt(proj, b, t):
    tq = _ATT_TILE
    nq = t // tq
    return pl.pallas_call(
        _sb_prompt_kernel,
        grid=(b, nq),
        in_specs=_attn_prompt_specs(t, COL_SBQ, COL_SBK, COL_SBV),
        out_specs=pl.BlockSpec((tq, W_MIX), lambda bb, i: (bb * nq + i, 0)),
        out_shape=jax.ShapeDtypeStruct((b * t, W_MIX), BF16),
        scratch_shapes=[pltpu.VMEM((t, W_MIX), BF16)] * 2,
        compiler_params=_params("parallel", "arbitrary"),
        name="sb_prompt",
    )(proj, proj, proj)


def _logf_cum_kernel(x_ref, bias_ref, logf_ref, cum_ref, cumt_ref):
    t = x_ref.shape[0]
    blk = 256
    logf = jax.nn.log_sigmoid(x_ref[...] + bias_ref[...])
    logf_ref[...] = logf
    lower = (_iota((blk, blk), 1) <= _iota((blk, blk), 0)).astype(BF16)
    carry = jnp.zeros((1, LANES), F32)
    for c in range(t // blk):
        part = logf[c * blk:(c + 1) * blk]
        hi, mid, lo = _split3(part)
        cum = _dot(lower, hi) + _dot(lower, mid) + _dot(lower, lo) + carry
        cum_ref[c * blk:(c + 1) * blk, :] = cum
        carry = cum[blk - 1:blk, :]
    cumt_ref[...] = jnp.transpose(cum_ref[...])[0:SUBLANES, :]


def logf_cum_prompt(proj, bias_pad, b, t):
    m = b * t
    return pl.pallas_call(
        _logf_cum_kernel,
        grid=(b,),
        in_specs=[pl.BlockSpec((t, LANES), lambda i: (i, COL_LOGIT * W_MIX // LANES)),
                  pl.BlockSpec((1, LANES), lambda i: (0, 0))],
        out_specs=[pl.BlockSpec((t, LANES), lambda i: (i, 0)),
                   pl.BlockSpec((t, LANES), lambda i: (i, 0)),
                   pl.BlockSpec((None, SUBLANES, t), lambda i: (i, 0, 0))],
        out_shape=[jax.ShapeDtypeStruct((m, LANES), F32),
                   jax.ShapeDtypeStruct((m, LANES), F32),
                   jax.ShapeDtypeStruct((b, SUBLANES, t), F32)],
        compiler_params=_params("parallel"),
        name="logf_cum",
    )(proj, bias_pad)


def _fox_prompt_kernel(q_ref, k_ref, v_ref, cq_ref, ck_ref, o_ref, k_scr, v_scr):
    tq = tk = _ATT_TILE
    i = pl.program_id(1)
    _load_kv_once(k_ref, v_ref, k_scr, v_scr)
    q4 = _stack_heads(q_ref[...] * ATTN_SCALE)
    rows = N_HEADS * tq
    on_or_below_diag = _iota((rows, tk), 1) <= (_iota((rows, tk), 0) % tq)
    cq = cq_ref[...]

    def block(off, diagonal, carry):
        acc, m, l = carry
        kb = k_scr[pl.ds(off, tk), :]
        vb = v_scr[pl.ds(off, tk), :]
        z = _dot_nt(q4, kb)
        sc = jnp.concatenate(
            [z[h * tq:(h + 1) * tq] + (cq[:, h:h + 1] - ck_ref[h:h + 1, pl.ds(off, tk)])
             for h in range(N_HEADS)], axis=0)
        if diagonal:
            sc = jnp.where(on_or_below_diag, sc, -jnp.inf)
        m_new = jnp.maximum(m, jnp.max(sc, axis=1, keepdims=True))
        alpha = jnp.exp(m - m_new)
        p = jnp.exp(sc - m_new)
        return (alpha * acc + _dot(p.astype(BF16), vb), m_new,
                alpha * l + jnp.sum(p, axis=1, keepdims=True))

    carry = block(pl.multiple_of(i * tk, tk), True,
                  (jnp.zeros((rows, W_MIX), F32), jnp.full((rows, 1), -jnp.inf, F32),
                   jnp.zeros((rows, 1), F32)))
    acc, _, l = lax.fori_loop(
        0, i, lambda s, c: block(pl.multiple_of((i - 1 - s) * tk, tk), False, c), carry)
    o_ref[...] = _unstack_heads(acc / l, tq).astype(o_ref.dtype)


def fox_prompt(proj, cum, cum_t, b, t):
    tq = _ATT_TILE
    nq = t // tq
    specs = _attn_prompt_specs(t, COL_FQ, COL_FK, COL_FV)
    specs += [pl.BlockSpec((tq, LANES), lambda bb, i: (bb * nq + i, 0)),
              pl.BlockSpec((None, SUBLANES, t), lambda bb, i: (bb, 0, 0))]
    return pl.pallas_call(
        _fox_prompt_kernel,
        grid=(b, nq),
        in_specs=specs,
        out_specs=pl.BlockSpec((tq, W_MIX), lambda bb, i: (bb * nq + i, 0)),
        out_shape=jax.ShapeDtypeStruct((b * t, W_MIX), BF16),
        scratch_shapes=[pltpu.VMEM((t, W_MIX), BF16)] * 2,
        compiler_params=_params("parallel", "arbitrary"),
        name="fox_prompt",
    )(proj, proj, proj, cum, cum_t)


def _rotary(x, cos, sin_signed):
    half = HEAD_DIM // 2
    first = (_iota((1, W_MIX), 1) % HEAD_DIM) < half
    swapped = jnp.where(first, pltpu.roll(x, W_MIX - half, axis=1), pltpu.roll(x, half, axis=1))
    return x * cos + swapped * sin_signed


def _head_mean_matrix():
    same = (_iota((W_MIX, W_MIX), 0) // HEAD_DIM) == (_iota((W_MIX, W_MIX), 1) // HEAD_DIM)
    return jnp.where(same, 1.0 / HEAD_DIM, 0.0).astype(BF16)


def _head_norm_gate(o, gate_logit):
    avg = _head_mean_matrix()
    mu = _dot_split(o, avg)
    dev = o - mu
    var = _dot_split(dev * dev, avg)
    return _silu(gate_logit) * (dev * lax.rsqrt(var + NORM_EPS))


def _ret_prompt_kernel(q_ref, k_ref, v_ref, g_ref, cos_ref, sin_ref, inner_ref, qd_ref, kd_ref,
                       cd_ref, o_ref, s_ref):
    c = pl.program_id(1)

    @pl.when(c == 0)
    def _():
        s_ref[...] = jnp.zeros_like(s_ref)

    cos = cos_ref[...]
    sin = sin_ref[...]
    ch = cos.shape[0]
    same = (_iota((W_MIX, W_MIX), 0) // HEAD_DIM) == (_iota((W_MIX, W_MIX), 1) // HEAD_DIM)
    inner = inner_ref[...].reshape(N_HEADS * ch, ch)
    for s in range(q_ref.shape[0]):
        q = _rotary(q_ref[s], cos, sin)
        k = _rotary(k_ref[s], cos, sin) * ATTN_SCALE
        kb = k.astype(BF16)
        vb = v_ref[s].astype(BF16)
        state = s_ref[s]
        att = _dot_nt(_stack_heads(q), kb) * inner
        o = _dot(q.astype(BF16), state.astype(BF16)) * qd_ref[...]
        o = o + _unstack_heads(_dot(att.astype(BF16), vb), ch)
        kv = _dot_tn((k * kd_ref[...]).astype(BF16), vb)
        s_ref[s] = state * cd_ref[...] + jnp.where(same, kv, 0.0)
        o_ref[s] = _head_norm_gate(o, g_ref[s]).astype(o_ref.dtype)


def ret_prompt(proj, tabs, b, t):
    ch = RET_CHUNK
    nc = t // ch
    nb = max(n for n in (4, 2, 1) if b % n == 0)
    proj3 = proj.reshape(b, t, proj.shape[1])

    def col(cb):
        return pl.BlockSpec((nb, ch, W_MIX), lambda bb, c: (bb, c, cb))

    full = lambda shape: pl.BlockSpec(shape, lambda bb, c: (0,) * len(shape))
    out, state = pl.pallas_call(
        _ret_prompt_kernel,
        grid=(b // nb, nc),
        in_specs=[col(COL_RQ), col(COL_RK), col(COL_RV), col(COL_RG),
                  pl.BlockSpec((ch, W_MIX), lambda bb, c: (c, 0)),
                  pl.BlockSpec((ch, W_MIX), lambda bb, c: (c, 0)),
                  full((N_HEADS, ch, ch)), full((ch, W_MIX)), full((ch, W_MIX)), full((1, W_MIX))],
        out_specs=[pl.BlockSpec((nb, ch, W_MIX), lambda bb, c: (bb, c, 0)),
                   pl.BlockSpec((nb, W_MIX, W_MIX), lambda bb, c: (bb, 0, 0))],
        out_shape=[jax.ShapeDtypeStruct((b, t, W_MIX), BF16),
                   jax.ShapeDtypeStruct((b, W_MIX, W_MIX), F32)],
        compiler_params=_params("parallel", "arbitrary"),
        name="ret_prompt",
    )(proj3, proj3, proj3, proj3, tabs["cos"], tabs["sin"], tabs["inner"], tabs["q_decay"],
      tabs["k_decay"], tabs["chunk_decay"])
    return out.reshape(b * t, W_MIX), state


def _ret_sample_kernel(gam_ref, q_ref, k_ref, v_ref, g_ref, cos_ref, sin_ref, s_ref,
                       o_ref, sn_ref, qt_scr, kt_scr, vt_scr, ot_scr):
    h = pl.program_id(0)
    rows = q_ref.shape[0]

    @pl.when(h == 0)
    def _():
        cos = cos_ref[...]
        sin = sin_ref[...]
        qt_scr[...] = jnp.transpose(_rotary(q_ref[...], cos, sin))
        kt_scr[...] = jnp.transpose(_rotary(k_ref[...], cos, sin) * ATTN_SCALE)
        vt_scr[...] = jnp.transpose(v_ref[...])

    row0 = pl.multiple_of(h * HEAD_DIM, HEAD_DIM)
    gamma = gam_ref[h]
    vh = vt_scr[pl.ds(row0, HEAD_DIM), :]

    def body(d, cross):
        qd = qt_scr[pl.ds(row0 + d, 1), :]
        kd = kt_scr[pl.ds(row0 + d, 1), :]
        s_d = s_ref[d]
        sn_ref[d] = gamma * s_d + kd * vh
        return cross + qd * s_d

    cross = lax.fori_loop(0, HEAD_DIM, body, jnp.zeros((HEAD_DIM, rows), F32))
    qk = jnp.sum(qt_scr[pl.ds(row0, HEAD_DIM), :] * kt_scr[pl.ds(row0, HEAD_DIM), :],
                 axis=0, keepdims=True)
    ot_scr[pl.ds(row0, HEAD_DIM), :] = qk * vh + cross * gamma

    @pl.when(h == N_HEADS - 1)
    def _():
        o_ref[...] = _head_norm_gate(jnp.transpose(ot_scr[...]), g_ref[...])


def ret_sample(proj, state_t, layer, tabs):
    rows = proj.shape[0]

    def col(cb):
        return pl.BlockSpec((rows, W_MIX), lambda h: (0, cb))

    row_tab = pl.BlockSpec((1, W_MIX), lambda h: (0, 0))
    state_blk = (None, HEAD_DIM, HEAD_DIM, rows)
    return pl.pallas_call(
        _ret_sample_kernel,
        grid=(N_HEADS,),
        in_specs=[pl.BlockSpec(memory_space=pltpu.SMEM),
                  col(COL_RQ), col(COL_RK), col(COL_RV), col(COL_RG), row_tab, row_tab,
                  pl.BlockSpec(state_blk, lambda h: (layer * N_HEADS + h, 0, 0, 0))],
        out_specs=[pl.BlockSpec((rows, W_MIX), lambda h: (0, 0)),
                   pl.BlockSpec(state_blk, lambda h: (h, 0, 0, 0))],
        out_shape=[jax.ShapeDtypeStruct((rows, W_MIX), F32),
                   jax.ShapeDtypeStruct((N_HEADS, HEAD_DIM, HEAD_DIM, rows), F32)],
        scratch_shapes=[pltpu.VMEM((W_MIX, rows), F32)] * 4,
        compiler_params=_params("arbitrary"),
        name="ret_sample",
    )(tabs["gamma"], proj, proj, proj, proj, tabs["cos"], tabs["sin"], state_t)


def _suffix_sums(x3, upper):
    n_pages, rows, page = x3.shape
    flat = x3.reshape(n_pages * rows, page)
    within = _dot_split(flat, upper, parts=3)
    total = (within[:, 0:1] + flat[:, 0:1]).reshape(n_pages, rows, 1)
    within = within.reshape(n_pages, rows, page)
    carry = jnp.zeros((rows, 1), F32)
    outs = [None] * n_pages
    for j in range(n_pages - 1, -1, -1):
        outs[j] = within[j] + carry
        carry = carry + total[j]
    return jnp.stack(outs)


def _rows_to_col(row):
    pick = _iota((SUBLANES, W_MIX), 0) == _iota((SUBLANES, W_MIX), 1)
    return jnp.sum(jnp.where(pick, row, 0.0), axis=1, keepdims=True)


def _lane_bcast_col(row):
    return jnp.transpose(jnp.broadcast_to(row, (LANES, W_MIX)))


def _col_to_row(col):
    return jnp.transpose(jnp.broadcast_to(col, (W_MIX, LANES)))[0:1, :]


def _head_scores(kt, qcol):
    page = kt.shape[1]
    prod = kt * qcol
    row = _iota((SUBLANES, page), 0)
    out = jnp.zeros((SUBLANES, page), F32)
    for h in range(N_HEADS):
        part = prod[h * HEAD_DIM:(h + 1) * HEAD_DIM].reshape(HEAD_DIM // SUBLANES, SUBLANES, page)
        tot = jnp.sum(jnp.sum(part, axis=0), axis=0, keepdims=True)
        out = jnp.where(row == h, tot, out)
    return out


def _head_weights(a8):
    page = a8.shape[1]
    return jnp.concatenate([jnp.broadcast_to(a8[h:h + 1, :], (HEAD_DIM, page))
                            for h in range(N_HEADS)], axis=0)


def _decode_kernel(pt_ref, qs_ref, qf_ref, kf_ref, vf_ref, lg_ref, bias_ref, *refs, n_pages):
    sb_pages = refs[0:n_pages]
    fx_pages = refs[n_pages:2 * n_pages]
    lf_pages = refs[2 * n_pages:3 * n_pages]
    sb_o, fx_o, lf_o, lf_scr = refs[3 * n_pages:]
    page = sb_pages[0].shape[2]
    head_rows = _iota((SUBLANES, W_MIX), 0) == (_iota((SUBLANES, W_MIX), 1) // HEAD_DIM)
    upper = (_iota((page, page), 0) > _iota((page, page), 1)).astype(BF16)

    def per_head_col(row):
        return jnp.sum(jnp.where(head_rows, row, 0.0), axis=1, keepdims=True)

    def per_head_lanes(col):
        return jnp.sum(jnp.where(head_rows, col, 0.0), axis=0, keepdims=True)

    qs_col = _lane_bcast_col(qs_ref[...] * ATTN_SCALE)
    z = jnp.stack([_head_scores(sb_pages[j][0], qs_col) for j in range(n_pages)])
    sp = _softplus(z)
    after = _suffix_sums(-sp, upper)
    a = jnp.exp(z - sp + after)
    acc = jnp.zeros((W_MIX, page), F32)
    for j in range(n_pages):
        acc = acc + sb_pages[j][1] * _head_weights(a[j])
    sb_o[...] = _col_to_row(jnp.sum(acc, axis=1, keepdims=True))

    logf_new = jax.nn.log_sigmoid(lg_ref[...] + bias_ref[...])
    lf_o[...] = logf_new
    qf_row = qf_ref[...] * ATTN_SCALE
    qf_col = _lane_bcast_col(qf_row)
    lf_scr[...] = jnp.zeros_like(lf_scr)
    for j in range(n_pages):
        lf_scr[j, 0:N_HEADS, :] = lf_pages[j][...]
    decay = _suffix_sums(lf_scr[...], upper)
    s = jnp.stack([_head_scores(fx_pages[j][0], qf_col) for j in range(n_pages)])
    s = s + decay + _rows_to_col(logf_new)
    s_self = per_head_col(qf_row * kf_ref[...])
    m = jnp.max(jnp.max(s, axis=0), axis=1, keepdims=True)
    m = jnp.maximum(m, s_self)
    p = jnp.exp(s - m)
    p_self = jnp.exp(s_self - m)
    l = jnp.sum(jnp.sum(p, axis=0), axis=1, keepdims=True) + p_self
    p = p / l
    acc = jnp.zeros((W_MIX, page), F32)
    for j in range(n_pages):
        acc = acc + fx_pages[j][1] * _head_weights(p[j])
    fx_o[...] = (_col_to_row(jnp.sum(acc, axis=1, keepdims=True))
                 + per_head_lanes(p_self / l) * vf_ref[...])


def decode_attn(page_table, proj, sb_cache, fox_cache, logf_t, bias_row, layer):
    rows, n_pages = page_table.shape
    page = sb_cache.shape[4]
    proj3 = proj.reshape(rows, 1, proj.shape[1])

    def col(cb):
        return pl.BlockSpec((None, 1, W_MIX), lambda b, pt: (b, 0, cb))

    def kv_page(j):
        return pl.BlockSpec((None, None, 2, W_MIX, page),
                            lambda b, pt, j=j: (pt[b * n_pages + j], layer, 0, 0, 0))

    def lf_page(j):
        return pl.BlockSpec((None, None, N_HEADS, page),
                            lambda b, pt, j=j: (pt[b * n_pages + j], layer, 0, 0))

    out_row = pl.BlockSpec((None, 1, W_MIX), lambda b, pt: (b, 0, 0))
    grid_spec = pltpu.PrefetchScalarGridSpec(
        num_scalar_prefetch=1,
        grid=(rows,),
        in_specs=[col(COL_SBQ), col(COL_FQ), col(COL_FK), col(COL_FV), col(COL_LOGIT),
                  pl.BlockSpec((1, W_MIX), lambda b, pt: (0, 0))]
                 + [kv_page(j) for j in range(n_pages)]
                 + [kv_page(j) for j in range(n_pages)]
                 + [lf_page(j) for j in range(n_pages)],
        out_specs=[out_row, out_row, out_row],
        scratch_shapes=[pltpu.VMEM((n_pages, SUBLANES, page), F32)],
    )
    sb_o, fx_o, lf_o = pl.pallas_call(
        functools.partial(_decode_kernel, n_pages=n_pages),
        grid_spec=grid_spec,
        out_shape=[jax.ShapeDtypeStruct((rows, 1, W_MIX), F32)] * 3,
        compiler_params=_params("arbitrary"),
        name="decode_attn",
    )(page_table.reshape(-1), proj3, proj3, proj3, proj3, proj3, bias_row,
      *([sb_cache] * n_pages), *([fox_cache] * n_pages), *([logf_t] * n_pages))
    return sb_o.reshape(rows, W_MIX), fx_o.reshape(rows, W_MIX), lf_o.reshape(rows, W_MIX)


def _merge_kernel(b0_ref, b1_ref, b2_ref, b3_ref, h_ref, *refs, precise):
    n_w = 6 if precise else 3
    w_refs, (x_ref, gate_ref, o_ref) = refs[:n_w], refs[n_w:]
    wg_ref, wb_ref, wo_ref = w_refs[:3]
    d = x_ref.shape[1]
    h = h_ref[...]
    merged = None
    for i, br in enumerate((b0_ref, b1_ref, b2_ref, b3_ref)):
        cols = slice(i * d, (i + 1) * d)
        if precise:
            gate_logit = _dot3(h, wg_ref[:, cols], w_refs[3][:, cols])
            mixed = _dot3(br[...].astype(F32), wb_ref[i], w_refs[4][i])
        else:
            gate_logit = _dot(h, wg_ref[:, cols])
            mixed = _dot(br[...].astype(BF16), wb_ref[i])
        y = jax.nn.sigmoid(gate_logit) * mixed
        merged = y if merged is None else merged + y
    if precise:
        out = _dot3(merged, wo_ref[...], w_refs[5][...])
    else:
        out = _dot(merged.astype(BF16), wo_ref[...])
    o_ref[...] = x_ref[...] + gate_ref[...] * out


def merge_out(branches, h, weights, layer, x, mod, tm):
    m, d = x.shape
    precise = len(weights) == 6
    br_spec = pl.BlockSpec((tm, W_MIX), lambda i: (i, 0))
    w_specs = [pl.BlockSpec((None, d, N_GATE), lambda i: (layer, 0, 0)),
               pl.BlockSpec((None, N_BRANCH, W_MIX, d), lambda i: (layer, 0, 0, 0)),
               pl.BlockSpec((None, d, d), lambda i: (layer, 0, 0))]
    return pl.pallas_call(
        functools.partial(_merge_kernel, precise=precise),
        grid=(m // tm,),
        in_specs=[br_spec] * N_BRANCH
                 + [pl.BlockSpec((tm, d), lambda i: (i, 0))]
                 + w_specs * (2 if precise else 1)
                 + [pl.BlockSpec((tm, d), lambda i: (i, 0)), mod.spec(2)],
        out_specs=pl.BlockSpec((tm, d), lambda i: (i, 0)),
        out_shape=jax.ShapeDtypeStruct((m, d), F32),
        compiler_params=_params("parallel"),
        name="merge_out",
    )(*branches, h, *weights, x, mod.arr)


def _route(scores, biased):
    lane_i = _iota(scores.shape, 1)
    real = lane_i < N_EXPERTS
    pos = lane_i % EXPERTS_PER_GROUP
    lane = lane_i.astype(F32)
    group = (lane_i // EXPERTS_PER_GROUP).astype(F32)
    neg = -jnp.inf
    n = scores.shape[1]
    mates = [biased]
    for k in range(1, EXPERTS_PER_GROUP):
        fwd = pltpu.roll(biased, n - k, axis=1)
        back = pltpu.roll(biased, EXPERTS_PER_GROUP - k, axis=1)
        mates.append(jnp.where(pos + k < EXPERTS_PER_GROUP, fwd, back))
    group_score = None
    for a in range(EXPERTS_PER_GROUP):
        for b in range(a + 1, EXPERTS_PER_GROUP):
            pair = mates[a] + mates[b]
            group_score = pair if group_score is None else jnp.maximum(group_score, pair)
    group_score = jnp.where(real, group_score, neg)
    best_score = jnp.max(group_score, axis=1, keepdims=True)
    best = jnp.min(jnp.where(group_score == best_score, group, float(n)), axis=1, keepdims=True)
    cand = jnp.where(real & (group == best), biased, neg)
    top0 = jnp.max(cand, axis=1, keepdims=True)
    idx0 = jnp.min(jnp.where(cand == top0, lane, float(n)), axis=1, keepdims=True)
    cand = jnp.where(lane == idx0, neg, cand)
    top1 = jnp.max(cand, axis=1, keepdims=True)
    idx1 = jnp.min(jnp.where(cand == top1, lane, float(n)), axis=1, keepdims=True)
    chosen = (lane == idx0) | (lane == idx1)
    sel = jnp.where(chosen, scores, 0.0)
    return sel / jnp.sum(sel, axis=1, keepdims=True)


def _moe_kernel(x_ref, g_ref, sh_ref, sc_ref, gate_ref, rw_ref, rb_ref, w1_ref, w3_ref, w2_ref,
                gf_ref, o_ref, h_scr, comb_scr, acc_scr, *, final_norm):
    e = pl.program_id(1)
    precise = w1_ref.dtype == F32

    @pl.when(e == 0)
    def _():
        h = _norm_mod(x_ref[...], g_ref[...], sc_ref[...], sh_ref[...])
        h_scr[...] = h.astype(h_scr.dtype)
        if precise:
            logits = _dot3(h, *_split2(rw_ref[...]))
        else:
            logits = _dot(h_scr[...], rw_ref[...].astype(BF16))
        scores = jax.nn.sigmoid(logits)
        comb_scr[...] = _route(scores, scores + rb_ref[...])
        acc_scr[...] = jnp.zeros_like(acc_scr)

    h = h_scr[...]
    if precise:
        a = _dot3(h, *_split2(w1_ref[...]))
        b = _dot3(h, *_split2(w3_ref[...]))
        y = _dot3(_silu(a) * b, *_split2(w2_ref[...]))
    else:
        a = _dot(h, w1_ref[...])
        b = _dot(h, w3_ref[...])
        y = _dot((_silu(a) * b).astype(BF16), w2_ref[...])
    lane = _iota(comb_scr.shape, 1)
    w_e = jnp.sum(jnp.where(lane == e, comb_scr[...], 0.0), axis=1, keepdims=True)
    acc_scr[...] = acc_scr[...] + w_e * y

    @pl.when(e == N_EXPERTS - 1)
    def _():
        out = x_ref[...] + gate_ref[...] * acc_scr[...]
        if final_norm:
            ms = jnp.mean(out * out, axis=-1, keepdims=True)
            out = out * lax.rsqrt(ms + NORM_EPS) * gf_ref[...]
        o_ref[...] = out


def moe(x, g, mod, router_w_pad, router_b_pad, w1, w3, w2, layer, tm, final_gain=None):
    m, d = x.shape
    f = w1.shape[-1]
    expert = lambda i, e: (layer, e, 0, 0)
    final_norm = final_gain is not None
    gf = (final_gain if final_norm else g).reshape(1, d)
    return pl.pallas_call(
        functools.partial(_moe_kernel, final_norm=final_norm),
        grid=(m // tm, N_EXPERTS),
        in_specs=[pl.BlockSpec((tm, d), lambda i, e: (i, 0)),
                  pl.BlockSpec((1, d), lambda i, e: (0, 0)),
                  mod.spec(3), mod.spec(4), mod.spec(5),
                  pl.BlockSpec((d, LANES), lambda i, e: (0, 0)),
                  pl.BlockSpec((1, LANES), lambda i, e: (0, 0)),
                  pl.BlockSpec((None, None, d, f), expert),
                  pl.BlockSpec((None, None, d, f), expert),
                  pl.BlockSpec((None, None, f, d), expert),
                  pl.BlockSpec((1, d), lambda i, e: (0, 0))],
        out_specs=pl.BlockSpec((tm, d), lambda i, e: (i, 0)),
        out_shape=jax.ShapeDtypeStruct((m, d), F32),
        scratch_shapes=[pltpu.VMEM((tm, d), F32 if w1.dtype == F32 else BF16),
                        pltpu.VMEM((tm, LANES), F32), pltpu.VMEM((tm, d), F32)],
        compiler_params=_params("parallel", "arbitrary"),
        name="moe",
    )(x, g.reshape(1, d), mod.arr, mod.arr, mod.arr, router_w_pad, router_b_pad, w1, w3, w2, gf)


def _rope_tables(pos):
    half = HEAD_DIM // 2
    inv = ROPE_BASE ** (-jnp.arange(half, dtype=F32) / half)
    ang = pos.astype(F32)[:, None] * inv[None, :]
    cos, sin = jnp.cos(ang), jnp.sin(ang)
    cos_row = jnp.tile(jnp.concatenate([cos, cos], axis=1), (1, N_HEADS))
    sin_row = jnp.tile(jnp.concatenate([-sin, sin], axis=1), (1, N_HEADS))
    return cos_row, sin_row


def _retention_tables(pos, chunk):
    lg = jnp.log1p(-(2.0 ** (-5.0 - jnp.arange(N_HEADS, dtype=F32))))
    idx = jnp.arange(chunk, dtype=F32)
    diff = idx[:, None] - idx[None, :]
    inner = jnp.where(diff >= 0, jnp.exp(lg[:, None, None] * jnp.maximum(diff, 0.0)), 0.0)
    q_decay = jnp.exp(lg[None, :] * (idx[:, None] + 1.0))
    k_decay = jnp.exp(lg[None, :] * (chunk - 1.0 - idx[:, None]))
    chunk_decay = jnp.exp(lg * chunk)
    lanes = lambda a: jnp.repeat(a, HEAD_DIM, axis=-1)
    cos, sin = _rope_tables(pos)
    return {"cos": cos, "sin": sin, "inner": inner, "q_decay": lanes(q_decay),
            "k_decay": lanes(k_decay), "chunk_decay": lanes(chunk_decay[None, :]),
            "gamma": chunk_decay}


def _pack_w_kernel(w_ref, wm_ref, wg_ref, wm_lo_ref, wg_lo_ref):
    rows = w_ref.shape[0]

    def put(hi_ref, lo_ref, cols, val):
        hi, lo = _split2(val)
        hi_ref[:, cols] = hi
        lo_ref[:, cols] = lo

    put(wm_ref, wm_lo_ref, slice(0, N_MAIN), w_ref[:, 0:N_MAIN])
    lane = _iota((rows, W_MIX), 1)
    put(wm_ref, wm_lo_ref, slice(N_MAIN, GATE_COL0),
        jnp.where(lane < N_HEADS, w_ref[:, N_MAIN:GATE_COL0], 0.0))
    step = 1024
    for c in range(0, N_GATE, step):
        width = min(step + LANES, N_GATE + N_HEADS - c)
        win = w_ref[:, N_MAIN + c:N_MAIN + c + width]
        put(wg_ref, wg_lo_ref, slice(c, c + step), win[:, N_HEADS:N_HEADS + step])


def pack_w_in(w_in):
    depth, d, n_in = w_in.shape
    assert n_in == N_MAIN + N_HEADS + N_GATE
    tr = 256
    main = pl.BlockSpec((None, tr, GATE_COL0), lambda l, r: (l, r, 0))
    gate = pl.BlockSpec((None, tr, N_GATE), lambda l, r: (l, r, 0))
    main_shape = jax.ShapeDtypeStruct((depth, d, GATE_COL0), BF16)
    gate_shape = jax.ShapeDtypeStruct((depth, d, N_GATE), BF16)
    return pl.pallas_call(
        _pack_w_kernel,
        grid=(depth, d // tr),
        in_specs=[pl.BlockSpec((None, tr, n_in), lambda l, r: (l, r, 0))],
        out_specs=[main, gate, main, gate],
        out_shape=[main_shape, gate_shape, main_shape, gate_shape],
        compiler_params=_params("parallel", "parallel"),
        name="pack_w_in",
    )(w_in)


def _split_w_kernel(w_ref, hi_ref, lo_ref):
    hi, lo = _split2(w_ref[...])
    hi_ref[...] = hi
    lo_ref[...] = lo


def split_weights(w):
    cols = w.shape[-1]
    flat = w.reshape(-1, cols)
    tr = 256
    spec = pl.BlockSpec((tr, cols), lambda i: (i, 0))
    shape = jax.ShapeDtypeStruct(flat.shape, BF16)
    hi, lo = pl.pallas_call(
        _split_w_kernel,
        grid=(flat.shape[0] // tr,),
        in_specs=[spec],
        out_specs=[spec, spec],
        out_shape=[shape, shape],
        compiler_params=_params("parallel"),
        name="split_weights",
    )(flat)
    return hi.reshape(w.shape), lo.reshape(w.shape)


def _block_diag_pool(w_pool_l):
    g, c, _ = w_pool_l.shape
    out = jnp.zeros((g * c, g * c), w_pool_l.dtype)
    for i in range(g):
        out = out.at[i * c:(i + 1) * c, i * c:(i + 1) * c].set(w_pool_l[i])
    return out


def _pad_lanes(row, n):
    return jnp.pad(row, ((0, 0), (0, n - row.shape[1])))


def _diag_blocks(s_bd):
    return jnp.stack([s_bd[:, h * HEAD_DIM:(h + 1) * HEAD_DIM, h * HEAD_DIM:(h + 1) * HEAD_DIM]
                      for h in range(N_HEADS)], axis=1)


def kernel(x_prompt, x_sample, cache_sb_kv, cache_fox_kv, cache_fox_logf, state_pool, state_ret,
           page_table, c_prompt, c_sample, w_ada, b_ada, norm_mix, w_in, w_pool, pool_scale,
           fox_bias, w_branch, w_out, norm_ffn, router_w, router_b, w1, w3, w2, norm_final):
    bp, tp, d = x_prompt.shape
    db, ts, _ = x_sample.shape
    assert ts == 1 and d == D_MODEL
    depth = w_in.shape[0]
    n_phys, _, _, page, _, _ = cache_sb_kv.shape
    n_pages = page_table.shape[1]
    past_len = n_pages * page
    mp = bp * tp

    mod_all = adaln(jnp.concatenate([c_prompt, c_sample], axis=0), w_ada, b_ada)

    sb_cache = jnp.transpose(cache_sb_kv, (0, 1, 2, 4, 5, 3)).reshape(n_phys, depth, 2, W_MIX, page)
    fox_cache = jnp.transpose(cache_fox_kv, (0, 1, 2, 4, 5, 3)).reshape(n_phys, depth, 2, W_MIX, page)
    logf_t = jnp.swapaxes(cache_fox_logf, 2, 3)
    state_t = jnp.transpose(state_ret, (1, 2, 3, 4, 0)).reshape(depth * N_HEADS, HEAD_DIM, HEAD_DIM, db)
    w_main, w_gate, w_main_lo, w_gate_lo = pack_w_in(w_in)
    router_w_pad = _pad_lanes(router_w, LANES)
    router_b_pad = _pad_lanes(router_b[None, :], LANES)
    tabs_p = _retention_tables(jnp.arange(tp), RET_CHUNK if tp % RET_CHUNK == 0 else tp)
    tabs_s = _retention_tables(past_len + jnp.arange(ts), ts)
    wb, wb_lo = split_weights(w_branch)
    wo, wo_lo = split_weights(w_out)
    w1b, w3b, w2b = w1.astype(BF16), w3.astype(BF16), w2.astype(BF16)
    w_merge_p = (w_gate, wb, wo)
    w_merge_s = w_merge_p + (w_gate_lo, wb_lo, wo_lo)

    tm_p = min(1024, tp)
    tm_mrg = min(512, tp)
    xp = x_prompt.reshape(mp, d)
    xs = x_sample.reshape(db, d)
    outs_p = {k: [] for k in ("logf", "pool", "ret")}
    outs_s = {k: [] for k in ("sb", "fox", "logf", "pool", "ret")}
    kv_prev = None

    for l in range(depth):
        w_bd_f32 = _block_diag_pool(w_pool[l])
        w_bd = w_bd_f32.astype(BF16)
        scale_row = pool_scale[l][None, :]
        bias_lanes = _pad_lanes(fox_bias[l][None, :], LANES)
        bias_row = _pad_lanes(fox_bias[l][None, :], W_MIX)

        mod_in = Mod(mod_all[l, :bp], tp, tm_p)
        mod_mrg = Mod(mod_all[l, :bp], tp, tm_mrg)
        final_gain = norm_final if l == depth - 1 else None
        kv_sb, kv_fox, proj, h = in_proj(xp, norm_mix[l], mod_in, w_main, l, tm_p,
                                         kv_seq=(bp, tp), kv_prev=kv_prev)
        kv_prev = (kv_sb, kv_fox)
        pool_o = pool_prompt(proj, w_bd, scale_row, bp, tp)
        sb_o = sb_prompt(proj, bp, tp)
        ret_o, ret_state = ret_prompt(proj, tabs_p, bp, tp)
        logf, cum, cum_t = logf_cum_prompt(proj, bias_lanes, bp, tp)
        fox_o = fox_prompt(proj, cum, cum_t, bp, tp)
        xp = merge_out((pool_o, sb_o, ret_o, fox_o), h, w_merge_p, l, xp, mod_mrg, tm_mrg)
        xp = moe(xp, norm_ffn[l], mod_in, router_w_pad, router_b_pad, w1b, w3b, w2b, l, tm_p,
                 final_gain)
        p3 = proj.reshape(bp, tp, GATE_COL0)
        outs_p["logf"].append(logf.reshape(bp, tp, LANES)[:, :, :N_HEADS])
        outs_p["pool"].append(p3[:, tp - POOL_BUF:, :W_MIX])
        outs_p["ret"].append(_diag_blocks(ret_state))

        mod_s = Mod(mod_all[l, bp:], 1, db)
        proj_s, h_s = in_proj(xs, norm_mix[l], mod_s, w_main, l, db, w_lo=w_main_lo)
        buf_t = jnp.swapaxes(state_pool[:, l], 0, 1)
        pool_s = pool_sample(buf_t, proj_s, w_bd_f32, scale_row, past_len)
        sb_s, fox_s, logf_s = decode_attn(page_table, proj_s, sb_cache, fox_cache, logf_t, bias_row, l)
        ret_s, state_new = ret_sample(proj_s, state_t, l, tabs_s)
        xs = merge_out((pool_s, sb_s, ret_s, fox_s), h_s, w_merge_s, l, xs, mod_s, db)
        xs = moe(xs, norm_ffn[l], mod_s, router_w_pad, router_b_pad, w1, w3, w2, l, db, final_gain)

        def kv_s(cb, proj_s=proj_s):
            return jnp.stack([proj_s[:, cb * W_MIX:(cb + 1) * W_MIX],
                              proj_s[:, (cb + 1) * W_MIX:(cb + 2) * W_MIX]], axis=1)

        outs_s["sb"].append(kv_s(COL_SBK))
        outs_s["fox"].append(kv_s(COL_FK))
        outs_s["logf"].append(logf_s[:, :N_HEADS])
        outs_s["pool"].append(jnp.concatenate([state_pool[:, l, 1:], proj_s[:, None, :W_MIX]], axis=1))
        outs_s["ret"].append(state_new)

    y_p = xp.reshape(bp, tp, d)
    y_s = xs.reshape(db, ts, d)

    def heads(a, t):
        return a.reshape(a.shape[0], depth, 2, t, N_HEADS, HEAD_DIM)

    def from_transposed(kv):
        return heads(jnp.transpose(kv, (2, 0, 1, 4, 3)), tp)

    return (y_p, y_s,
            from_transposed(kv_prev[0]),
            heads(jnp.stack(outs_s["sb"], axis=1), ts),
            from_transposed(kv_prev[1]),
            heads(jnp.stack(outs_s["fox"], axis=1), ts),
            jnp.stack(outs_p["logf"], axis=1),
            jnp.stack(outs_s["logf"], axis=1).reshape(db, depth, ts, N_HEADS),
            jnp.stack(outs_p["pool"], axis=1),
            jnp.stack(outs_s["pool"], axis=1),
            jnp.stack(outs_p["ret"], axis=1),
            jnp.transpose(jnp.stack(outs_s["ret"], axis=0), (4, 0, 1, 2, 3)))
```

```python
import functools

import numpy as np
import jax
import jax.numpy as jnp
from jax import lax
from jax.experimental import pallas as pl
from jax.experimental.pallas import tpu as pltpu

F32 = jnp.float32
BF16 = jnp.bfloat16

D_MODEL = 1024
HEAD_DIM = 64
W_MIX = 256
N_HEADS = W_MIX // HEAD_DIM
N_BRANCH = 4
POOL_WINDOWS = (2, 4, 8, 16)
POOL_BUF = 15
RET_CHUNK = 128
ROPE_BASE = 10000.0
N_EXPERTS = 16
EXPERTS_PER_GROUP = 4
D_EXPERT = 512
N_MOD = 6
NORM_EPS = 1e-6
ATTN_SCALE = HEAD_DIM ** -0.5
LANES = 128
SUBLANES = 8
VMEM_LIMIT = 56 * 1024 * 1024

COL_U, COL_SBQ, COL_SBK, COL_SBV = 0, 1, 2, 3
COL_RQ, COL_RK, COL_RV, COL_RG = 4, 5, 6, 7
COL_FQ, COL_FK, COL_FV, COL_LOGIT = 8, 9, 10, 11
N_MAIN = 11 * W_MIX
GATE_COL0 = 12 * W_MIX
N_GATE = N_BRANCH * D_MODEL


def _params(*sem):
    return pltpu.CompilerParams(dimension_semantics=sem, vmem_limit_bytes=VMEM_LIMIT)


def _split2(x):
    hi = x.astype(BF16)
    lo = (x - hi.astype(F32)).astype(BF16)
    return hi, lo


def _split3(x):
    hi = x.astype(BF16)
    r = x - hi.astype(F32)
    mid = r.astype(BF16)
    lo = (r - mid.astype(F32)).astype(BF16)
    return hi, mid, lo


def _dot(a, b):
    return jnp.dot(a, b, preferred_element_type=F32)


def _dot_nt(a, b):
    return lax.dot_general(a, b, (((1,), (1,)), ((), ())), preferred_element_type=F32)


def _dot_tn(a, b):
    return lax.dot_general(a, b, (((0,), (0,)), ((), ())), preferred_element_type=F32)


def _dot_split(x, w, parts=2):
    ps = _split2(x) if parts == 2 else _split3(x)
    acc = _dot(ps[0], w)
    for p in ps[1:]:
        acc = acc + _dot(p, w)
    return acc


def _dot3(x, w_hi, w_lo):
    x_hi, x_lo = _split2(x)
    return _dot(x_hi, w_hi) + (_dot(x_lo, w_hi) + _dot(x_hi, w_lo))


def _iota(shape, dim):
    return lax.broadcasted_iota(jnp.int32, shape, dim)


def _softplus(z):
    return jnp.maximum(z, 0.0) + jnp.log1p(jnp.exp(-jnp.abs(z)))


def _silu(x):
    return x * jax.nn.sigmoid(x)


def _adaln_kernel(c_ref, w_ref, b_ref, o_ref):
    w_hi, w_lo = _split2(w_ref[...])
    o_ref[...] = _dot3(_silu(c_ref[...]), w_hi, w_lo) + b_ref[...]


def adaln(c_all, w_ada, b_ada):
    rows, d = c_all.shape
    depth, _, n = w_ada.shape
    tn = 1024
    return pl.pallas_call(
        _adaln_kernel,
        grid=(depth, n // tn),
        in_specs=[pl.BlockSpec((rows, d), lambda l, j: (0, 0)),
                  pl.BlockSpec((None, d, tn), lambda l, j: (l, 0, j)),
                  pl.BlockSpec((None, 1, tn), lambda l, j: (l, 0, j))],
        out_specs=pl.BlockSpec((None, rows, tn), lambda l, j: (l, 0, j)),
        out_shape=jax.ShapeDtypeStruct((depth, rows, n), F32),
        compiler_params=_params("parallel", "parallel"),
        name="adaln",
    )(c_all, w_ada, b_ada.reshape(depth, 1, n))


class Mod:
    def __init__(self, arr, rows_per_vec, tm):
        self.per_row = rows_per_vec == 1
        self.arr = arr if self.per_row else arr.reshape(arr.shape[0], 1, arr.shape[-1])
        self.tiles_per_vec = 1 if self.per_row else rows_per_vec // tm
        self.tm = tm

    def spec(self, k):
        if self.per_row:
            return pl.BlockSpec((self.tm, D_MODEL), lambda i, *_: (i, k))
        t = self.tiles_per_vec
        return pl.BlockSpec((None, 1, D_MODEL), lambda i, *_: (i // t, 0, k))


def _norm_mod(x, g, sc, sh):
    ms = jnp.mean(x * x, axis=-1, keepdims=True)
    y = x * lax.rsqrt(ms + NORM_EPS) * g
    return y * (1.0 + sc) + sh


_PROJ_TN = 1024


def _inproj_kernel(x_ref, g_ref, sh_ref, sc_ref, w_ref, *refs, kv_out, precise):
    j = pl.program_id(1)
    o_ref, h_ref = refs[-2], refs[-1]

    @pl.when(j == 0)
    def _():
        h_ref[...] = _norm_mod(x_ref[...], g_ref[...], sc_ref[...], sh_ref[...]).astype(h_ref.dtype)

    if precise:
        res = _dot3(h_ref[...], w_ref[...], refs[0][...])
    else:
        res = _dot(h_ref[...], w_ref[...])
    o_ref[...] = res
    if kv_out:
        per_tile = _PROJ_TN // W_MIX
        for col_k, kv_ref in ((COL_SBK, refs[-4]), (COL_FK, refs[-3])):
            c0 = (col_k % per_tile) * W_MIX

            @pl.when(j == col_k // per_tile)
            def _(c0=c0, kv_ref=kv_ref):
                kv_ref[0] = jnp.transpose(res[:, c0:c0 + W_MIX])
                kv_ref[1] = jnp.transpose(res[:, c0 + W_MIX:c0 + 2 * W_MIX])


def in_proj(x, g, mod, w_all, layer, tm, kv_seq=None, kv_prev=None, w_lo=None):
    m, d = x.shape
    depth, _, n = w_all.shape
    tn = _PROJ_TN
    precise = w_lo is not None
    w_spec = pl.BlockSpec((None, d, tn), lambda i, j: (layer, 0, j))
    in_specs = [pl.BlockSpec((tm, d), lambda i, j: (i, 0)),
                pl.BlockSpec((1, d), lambda i, j: (0, 0)),
                mod.spec(0), mod.spec(1), w_spec]
    args = [x, g.reshape(1, d), mod.arr, mod.arr, w_all]
    if precise:
        in_specs.append(w_spec)
        args.append(w_lo)
    out_specs = [pl.BlockSpec((tm, tn), lambda i, j: (i, j)),
                 pl.BlockSpec((tm, d), lambda i, j: (i, 0))]
    out_shape = [jax.ShapeDtypeStruct((m, n), F32),
                 jax.ShapeDtypeStruct((m, d), F32 if precise else BF16)]
    aliases = {}
    if kv_seq is not None:
        b, t = kv_seq
        per_b = t // tm
        kv_spec = pl.BlockSpec((None, 2, None, W_MIX, tm),
                               lambda i, j: (layer, 0, i // per_b, 0, i % per_b))
        kv_shape = jax.ShapeDtypeStruct((depth, 2, b, W_MIX, t), F32)
        out_specs = [kv_spec, kv_spec] + out_specs
        out_shape = [kv_shape, kv_shape] + out_shape
        if kv_prev is not None:
            aliases = {len(args): 0, len(args) + 1: 1}
            in_specs += [pl.BlockSpec(memory_space=pl.ANY)] * 2
            args += list(kv_prev)
    outs = pl.pallas_call(
        functools.partial(_inproj_kernel, kv_out=kv_seq is not None, precise=precise),
        grid=(m // tm, n // tn),
        in_specs=in_specs,
        out_specs=out_specs,
        out_shape=out_shape,
        input_output_aliases=aliases,
        compiler_params=_params("parallel", "arbitrary"),
        name="in_proj",
    )(*args)
    return outs


def _pool_tail(s2, s4, s8, s16, u, pos0, w_ref, sc_ref, o_ref):
    t = u.shape[0]
    lane = _iota((1, W_MIX), 1) // (W_MIX // len(POOL_WINDOWS))
    win = jnp.where(lane == 0, s2, jnp.where(lane == 1, s4, jnp.where(lane == 2, s8, s16)))
    width = jnp.where(lane == 0, 2, jnp.where(lane == 1, 4, jnp.where(lane == 2, 8, 16)))
    count = jnp.minimum(width, pos0 + 1 + _iota((t, W_MIX), 0)).astype(F32)
    resid = win / count - u
    if w_ref.dtype == F32:
        w_hi, w_lo = _split2(w_ref[...])
        mixed = _dot3(resid, w_hi, w_lo)
    else:
        mixed = _dot(resid.astype(BF16), w_ref[...])
    o_ref[...] = (mixed * sc_ref[...]).astype(o_ref.dtype)


_POOL_PAD = 32


def _pool_prompt_kernel(u_ref, w_ref, sc_ref, o_ref, a_scr, b_scr, c_scr):
    t = u_ref.shape[0]
    p = _POOL_PAD
    u = u_ref[...]
    a_scr[0:p, :] = jnp.zeros((p, W_MIX), F32)
    a_scr[p:p + t, :] = u

    def stage(src, dst, k, lo):
        n = p + t - lo
        dst[lo:lo + n, :] = src[lo:lo + n, :] + src[lo - k:lo - k + n, :]

    stage(a_scr, b_scr, 1, 8)
    stage(b_scr, c_scr, 2, 16)
    stage(c_scr, a_scr, 4, 24)
    s8 = a_scr[p:p + t, :]
    s16 = s8 + a_scr[p - 8:p - 8 + t, :]
    _pool_tail(b_scr[p:p + t, :], c_scr[p:p + t, :], s8, s16, u, 0, w_ref, sc_ref, o_ref)


def pool_prompt(proj, w_bd, scale, b, t):
    return pl.pallas_call(
        _pool_prompt_kernel,
        grid=(b,),
        in_specs=[pl.BlockSpec((t, W_MIX), lambda i: (i, COL_U)),
                  pl.BlockSpec((W_MIX, W_MIX), lambda i: (0, 0)),
                  pl.BlockSpec((1, W_MIX), lambda i: (0, 0))],
        out_specs=pl.BlockSpec((t, W_MIX), lambda i: (i, 0)),
        out_shape=jax.ShapeDtypeStruct((b * t, W_MIX), BF16),
        scratch_shapes=[pltpu.VMEM((t + _POOL_PAD, W_MIX), F32)] * 3,
        compiler_params=_params("parallel"),
        name="pool_prompt",
    )(proj, w_bd, scale)


def _pool_sample_kernel(buf_ref, u_ref, w_ref, sc_ref, o_ref, *, pos0):
    u = u_ref[...]
    s2 = u + buf_ref[14]
    s4 = s2 + buf_ref[13] + buf_ref[12]
    s8 = s4 + buf_ref[11] + buf_ref[10] + buf_ref[9] + buf_ref[8]
    s16 = s8
    for r in range(7, -1, -1):
        s16 = s16 + buf_ref[r]
    _pool_tail(s2, s4, s8, s16, u, pos0, w_ref, sc_ref, o_ref)


def pool_sample(buf_t, proj, w_bd, scale, pos0):
    rows = proj.shape[0]
    return pl.pallas_call(
        functools.partial(_pool_sample_kernel, pos0=pos0),
        grid=(1,),
        in_specs=[pl.BlockSpec((POOL_BUF, rows, W_MIX), lambda i: (0, 0, 0)),
                  pl.BlockSpec((rows, W_MIX), lambda i: (0, COL_U)),
                  pl.BlockSpec((W_MIX, W_MIX), lambda i: (0, 0)),
                  pl.BlockSpec((1, W_MIX), lambda i: (0, 0))],
        out_specs=pl.BlockSpec((rows, W_MIX), lambda i: (0, 0)),
        out_shape=jax.ShapeDtypeStruct((rows, W_MIX), F32),
        compiler_params=_params("arbitrary"),
        name="pool_sample",
    )(buf_t, proj, w_bd, scale)


_ATT_TILE = 256
_ATT_CHAINS = 1


def _head_of_lane():
    return _iota((1, W_MIX), 1) // HEAD_DIM


def _stack_heads(q):
    head = _head_of_lane()
    return jnp.concatenate([jnp.where(head == h, q, 0.0) for h in range(N_HEADS)], axis=0).astype(BF16)


def _unstack_heads(acc, tq):
    head = _head_of_lane()
    out = jnp.where(head == 0, acc[0:tq], 0.0)
    for h in range(1, N_HEADS):
        out = out + jnp.where(head == h, acc[h * tq:(h + 1) * tq], 0.0)
    return out


def _load_kv_once(k_ref, v_ref, k_scr, v_scr):
    @pl.when(pl.program_id(1) == 0)
    def _():
        k_scr[...] = k_ref[...].astype(BF16)
        v_scr[...] = v_ref[...].astype(BF16)


def _sb_prompt_kernel(q_ref, k_ref, v_ref, o_ref, k_scr, v_scr):
    tq = tk = _ATT_TILE
    i = pl.program_id(1)
    _load_kv_once(k_ref, v_ref, k_scr, v_scr)
    q4 = _stack_heads(q_ref[...] * ATTN_SCALE)
    rows = N_HEADS * tq // _ATT_CHAINS
    q_parts = [q4[c * rows:(c + 1) * rows] for c in range(_ATT_CHAINS)]
    below_diag = _iota((rows, tk), 1) < (_iota((rows, tk), 0) % tq)
    upper = (_iota((tk, tk), 0) > _iota((tk, tk), 1)).astype(BF16)

    def block(off, diagonal, carry):
        kb = k_scr[pl.ds(off, tk), :]
        vb = v_scr[pl.ds(off, tk), :]
        out = []
        for qp, (acc, run) in zip(q_parts, carry):
            z = _dot_nt(qp, kb)
            sp = jnp.maximum(z, 0.0) + jnp.log(1.0 + jnp.exp(-jnp.abs(z)))
            log_not = jnp.where(below_diag, -sp, 0.0) if diagonal else -sp
            suffix = _dot_split(log_not, upper)
            a = jnp.exp(z - sp + suffix + run)
            if diagonal:
                a = jnp.where(below_diag, a, 0.0)
            out.append((acc + _dot(a.astype(BF16), vb), run + suffix[:, 0:1] + log_not[:, 0:1]))
        return tuple(out)

    zero = (jnp.zeros((rows, W_MIX), F32), jnp.zeros((rows, 1), F32))
    carry = block(pl.multiple_of(i * tk, tk), True, (zero,) * _ATT_CHAINS)
    carry = lax.fori_loop(
        0, i, lambda s, c: block(pl.multiple_of((i - 1 - s) * tk, tk), False, c), carry)
    acc = jnp.concatenate([c[0] for c in carry], axis=0)
    o_ref[...] = _unstack_heads(acc, tq).astype(o_ref.dtype)


def _attn_prompt_specs(t, col_q, col_k, col_v):
    tq = _ATT_TILE
    nq = t // tq
    return [pl.BlockSpec((tq, W_MIX), lambda b, i: (b * nq + i, col_q)),
            pl.BlockSpec((t, W_MIX), lambda b, i: (b, col_k)),
            pl.BlockSpec((t, W_MIX), lambda b, i: (b, col_v))]


def sb_prompt(proj, b, t):
    tq = _ATT_TILE
    nq = t // tq
    return pl.pallas_call(
        _sb_prompt_kernel,
        grid=(b, nq),
        in_specs=_attn_prompt_specs(t, COL_SBQ, COL_SBK, COL_SBV),
        out_specs=pl.BlockSpec((tq, W_MIX), lambda bb, i: (bb * nq + i, 0)),
        out_shape=jax.ShapeDtypeStruct((b * t, W_MIX), BF16),
        scratch_shapes=[pltpu.VMEM((t, W_MIX), BF16)] * 2,
        compiler_params=_params("parallel", "arbitrary"),
        name="sb_prompt",
    )(proj, proj, proj)


def _logf_cum_kernel(x_ref, bias_ref, logf_ref, cum_ref, cumt_ref):
    t = x_ref.shape[0]
    blk = 256
    logf = jax.nn.log_sigmoid(x_ref[...] + bias_ref[...])
    logf_ref[...] = logf
    lower = (_iota((blk, blk), 1) <= _iota((blk, blk), 0)).astype(BF16)
    carry = jnp.zeros((1, LANES), F32)
    for c in range(t // blk):
        part = logf[c * blk:(c + 1) * blk]
        hi, mid, lo = _split3(part)
        cum = _dot(lower, hi) + _dot(lower, mid) + _dot(lower, lo) + carry
        cum_ref[c * blk:(c + 1) * blk, :] = cum
        carry = cum[blk - 1:blk, :]
    cumt_ref[...] = jnp.transpose(cum_ref[...])[0:SUBLANES, :]


def logf_cum_prompt(proj, bias_pad, b, t):
    m = b * t
    return pl.pallas_call(
        _logf_cum_kernel,
        grid=(b,),
        in_specs=[pl.BlockSpec((t, LANES), lambda i: (i, COL_LOGIT * W_MIX // LANES)),
                  pl.BlockSpec((1, LANES), lambda i: (0, 0))],
        out_specs=[pl.BlockSpec((t, LANES), lambda i: (i, 0)),
                   pl.BlockSpec((t, LANES), lambda i: (i, 0)),
                   pl.BlockSpec((None, SUBLANES, t), lambda i: (i, 0, 0))],
        out_shape=[jax.ShapeDtypeStruct((m, LANES), F32),
                   jax.ShapeDtypeStruct((m, LANES), F32),
                   jax.ShapeDtypeStruct((b, SUBLANES, t), F32)],
        compiler_params=_params("parallel"),
        name="logf_cum",
    )(proj, bias_pad)


def _fox_prompt_kernel(q_ref, k_ref, v_ref, cq_ref, ck_ref, o_ref, k_scr, v_scr):
    tq = tk = _ATT_TILE
    i = pl.program_id(1)
    _load_kv_once(k_ref, v_ref, k_scr, v_scr)
    q4 = _stack_heads(q_ref[...] * ATTN_SCALE)
    rows = N_HEADS * tq
    on_or_below_diag = _iota((rows, tk), 1) <= (_iota((rows, tk), 0) % tq)
    cq = cq_ref[...]

    def block(off, diagonal, carry):
        acc, m, l = carry
        kb = k_scr[pl.ds(off, tk), :]
        vb = v_scr[pl.ds(off, tk), :]
        z = _dot_nt(q4, kb)
        sc = jnp.concatenate(
            [z[h * tq:(h + 1) * tq] + (cq[:, h:h + 1] - ck_ref[h:h + 1, pl.ds(off, tk)])
             for h in range(N_HEADS)], axis=0)
        if diagonal:
            sc = jnp.where(on_or_below_diag, sc, -jnp.inf)
        m_new = jnp.maximum(m, jnp.max(sc, axis=1, keepdims=True))
        alpha = jnp.exp(m - m_new)
        p = jnp.exp(sc - m_new)
        return (alpha * acc + _dot(p.astype(BF16), vb), m_new,
                alpha * l + jnp.sum(p, axis=1, keepdims=True))

    carry = block(pl.multiple_of(i * tk, tk), True,
                  (jnp.zeros((rows, W_MIX), F32), jnp.full((rows, 1), -jnp.inf, F32),
                   jnp.zeros((rows, 1), F32)))
    acc, _, l = lax.fori_loop(
        0, i, lambda s, c: block(pl.multiple_of((i - 1 - s) * tk, tk), False, c), carry)
    o_ref[...] = _unstack_heads(acc / l, tq).astype(o_ref.dtype)


def fox_prompt(proj, cum, cum_t, b, t):
    tq = _ATT_TILE
    nq = t // tq
    specs = _attn_prompt_specs(t, COL_FQ, COL_FK, COL_FV)
    specs += [pl.BlockSpec((tq, LANES), lambda bb, i: (bb * nq + i, 0)),
              pl.BlockSpec((None, SUBLANES, t), lambda bb, i: (bb, 0, 0))]
    return pl.pallas_call(
        _fox_prompt_kernel,
        grid=(b, nq),
        in_specs=specs,
        out_specs=pl.BlockSpec((tq, W_MIX), lambda bb, i: (bb * nq + i, 0)),
        out_shape=jax.ShapeDtypeStruct((b * t, W_MIX), BF16),
        scratch_shapes=[pltpu.VMEM((t, W_MIX), BF16)] * 2,
        compiler_params=_params("parallel", "arbitrary"),
        name="fox_prompt",
    )(proj, proj, proj, cum, cum_t)


def _rotary(x, cos, sin_signed):
    half = HEAD_DIM // 2
    first = (_iota((1, W_MIX), 1) % HEAD_DIM) < half
    swapped = jnp.where(first, pltpu.roll(x, W_MIX - half, axis=1), pltpu.roll(x, half, axis=1))
    return x * cos + swapped * sin_signed


def _head_mean_matrix():
    same = (_iota((W_MIX, W_MIX), 0) // HEAD_DIM) == (_iota((W_MIX, W_MIX), 1) // HEAD_DIM)
    return jnp.where(same, 1.0 / HEAD_DIM, 0.0).astype(BF16)


def _head_norm_gate(o, gate_logit):
    avg = _head_mean_matrix()
    mu = _dot_split(o, avg)
    dev = o - mu
    var = _dot_split(dev * dev, avg)
    return _silu(gate_logit) * (dev * lax.rsqrt(var + NORM_EPS))


def _ret_prompt_kernel(q_ref, k_ref, v_ref, g_ref, cos_ref, sin_ref, inner_ref, qd_ref, kd_ref,
                       cd_ref, o_ref, s_ref):
    c = pl.program_id(1)

    @pl.when(c == 0)
    def _():
        s_ref[...] = jnp.zeros_like(s_ref)

    cos = cos_ref[...]
    sin = sin_ref[...]
    ch = cos.shape[0]
    same = (_iota((W_MIX, W_MIX), 0) // HEAD_DIM) == (_iota((W_MIX, W_MIX), 1) // HEAD_DIM)
    inner = inner_ref[...].reshape(N_HEADS * ch, ch)
    for s in range(q_ref.shape[0]):
        q = _rotary(q_ref[s], cos, sin)
        k = _rotary(k_ref[s], cos, sin) * ATTN_SCALE
        kb = k.astype(BF16)
        vb = v_ref[s].astype(BF16)
        state = s_ref[s]
        att = _dot_nt(_stack_heads(q), kb) * inner
        o = _dot(q.astype(BF16), state.astype(BF16)) * qd_ref[...]
        o = o + _unstack_heads(_dot(att.astype(BF16), vb), ch)
        kv = _dot_tn((k * kd_ref[...]).astype(BF16), vb)
        s_ref[s] = state * cd_ref[...] + jnp.where(same, kv, 0.0)
        o_ref[s] = _head_norm_gate(o, g_ref[s]).astype(o_ref.dtype)


def ret_prompt(proj, tabs, b, t):
    ch = RET_CHUNK
    nc = t // ch
    nb = max(n for n in (4, 2, 1) if b % n == 0)
    proj3 = proj.reshape(b, t, proj.shape[1])

    def col(cb):
        return pl.BlockSpec((nb, ch, W_MIX), lambda bb, c: (bb, c, cb))

    full = lambda shape: pl.BlockSpec(shape, lambda bb, c: (0,) * len(shape))
    out, state = pl.pallas_call(
        _ret_prompt_kernel,
        grid=(b // nb, nc),
        in_specs=[col(COL_RQ), col(COL_RK), col(COL_RV), col(COL_RG),
                  pl.BlockSpec((ch, W_MIX), lambda bb, c: (c, 0)),
                  pl.BlockSpec((ch, W_MIX), lambda bb, c: (c, 0)),
                  full((N_HEADS, ch, ch)), full((ch, W_MIX)), full((ch, W_MIX)), full((1, W_MIX))],
        out_specs=[pl.BlockSpec((nb, ch, W_MIX), lambda bb, c: (bb, c, 0)),
                   pl.BlockSpec((nb, W_MIX, W_MIX), lambda bb, c: (bb, 0, 0))],
        out_shape=[jax.ShapeDtypeStruct((b, t, W_MIX), BF16),
                   jax.ShapeDtypeStruct((b, W_MIX, W_MIX), F32)],
        compiler_params=_params("parallel", "arbitrary"),
        name="ret_prompt",
    )(proj3, proj3, proj3, proj3, tabs["cos"], tabs["sin"], tabs["inner"], tabs["q_decay"],
      tabs["k_decay"], tabs["chunk_decay"])
    return out.reshape(b * t, W_MIX), state


def _ret_sample_kernel(gam_ref, q_ref, k_ref, v_ref, g_ref, cos_ref, sin_ref, s_ref,
                       o_ref, sn_ref, qt_scr, kt_scr, vt_scr, ot_scr):
    h = pl.program_id(0)
    rows = q_ref.shape[0]

    @pl.when(h == 0)
    def _():
        cos = cos_ref[...]
        sin = sin_ref[...]
        qt_scr[...] = jnp.transpose(_rotary(q_ref[...], cos, sin))
        kt_scr[...] = jnp.transpose(_rotary(k_ref[...], cos, sin) * ATTN_SCALE)
        vt_scr[...] = jnp.transpose(v_ref[...])

    row0 = pl.multiple_of(h * HEAD_DIM, HEAD_DIM)
    gamma = gam_ref[h]
    vh = vt_scr[pl.ds(row0, HEAD_DIM), :]

    def body(d, cross):
        qd = qt_scr[pl.ds(row0 + d, 1), :]
        kd = kt_scr[pl.ds(row0 + d, 1), :]
        s_d = s_ref[d]
        sn_ref[d] = gamma * s_d + kd * vh
        return cross + qd * s_d

    cross = lax.fori_loop(0, HEAD_DIM, body, jnp.zeros((HEAD_DIM, rows), F32))
    qk = jnp.sum(qt_scr[pl.ds(row0, HEAD_DIM), :] * kt_scr[pl.ds(row0, HEAD_DIM), :],
                 axis=0, keepdims=True)
    ot_scr[pl.ds(row0, HEAD_DIM), :] = qk * vh + cross * gamma

    @pl.when(h == N_HEADS - 1)
    def _():
        o_ref[...] = _head_norm_gate(jnp.transpose(ot_scr[...]), g_ref[...])


def ret_sample(proj, state_t, layer, tabs):
    rows = proj.shape[0]

    def col(cb):
        return pl.BlockSpec((rows, W_MIX), lambda h: (0, cb))

    row_tab = pl.BlockSpec((1, W_MIX), lambda h: (0, 0))
    state_blk = (None, HEAD_DIM, HEAD_DIM, rows)
    return pl.pallas_call(
        _ret_sample_kernel,
        grid=(N_HEADS,),
        in_specs=[pl.BlockSpec(memory_space=pltpu.SMEM),
                  col(COL_RQ), col(COL_RK), col(COL_RV), col(COL_RG), row_tab, row_tab,
                  pl.BlockSpec(state_blk, lambda h: (layer * N_HEADS + h, 0, 0, 0))],
        out_specs=[pl.BlockSpec((rows, W_MIX), lambda h: (0, 0)),
                   pl.BlockSpec(state_blk, lambda h: (h, 0, 0, 0))],
        out_shape=[jax.ShapeDtypeStruct((rows, W_MIX), F32),
                   jax.ShapeDtypeStruct((N_HEADS, HEAD_DIM, HEAD_DIM, rows), F32)],
        scratch_shapes=[pltpu.VMEM((W_MIX, rows), F32)] * 4,
        compiler_params=_params("arbitrary"),
        name="ret_sample",
    )(tabs["gamma"], proj, proj, proj, proj, tabs["cos"], tabs["sin"], state_t)


def _suffix_sums(x3, upper):
    n_pages, rows, page = x3.shape
    flat = x3.reshape(n_pages * rows, page)
    within = _dot_split(flat, upper, parts=3)
    total = (within[:, 0:1] + flat[:, 0:1]).reshape(n_pages, rows, 1)
    within = within.reshape(n_pages, rows, page)
    carry = jnp.zeros((rows, 1), F32)
    outs = [None] * n_pages
    for j in range(n_pages - 1, -1, -1):
        outs[j] = within[j] + carry
        carry = carry + total[j]
    return jnp.stack(outs)


def _rows_to_col(row):
    pick = _iota((SUBLANES, W_MIX), 0) == _iota((SUBLANES, W_MIX), 1)
    return jnp.sum(jnp.where(pick, row, 0.0), axis=1, keepdims=True)


def _lane_bcast_col(row):
    return jnp.transpose(jnp.broadcast_to(row, (LANES, W_MIX)))


def _col_to_row(col):
    return jnp.transpose(jnp.broadcast_to(col, (W_MIX, LANES)))[0:1, :]


def _head_scores(kt, qcol):
    page = kt.shape[1]
    prod = kt * qcol
    row = _iota((SUBLANES, page), 0)
    out = jnp.zeros((SUBLANES, page), F32)
    for h in range(N_HEADS):
        part = prod[h * HEAD_DIM:(h + 1) * HEAD_DIM].reshape(HEAD_DIM // SUBLANES, SUBLANES, page)
        tot = jnp.sum(jnp.sum(part, axis=0), axis=0, keepdims=True)
        out = jnp.where(row == h, tot, out)
    return out


def _head_weights(a8):
    page = a8.shape[1]
    return jnp.concatenate([jnp.broadcast_to(a8[h:h + 1, :], (HEAD_DIM, page))
                            for h in range(N_HEADS)], axis=0)


def _decode_kernel(pt_ref, qs_ref, qf_ref, kf_ref, vf_ref, lg_ref, bias_ref, *refs, n_pages):
    sb_pages = refs[0:n_pages]
    fx_pages = refs[n_pages:2 * n_pages]
    lf_pages = refs[2 * n_pages:3 * n_pages]
    sb_o, fx_o, lf_o, lf_scr = refs[3 * n_pages:]
    page = sb_pages[0].shape[2]
    head_rows = _iota((SUBLANES, W_MIX), 0) == (_iota((SUBLANES, W_MIX), 1) // HEAD_DIM)
    upper = (_iota((page, page), 0) > _iota((page, page), 1)).astype(BF16)

    def per_head_col(row):
        return jnp.sum(jnp.where(head_rows, row, 0.0), axis=1, keepdims=True)

    def per_head_lanes(col):
        return jnp.sum(jnp.where(head_rows, col, 0.0), axis=0, keepdims=True)

    qs_col = _lane_bcast_col(qs_ref[...] * ATTN_SCALE)
    z = jnp.stack([_head_scores(sb_pages[j][0], qs_col) for j in range(n_pages)])
    sp = _softplus(z)
    after = _suffix_sums(-sp, upper)
    a = jnp.exp(z - sp + after)
    acc = jnp.zeros((W_MIX, page), F32)
    for j in range(n_pages):
        acc = acc + sb_pages[j][1] * _head_weights(a[j])
    sb_o[...] = _col_to_row(jnp.sum(acc, axis=1, keepdims=True))

    logf_new = jax.nn.log_sigmoid(lg_ref[...] + bias_ref[...])
    lf_o[...] = logf_new
    qf_row = qf_ref[...] * ATTN_SCALE
    qf_col = _lane_bcast_col(qf_row)
    lf_scr[...] = jnp.zeros_like(lf_scr)
    for j in range(n_pages):
        lf_scr[j, 0:N_HEADS, :] = lf_pages[j][...]
    decay = _suffix_sums(lf_scr[...], upper)
    s = jnp.stack([_head_scores(fx_pages[j][0], qf_col) for j in range(n_pages)])
    s = s + decay + _rows_to_col(logf_new)
    s_self = per_head_col(qf_row * kf_ref[...])
    m = jnp.max(jnp.max(s, axis=0), axis=1, keepdims=True)
    m = jnp.maximum(m, s_self)
    p = jnp.exp(s - m)
    p_self = jnp.exp(s_self - m)
    l = jnp.sum(jnp.sum(p, axis=0), axis=1, keepdims=True) + p_self
    p = p / l
    acc = jnp.zeros((W_MIX, page), F32)
    for j in range(n_pages):
        acc = acc + fx_pages[j][1] * _head_weights(p[j])
    fx_o[...] = (_col_to_row(jnp.sum(acc, axis=1, keepdims=True))
                 + per_head_lanes(p_self / l) * vf_ref[...])


def decode_attn(page_table, proj, sb_cache, fox_cache, logf_t, bias_row, layer):
    rows, n_pages = page_table.shape
    page = sb_cache.shape[4]
    proj3 = proj.reshape(rows, 1, proj.shape[1])

    def col(cb):
        return pl.BlockSpec((None, 1, W_MIX), lambda b, pt: (b, 0, cb))

    def kv_page(j):
        return pl.BlockSpec((None, None, 2, W_MIX, page),
                            lambda b, pt, j=j: (pt[b * n_pages + j], layer, 0, 0, 0))

    def lf_page(j):
        return pl.BlockSpec((None, None, N_HEADS, page),
                            lambda b, pt, j=j: (pt[b * n_pages + j], layer, 0, 0))

    out_row = pl.BlockSpec((None, 1, W_MIX), lambda b, pt: (b, 0, 0))
    grid_spec = pltpu.PrefetchScalarGridSpec(
        num_scalar_prefetch=1,
        grid=(rows,),
        in_specs=[col(COL_SBQ), col(COL_FQ), col(COL_FK), col(COL_FV), col(COL_LOGIT),
                  pl.BlockSpec((1, W_MIX), lambda b, pt: (0, 0))]
                 + [kv_page(j) for j in range(n_pages)]
                 + [kv_page(j) for j in range(n_pages)]
                 + [lf_page(j) for j in range(n_pages)],
        out_specs=[out_row, out_row, out_row],
        scratch_shapes=[pltpu.VMEM((n_pages, SUBLANES, page), F32)],
    )
    sb_o, fx_o, lf_o = pl.pallas_call(
        functools.partial(_decode_kernel, n_pages=n_pages),
        grid_spec=grid_spec,
        out_shape=[jax.ShapeDtypeStruct((rows, 1, W_MIX), F32)] * 3,
        compiler_params=_params("arbitrary"),
        name="decode_attn",
    )(page_table.reshape(-1), proj3, proj3, proj3, proj3, proj3, bias_row,
      *([sb_cache] * n_pages), *([fox_cache] * n_pages), *([logf_t] * n_pages))
    return sb_o.reshape(rows, W_MIX), fx_o.reshape(rows, W_MIX), lf_o.reshape(rows, W_MIX)


def _merge_kernel(b0_ref, b1_ref, b2_ref, b3_ref, h_ref, *refs, precise):
    n_w = 6 if precise else 3
    w_refs, (x_ref, gate_ref, o_ref) = refs[:n_w], refs[n_w:]
    wg_ref, wb_ref, wo_ref = w_refs[:3]
    d = x_ref.shape[1]
    h = h_ref[...]
    merged = None
    for i, br in enumerate((b0_ref, b1_ref, b2_ref, b3_ref)):
        cols = slice(i * d, (i + 1) * d)
        if precise:
            gate_logit = _dot3(h, wg_ref[:, cols], w_refs[3][:, cols])
            mixed = _dot3(br[...].astype(F32), wb_ref[i], w_refs[4][i])
        else:
            gate_logit = _dot(h, wg_ref[:, cols])
            mixed = _dot(br[...].astype(BF16), wb_ref[i])
        y = jax.nn.sigmoid(gate_logit) * mixed
        merged = y if merged is None else merged + y
    if precise:
        out = _dot3(merged, wo_ref[...], w_refs[5][...])
    else:
        out = _dot(merged.astype(BF16), wo_ref[...])
    o_ref[...] = x_ref[...] + gate_ref[...] * out


def merge_out(branches, h, weights, layer, x, mod, tm):
    m, d = x.shape
    precise = len(weights) == 6
    br_spec = pl.BlockSpec((tm, W_MIX), lambda i: (i, 0))
    w_specs = [pl.BlockSpec((None, d, N_GATE), lambda i: (layer, 0, 0)),
               pl.BlockSpec((None, N_BRANCH, W_MIX, d), lambda i: (layer, 0, 0, 0)),
               pl.BlockSpec((None, d, d), lambda i: (layer, 0, 0))]
    return pl.pallas_call(
        functools.partial(_merge_kernel, precise=precise),
        grid=(m // tm,),
        in_specs=[br_spec] * N_BRANCH
                 + [pl.BlockSpec((tm, d), lambda i: (i, 0))]
                 + w_specs * (2 if precise else 1)
                 + [pl.BlockSpec((tm, d), lambda i: (i, 0)), mod.spec(2)],
        out_specs=pl.BlockSpec((tm, d), lambda i: (i, 0)),
        out_shape=jax.ShapeDtypeStruct((m, d), F32),
        compiler_params=_params("parallel"),
        name="merge_out",
    )(*branches, h, *weights, x, mod.arr)


def _route(scores, biased):
    lane_i = _iota(scores.shape, 1)
    real = lane_i < N_EXPERTS
    pos = lane_i % EXPERTS_PER_GROUP
    lane = lane_i.astype(F32)
    group = (lane_i // EXPERTS_PER_GROUP).astype(F32)
    neg = -jnp.inf
    n = scores.shape[1]
    mates = [biased]
    for k in range(1, EXPERTS_PER_GROUP):
        fwd = pltpu.roll(biased, n - k, axis=1)
        back = pltpu.roll(biased, EXPERTS_PER_GROUP - k, axis=1)
        mates.append(jnp.where(pos + k < EXPERTS_PER_GROUP, fwd, back))
    group_score = None
    for a in range(EXPERTS_PER_GROUP):
        for b in range(a + 1, EXPERTS_PER_GROUP):
            pair = mates[a] + mates[b]
            group_score = pair if group_score is None else jnp.maximum(group_score, pair)
    group_score = jnp.where(real, group_score, neg)
    best_score = jnp.max(group_score, axis=1, keepdims=True)
    best = jnp.min(jnp.where(group_score == best_score, group, float(n)), axis=1, keepdims=True)
    cand = jnp.where(real & (group == best), biased, neg)
    top0 = jnp.max(cand, axis=1, keepdims=True)
    idx0 = jnp.min(jnp.where(cand == top0, lane, float(n)), axis=1, keepdims=True)
    cand = jnp.where(lane == idx0, neg, cand)
    top1 = jnp.max(cand, axis=1, keepdims=True)
    idx1 = jnp.min(jnp.where(cand == top1, lane, float(n)), axis=1, keepdims=True)
    chosen = (lane == idx0) | (lane == idx1)
    sel = jnp.where(chosen, scores, 0.0)
    return sel / jnp.sum(sel, axis=1, keepdims=True), best


def _moe_kernel(x_ref, g_ref, sh_ref, sc_ref, gate_ref, rw_ref, rb_ref, w1_ref, w3_ref, w2_ref,
                gf_ref, o_ref, h_scr, comb_scr, acc_scr, *, final_norm):
    e = pl.program_id(1)
    precise = w1_ref.dtype == F32

    @pl.when(e == 0)
    def _():
        h = _norm_mod(x_ref[...], g_ref[...], sc_ref[...], sh_ref[...])
        h_scr[...] = h.astype(h_scr.dtype)
        if precise:
            logits = _dot3(h, *_split2(rw_ref[...]))
        else:
            logits = _dot(h_scr[...], rw_ref[...].astype(BF16))
        scores = jax.nn.sigmoid(logits)
        comb_scr[...] = _route(scores, scores + rb_ref[...])[0]
        acc_scr[...] = jnp.zeros_like(acc_scr)

    h = h_scr[...]
    if precise:
        a = _dot3(h, *_split2(w1_ref[...]))
        b = _dot3(h, *_split2(w3_ref[...]))
        y = _dot3(_silu(a) * b, *_split2(w2_ref[...]))
    else:
        a = _dot(h, w1_ref[...])
        b = _dot(h, w3_ref[...])
        y = _dot((_silu(a) * b).astype(BF16), w2_ref[...])
    lane = _iota(comb_scr.shape, 1)
    w_e = jnp.sum(jnp.where(lane == e, comb_scr[...], 0.0), axis=1, keepdims=True)
    acc_scr[...] = acc_scr[...] + w_e * y

    @pl.when(e == N_EXPERTS - 1)
    def _():
        out = x_ref[...] + gate_ref[...] * acc_scr[...]
        if final_norm:
            ms = jnp.mean(out * out, axis=-1, keepdims=True)
            out = out * lax.rsqrt(ms + NORM_EPS) * gf_ref[...]
        o_ref[...] = out


def moe(x, g, mod, router_w_pad, router_b_pad, w1, w3, w2, layer, tm, final_gain=None):
    m, d = x.shape
    f = w1.shape[-1]
    expert = lambda i, e: (layer, e, 0, 0)
    final_norm = final_gain is not None
    gf = (final_gain if final_norm else g).reshape(1, d)
    return pl.pallas_call(
        functools.partial(_moe_kernel, final_norm=final_norm),
        grid=(m // tm, N_EXPERTS),
        in_specs=[pl.BlockSpec((tm, d), lambda i, e: (i, 0)),
                  pl.BlockSpec((1, d), lambda i, e: (0, 0)),
                  mod.spec(3), mod.spec(4), mod.spec(5),
                  pl.BlockSpec((d, LANES), lambda i, e: (0, 0)),
                  pl.BlockSpec((1, LANES), lambda i, e: (0, 0)),
                  pl.BlockSpec((None, None, d, f), expert),
                  pl.BlockSpec((None, None, d, f), expert),
                  pl.BlockSpec((None, None, f, d), expert),
                  pl.BlockSpec((1, d), lambda i, e: (0, 0))],
        out_specs=pl.BlockSpec((tm, d), lambda i, e: (i, 0)),
        out_shape=jax.ShapeDtypeStruct((m, d), F32),
        scratch_shapes=[pltpu.VMEM((tm, d), F32 if w1.dtype == F32 else BF16),
                        pltpu.VMEM((tm, LANES), F32), pltpu.VMEM((tm, d), F32)],
        compiler_params=_params("parallel", "arbitrary"),
        name="moe",
    )(x, g.reshape(1, d), mod.arr, mod.arr, mod.arr, router_w_pad, router_b_pad, w1, w3, w2, gf)


_MOE_SUB = 128
N_GROUPS = N_EXPERTS // EXPERTS_PER_GROUP


def _moe_grouped_kernel(x_ref, g_ref, sh_ref, sc_ref, gate_ref, rw_ref, rb_ref, w1_ref, w3_ref,
                        w2_ref, gf_ref, o_ref, hs_scr, ys_scr, cs_scr, pt_scr, meta_ref, *,
                        final_norm):
    e = pl.program_id(1)
    tm = x_ref.shape[0]
    slots = hs_scr.shape[0]
    sub = _MOE_SUB

    @pl.when(e == 0)
    def _():
        h = _norm_mod(x_ref[...], g_ref[...], sc_ref[...], sh_ref[...])
        hb = h.astype(BF16)
        scores = jax.nn.sigmoid(_dot(hb, rw_ref[...].astype(BF16)))
        comb, best = _route(scores, scores + rb_ref[...])
        lane = _iota((tm, LANES), 1).astype(F32)
        member = jnp.where(lane == best, 1.0, 0.0)
        before = (_iota((tm, tm), 1) < _iota((tm, tm), 0)).astype(BF16)
        rank = _dot(before, member.astype(BF16))
        count = rank[tm - 1:tm, :] + member[tm - 1:tm, :]
        room = jnp.floor((count + (sub - 1)) * (1.0 / sub)) * sub
        lane_row = _iota((1, LANES), 1)
        start = jnp.zeros((1, LANES), F32)
        for k in range(1, N_GROUPS):
            start = start + jnp.where(lane_row >= k, pltpu.roll(room, k, axis=1), 0.0)
        dest = jnp.sum(member * (start + rank), axis=1, keepdims=True)
        place = jnp.where(_iota((tm, slots), 1).astype(F32) == dest, 1.0, 0.0).astype(BF16)
        pt_scr[...] = place
        hs_scr[...] = _dot_tn(place, hb).astype(BF16)
        c_hi, c_lo = _split2(comb)
        cs_scr[...] = _dot_tn(place, c_hi) + _dot_tn(place, c_lo)
        ys_scr[...] = jnp.zeros_like(ys_scr)
        for k in range(N_GROUPS):
            pick = lane_row == k
            meta_ref[k] = jnp.sum(jnp.where(pick, start, 0.0)).astype(jnp.int32)
            meta_ref[N_GROUPS + k] = jnp.sum(jnp.where(pick, room, 0.0)).astype(jnp.int32) // sub

    group = e // EXPERTS_PER_GROUP
    first = meta_ref[group]
    lane = _iota((sub, LANES), 1)

    def body(s, carry):
        r0 = pl.multiple_of(first + s * sub, sub)
        rows = hs_scr[pl.ds(r0, sub), :]
        a = _dot(rows, w1_ref[...])
        b = _dot(rows, w3_ref[...])
        y = _dot((_silu(a) * b).astype(BF16), w2_ref[...])
        w_e = jnp.sum(jnp.where(lane == e, cs_scr[pl.ds(r0, sub), :], 0.0), axis=1, keepdims=True)
        ys_scr[pl.ds(r0, sub), :] = ys_scr[pl.ds(r0, sub), :] + w_e * y
        return carry

    lax.fori_loop(0, meta_ref[N_GROUPS + group], body, 0)

    @pl.when(e == N_EXPERTS - 1)
    def _():
        y_hi, y_lo = _split2(ys_scr[...])
        place = pt_scr[...]
        out = x_ref[...] + gate_ref[...] * (_dot(place, y_hi) + _dot(place, y_lo))
        if final_norm:
            ms = jnp.mean(out * out, axis=-1, keepdims=True)
            out = out * lax.rsqrt(ms + NORM_EPS) * gf_ref[...]
        o_ref[...] = out


def moe_grouped(x, g, mod, router_w_pad, router_b_pad, w1, w3, w2, layer, tm, final_gain=None):
    m, d = x.shape
    f = w1.shape[-1]
    slots = tm + N_GROUPS * _MOE_SUB
    expert = lambda i, e: (layer, e, 0, 0)
    final_norm = final_gain is not None
    gf = (final_gain if final_norm else g).reshape(1, d)
    return pl.pallas_call(
        functools.partial(_moe_grouped_kernel, final_norm=final_norm),
        grid=(m // tm, N_EXPERTS),
        in_specs=[pl.BlockSpec((tm, d), lambda i, e: (i, 0)),
                  pl.BlockSpec((1, d), lambda i, e: (0, 0)),
                  mod.spec(3), mod.spec(4), mod.spec(5),
                  pl.BlockSpec((d, LANES), lambda i, e: (0, 0)),
                  pl.BlockSpec((1, LANES), lambda i, e: (0, 0)),
                  pl.BlockSpec((None, None, d, f), expert),
                  pl.BlockSpec((None, None, d, f), expert),
                  pl.BlockSpec((None, None, f, d), expert),
                  pl.BlockSpec((1, d), lambda i, e: (0, 0))],
        out_specs=pl.BlockSpec((tm, d), lambda i, e: (i, 0)),
        out_shape=jax.ShapeDtypeStruct((m, d), F32),
        scratch_shapes=[pltpu.VMEM((slots, d), BF16), pltpu.VMEM((slots, d), F32),
                        pltpu.VMEM((slots, LANES), F32), pltpu.VMEM((tm, slots), BF16),
                        pltpu.SMEM((2 * N_GROUPS,), jnp.int32)],
        compiler_params=_params("parallel", "arbitrary"),
        name="moe_grouped",
    )(x, g.reshape(1, d), mod.arr, mod.arr, mod.arr, router_w_pad, router_b_pad, w1, w3, w2, gf)


def _rope_tables(pos):
    half = HEAD_DIM // 2
    inv = ROPE_BASE ** (-jnp.arange(half, dtype=F32) / half)
    ang = pos.astype(F32)[:, None] * inv[None, :]
    cos, sin = jnp.cos(ang), jnp.sin(ang)
    cos_row = jnp.tile(jnp.concatenate([cos, cos], axis=1), (1, N_HEADS))
    sin_row = jnp.tile(jnp.concatenate([-sin, sin], axis=1), (1, N_HEADS))
    return cos_row, sin_row


def _retention_tables(pos, chunk):
    lg = jnp.log1p(-(2.0 ** (-5.0 - jnp.arange(N_HEADS, dtype=F32))))
    idx = jnp.arange(chunk, dtype=F32)
    diff = idx[:, None] - idx[None, :]
    inner = jnp.where(diff >= 0, jnp.exp(lg[:, None, None] * jnp.maximum(diff, 0.0)), 0.0)
    q_decay = jnp.exp(lg[None, :] * (idx[:, None] + 1.0))
    k_decay = jnp.exp(lg[None, :] * (chunk - 1.0 - idx[:, None]))
    chunk_decay = jnp.exp(lg * chunk)
    lanes = lambda a: jnp.repeat(a, HEAD_DIM, axis=-1)
    cos, sin = _rope_tables(pos)
    return {"cos": cos, "sin": sin, "inner": inner, "q_decay": lanes(q_decay),
            "k_decay": lanes(k_decay), "chunk_decay": lanes(chunk_decay[None, :]),
            "gamma": chunk_decay}


def _pack_w_kernel(w_ref, wm_ref, wg_ref, wm_lo_ref, wg_lo_ref):
    rows = w_ref.shape[0]

    def put(hi_ref, lo_ref, cols, val):
        hi, lo = _split2(val)
        hi_ref[:, cols] = hi
        lo_ref[:, cols] = lo

    put(wm_ref, wm_lo_ref, slice(0, N_MAIN), w_ref[:, 0:N_MAIN])
    lane = _iota((rows, W_MIX), 1)
    put(wm_ref, wm_lo_ref, slice(N_MAIN, GATE_COL0),
        jnp.where(lane < N_HEADS, w_ref[:, N_MAIN:GATE_COL0], 0.0))
    step = 1024
    for c in range(0, N_GATE, step):
        width = min(step + LANES, N_GATE + N_HEADS - c)
        win = w_ref[:, N_MAIN + c:N_MAIN + c + width]
        put(wg_ref, wg_lo_ref, slice(c, c + step), win[:, N_HEADS:N_HEADS + step])


def pack_w_in(w_in):
    depth, d, n_in = w_in.shape
    assert n_in == N_MAIN + N_HEADS + N_GATE
    tr = 256
    main = pl.BlockSpec((None, tr, GATE_COL0), lambda l, r: (l, r, 0))
    gate = pl.BlockSpec((None, tr, N_GATE), lambda l, r: (l, r, 0))
    main_shape = jax.ShapeDtypeStruct((depth, d, GATE_COL0), BF16)
    gate_shape = jax.ShapeDtypeStruct((depth, d, N_GATE), BF16)
    return pl.pallas_call(
        _pack_w_kernel,
        grid=(depth, d // tr),
        in_specs=[pl.BlockSpec((None, tr, n_in), lambda l, r: (l, r, 0))],
        out_specs=[main, gate, main, gate],
        out_shape=[main_shape, gate_shape, main_shape, gate_shape],
        compiler_params=_params("parallel", "parallel"),
        name="pack_w_in",
    )(w_in)


def _split_w_kernel(w_ref, hi_ref, lo_ref):
    hi, lo = _split2(w_ref[...])
    hi_ref[...] = hi
    lo_ref[...] = lo


def split_weights(w):
    cols = w.shape[-1]
    flat = w.reshape(-1, cols)
    tr = 256
    spec = pl.BlockSpec((tr, cols), lambda i: (i, 0))
    shape = jax.ShapeDtypeStruct(flat.shape, BF16)
    hi, lo = pl.pallas_call(
        _split_w_kernel,
        grid=(flat.shape[0] // tr,),
        in_specs=[spec],
        out_specs=[spec, spec],
        out_shape=[shape, shape],
        compiler_params=_params("parallel"),
        name="split_weights",
    )(flat)
    return hi.reshape(w.shape), lo.reshape(w.shape)


def _block_diag_pool(w_pool_l):
    g, c, _ = w_pool_l.shape
    out = jnp.zeros((g * c, g * c), w_pool_l.dtype)
    for i in range(g):
        out = out.at[i * c:(i + 1) * c, i * c:(i + 1) * c].set(w_pool_l[i])
    return out


def _pad_lanes(row, n):
    return jnp.pad(row, ((0, 0), (0, n - row.shape[1])))


def _diag_blocks(s_bd):
    return jnp.stack([s_bd[:, h * HEAD_DIM:(h + 1) * HEAD_DIM, h * HEAD_DIM:(h + 1) * HEAD_DIM]
                      for h in range(N_HEADS)], axis=1)


def kernel(x_prompt, x_sample, cache_sb_kv, cache_fox_kv, cache_fox_logf, state_pool, state_ret,
           page_table, c_prompt, c_sample, w_ada, b_ada, norm_mix, w_in, w_pool, pool_scale,
           fox_bias, w_branch, w_out, norm_ffn, router_w, router_b, w1, w3, w2, norm_final):
    bp, tp, d = x_prompt.shape
    db, ts, _ = x_sample.shape
    assert ts == 1 and d == D_MODEL
    depth = w_in.shape[0]
    n_phys, _, _, page, _, _ = cache_sb_kv.shape
    n_pages = page_table.shape[1]
    past_len = n_pages * page
    mp = bp * tp

    mod_all = adaln(jnp.concatenate([c_prompt, c_sample], axis=0), w_ada, b_ada)

    sb_cache = jnp.transpose(cache_sb_kv, (0, 1, 2, 4, 5, 3)).reshape(n_phys, depth, 2, W_MIX, page)
    fox_cache = jnp.transpose(cache_fox_kv, (0, 1, 2, 4, 5, 3)).reshape(n_phys, depth, 2, W_MIX, page)
    logf_t = jnp.swapaxes(cache_fox_logf, 2, 3)
    state_t = jnp.transpose(state_ret, (1, 2, 3, 4, 0)).reshape(depth * N_HEADS, HEAD_DIM, HEAD_DIM, db)
    w_main, w_gate, w_main_lo, w_gate_lo = pack_w_in(w_in)
    router_w_pad = _pad_lanes(router_w, LANES)
    router_b_pad = _pad_lanes(router_b[None, :], LANES)
    tabs_p = _retention_tables(jnp.arange(tp), RET_CHUNK if tp % RET_CHUNK == 0 else tp)
    tabs_s = _retention_tables(past_len + jnp.arange(ts), ts)
    wb, wb_lo = split_weights(w_branch)
    wo, wo_lo = split_weights(w_out)
    w1b, w3b, w2b = w1.astype(BF16), w3.astype(BF16), w2.astype(BF16)
    w_merge_p = (w_gate, wb, wo)
    w_merge_s = w_merge_p + (w_gate_lo, wb_lo, wo_lo)

    tm_p = min(1024, tp)
    tm_mrg = min(512, tp)
    xp = x_prompt.reshape(mp, d)
    xs = x_sample.reshape(db, d)
    outs_p = {k: [] for k in ("logf", "pool", "ret")}
    outs_s = {k: [] for k in ("sb", "fox", "logf", "pool", "ret")}
    kv_prev = None

    for l in range(depth):
        w_bd_f32 = _block_diag_pool(w_pool[l])
        w_bd = w_bd_f32.astype(BF16)
        scale_row = pool_scale[l][None, :]
        bias_lanes = _pad_lanes(fox_bias[l][None, :], LANES)
        bias_row = _pad_lanes(fox_bias[l][None, :], W_MIX)

        mod_in = Mod(mod_all[l, :bp], tp, tm_p)
        mod_mrg = Mod(mod_all[l, :bp], tp, tm_mrg)
        final_gain = norm_final if l == depth - 1 else None
        kv_sb, kv_fox, proj, h = in_proj(xp, norm_mix[l], mod_in, w_main, l, tm_p,
                                         kv_seq=(bp, tp), kv_prev=kv_prev)
        kv_prev = (kv_sb, kv_fox)
        pool_o = pool_prompt(proj, w_bd, scale_row, bp, tp)
        sb_o = sb_prompt(proj, bp, tp)
        ret_o, ret_state = ret_prompt(proj, tabs_p, bp, tp)
        logf, cum, cum_t = logf_cum_prompt(proj, bias_lanes, bp, tp)
        fox_o = fox_prompt(proj, cum, cum_t, bp, tp)
        xp = merge_out((pool_o, sb_o, ret_o, fox_o), h, w_merge_p, l, xp, mod_mrg, tm_mrg)
        xp = moe_grouped(xp, norm_ffn[l], mod_in, router_w_pad, router_b_pad, w1b, w3b, w2b, l,
                         tm_p, final_gain)
        p3 = proj.reshape(bp, tp, GATE_COL0)
        outs_p["logf"].append(logf.reshape(bp, tp, LANES)[:, :, :N_HEADS])
        outs_p["pool"].append(p3[:, tp - POOL_BUF:, :W_MIX])
        outs_p["ret"].append(_diag_blocks(ret_state))

        mod_s = Mod(mod_all[l, bp:], 1, db)
        proj_s, h_s = in_proj(xs, norm_mix[l], mod_s, w_main, l, db, w_lo=w_main_lo)
        buf_t = jnp.swapaxes(state_pool[:, l], 0, 1)
        pool_s = pool_sample(buf_t, proj_s, w_bd_f32, scale_row, past_len)
        sb_s, fox_s, logf_s = decode_attn(page_table, proj_s, sb_cache, fox_cache, logf_t, bias_row, l)
        ret_s, state_new = ret_sample(proj_s, state_t, l, tabs_s)
        xs = merge_out((pool_s, sb_s, ret_s, fox_s), h_s, w_merge_s, l, xs, mod_s, db)
        xs = moe(xs, norm_ffn[l], mod_s, router_w_pad, router_b_pad, w1, w3, w2, l, db, final_gain)

        def kv_s(cb, proj_s=proj_s):
            return jnp.stack([proj_s[:, cb * W_MIX:(cb + 1) * W_MIX],
                              proj_s[:, (cb + 1) * W_MIX:(cb + 2) * W_MIX]], axis=1)

        outs_s["sb"].append(kv_s(COL_SBK))
        outs_s["fox"].append(kv_s(COL_FK))
        outs_s["logf"].append(logf_s[:, :N_HEADS])
        outs_s["pool"].append(jnp.concatenate([state_pool[:, l, 1:], proj_s[:, None, :W_MIX]], axis=1))
        outs_s["ret"].append(state_new)

    y_p = xp.reshape(bp, tp, d)
    y_s = xs.reshape(db, ts, d)

    def heads(a, t):
        return a.reshape(a.shape[0], depth, 2, t, N_HEADS, HEAD_DIM)

    def from_transposed(kv):
        return heads(jnp.transpose(kv, (2, 0, 1, 4, 3)), tp)

    return (y_p, y_s,
            from_transposed(kv_prev[0]),
            heads(jnp.stack(outs_s["sb"], axis=1), ts),
            from_transposed(kv_prev[1]),
            heads(jnp.stack(outs_s["fox"], axis=1), ts),
            jnp.stack(outs_p["logf"], axis=1),
            jnp.stack(outs_s["logf"], axis=1).reshape(db, depth, ts, N_HEADS),
            jnp.stack(outs_p["pool"], axis=1),
            jnp.stack(outs_s["pool"], axis=1),
            jnp.stack(outs_p["ret"], axis=1),
            jnp.transpose(jnp.stack(outs_s["ret"], axis=0), (4, 0, 1, 2, 3)))
```

```python
import functools

import numpy as np
import jax
import jax.numpy as jnp
from jax import lax
from jax.experimental import pallas as pl
from jax.experimental.pallas import tpu as pltpu

F32 = jnp.float32
BF16 = jnp.bfloat16

D_MODEL = 1024
HEAD_DIM = 64
W_MIX = 256
N_HEADS = W_MIX // HEAD_DIM
N_BRANCH = 4
POOL_WINDOWS = (2, 4, 8, 16)
POOL_BUF = 15
RET_CHUNK = 128
ROPE_BASE = 10000.0
N_EXPERTS = 16
EXPERTS_PER_GROUP = 4
D_EXPERT = 512
N_MOD = 6
NORM_EPS = 1e-6
ATTN_SCALE = HEAD_DIM ** -0.5
LANES = 128
SUBLANES = 8
VMEM_LIMIT = 56 * 1024 * 1024

COL_U, COL_SBQ, COL_SBK, COL_SBV = 0, 1, 2, 3
COL_RQ, COL_RK, COL_RV, COL_RG = 4, 5, 6, 7
COL_FQ, COL_FK, COL_FV, COL_LOGIT = 8, 9, 10, 11
N_MAIN = 11 * W_MIX
GATE_COL0 = 12 * W_MIX
N_GATE = N_BRANCH * D_MODEL


def _params(*sem):
    return pltpu.CompilerParams(dimension_semantics=sem, vmem_limit_bytes=VMEM_LIMIT)


def _split2(x):
    hi = x.astype(BF16)
    lo = (x - hi.astype(F32)).astype(BF16)
    return hi, lo


def _split3(x):
    hi = x.astype(BF16)
    r = x - hi.astype(F32)
    mid = r.astype(BF16)
    lo = (r - mid.astype(F32)).astype(BF16)
    return hi, mid, lo


def _dot(a, b):
    return jnp.dot(a, b, preferred_element_type=F32)


def _dot_nt(a, b):
    return lax.dot_general(a, b, (((1,), (1,)), ((), ())), preferred_element_type=F32)


def _dot_tn(a, b):
    return lax.dot_general(a, b, (((0,), (0,)), ((), ())), preferred_element_type=F32)


def _dot_split(x, w, parts=2):
    ps = _split2(x) if parts == 2 else _split3(x)
    acc = _dot(ps[0], w)
    for p in ps[1:]:
        acc = acc + _dot(p, w)
    return acc


def _dot3(x, w_hi, w_lo):
    x_hi, x_lo = _split2(x)
    return _dot(x_hi, w_hi) + (_dot(x_lo, w_hi) + _dot(x_hi, w_lo))


def _iota(shape, dim):
    return lax.broadcasted_iota(jnp.int32, shape, dim)


def _softplus(z):
    return jnp.maximum(z, 0.0) + jnp.log1p(jnp.exp(-jnp.abs(z)))


def _silu(x):
    return x * jax.nn.sigmoid(x)


def _adaln_kernel(c_ref, w_ref, b_ref, o_ref):
    w_hi, w_lo = _split2(w_ref[...])
    o_ref[...] = _dot3(_silu(c_ref[...]), w_hi, w_lo) + b_ref[...]


def adaln(c_all, w_ada, b_ada):
    rows, d = c_all.shape
    depth, _, n = w_ada.shape
    tn = 1024
    return pl.pallas_call(
        _adaln_kernel,
        grid=(depth, n // tn),
        in_specs=[pl.BlockSpec((rows, d), lambda l, j: (0, 0)),
                  pl.BlockSpec((None, d, tn), lambda l, j: (l, 0, j)),
                  pl.BlockSpec((None, 1, tn), lambda l, j: (l, 0, j))],
        out_specs=pl.BlockSpec((None, rows, tn), lambda l, j: (l, 0, j)),
        out_shape=jax.ShapeDtypeStruct((depth, rows, n), F32),
        compiler_params=_params("parallel", "parallel"),
        name="adaln",
    )(c_all, w_ada, b_ada.reshape(depth, 1, n))


class Mod:
    def __init__(self, arr, rows_per_vec, tm):
        self.per_row = rows_per_vec == 1
        self.arr = arr if self.per_row else arr.reshape(arr.shape[0], 1, arr.shape[-1])
        self.tiles_per_vec = 1 if self.per_row else rows_per_vec // tm
        self.tm = tm

    def spec(self, k):
        if self.per_row:
            return pl.BlockSpec((self.tm, D_MODEL), lambda i, *_: (i, k))
        t = self.tiles_per_vec
        return pl.BlockSpec((None, 1, D_MODEL), lambda i, *_: (i // t, 0, k))


def _norm_mod(x, g, sc, sh):
    ms = jnp.mean(x * x, axis=-1, keepdims=True)
    y = x * lax.rsqrt(ms + NORM_EPS) * g
    return y * (1.0 + sc) + sh


_PROJ_TN = 1024


def _inproj_kernel(x_ref, g_ref, sh_ref, sc_ref, w_ref, *refs, kv_out, precise):
    j = pl.program_id(1)
    o_ref, h_ref = refs[-2], refs[-1]

    @pl.when(j == 0)
    def _():
        h_ref[...] = _norm_mod(x_ref[...], g_ref[...], sc_ref[...], sh_ref[...]).astype(h_ref.dtype)

    if precise:
        res = _dot3(h_ref[...], w_ref[...], refs[0][...])
    else:
        res = _dot(h_ref[...], w_ref[...])
    o_ref[...] = res
    if kv_out:
        per_tile = _PROJ_TN // W_MIX
        for col_k, kv_ref in ((COL_SBK, refs[-4]), (COL_FK, refs[-3])):
            c0 = (col_k % per_tile) * W_MIX

            @pl.when(j == col_k // per_tile)
            def _(c0=c0, kv_ref=kv_ref):
                kv_ref[0] = jnp.transpose(res[:, c0:c0 + W_MIX])
                kv_ref[1] = jnp.transpose(res[:, c0 + W_MIX:c0 + 2 * W_MIX])


def in_proj(x, g, mod, w_all, layer, tm, kv_seq=None, kv_prev=None, w_lo=None):
    m, d = x.shape
    depth, _, n = w_all.shape
    tn = _PROJ_TN
    precise = w_lo is not None
    w_spec = pl.BlockSpec((None, d, tn), lambda i, j: (layer, 0, j))
    in_specs = [pl.BlockSpec((tm, d), lambda i, j: (i, 0)),
                pl.BlockSpec((1, d), lambda i, j: (0, 0)),
                mod.spec(0), mod.spec(1), w_spec]
    args = [x, g.reshape(1, d), mod.arr, mod.arr, w_all]
    if precise:
        in_specs.append(w_spec)
        args.append(w_lo)
    out_specs = [pl.BlockSpec((tm, tn), lambda i, j: (i, j)),
                 pl.BlockSpec((tm, d), lambda i, j: (i, 0))]
    out_shape = [jax.ShapeDtypeStruct((m, n), F32),
                 jax.ShapeDtypeStruct((m, d), F32 if precise else BF16)]
    aliases = {}
    if kv_seq is not None:
        b, t = kv_seq
        per_b = t // tm
        kv_spec = pl.BlockSpec((None, 2, None, W_MIX, tm),
                               lambda i, j: (layer, 0, i // per_b, 0, i % per_b))
        kv_shape = jax.ShapeDtypeStruct((depth, 2, b, W_MIX, t), F32)
        out_specs = [kv_spec, kv_spec] + out_specs
        out_shape = [kv_shape, kv_shape] + out_shape
        if kv_prev is not None:
            aliases = {len(args): 0, len(args) + 1: 1}
            in_specs += [pl.BlockSpec(memory_space=pl.ANY)] * 2
            args += list(kv_prev)
    outs = pl.pallas_call(
        functools.partial(_inproj_kernel, kv_out=kv_seq is not None, precise=precise),
        grid=(m // tm, n // tn),
        in_specs=in_specs,
        out_specs=out_specs,
        out_shape=out_shape,
        input_output_aliases=aliases,
        compiler_params=_params("parallel", "arbitrary"),
        name="in_proj",
    )(*args)
    return outs


def _pool_tail(s2, s4, s8, s16, u, pos0, w_ref, sc_ref, o_ref):
    t = u.shape[0]
    lane = _iota((1, W_MIX), 1) // (W_MIX // len(POOL_WINDOWS))
    win = jnp.where(lane == 0, s2, jnp.where(lane == 1, s4, jnp.where(lane == 2, s8, s16)))
    width = jnp.where(lane == 0, 2, jnp.where(lane == 1, 4, jnp.where(lane == 2, 8, 16)))
    count = jnp.minimum(width, pos0 + 1 + _iota((t, W_MIX), 0)).astype(F32)
    resid = win / count - u
    if w_ref.dtype == F32:
        w_hi, w_lo = _split2(w_ref[...])
        mixed = _dot3(resid, w_hi, w_lo)
    else:
        mixed = _dot(resid.astype(BF16), w_ref[...])
    o_ref[...] = (mixed * sc_ref[...]).astype(o_ref.dtype)


_POOL_PAD = 32


def _pool_prompt_kernel(u_ref, w_ref, sc_ref, o_ref, a_scr, b_scr, c_scr):
    t = u_ref.shape[0]
    p = _POOL_PAD
    u = u_ref[...]
    a_scr[0:p, :] = jnp.zeros((p, W_MIX), F32)
    a_scr[p:p + t, :] = u

    def stage(src, dst, k, lo):
        n = p + t - lo
        dst[lo:lo + n, :] = src[lo:lo + n, :] + src[lo - k:lo - k + n, :]

    stage(a_scr, b_scr, 1, 8)
    stage(b_scr, c_scr, 2, 16)
    stage(c_scr, a_scr, 4, 24)
    s8 = a_scr[p:p + t, :]
    s16 = s8 + a_scr[p - 8:p - 8 + t, :]
    _pool_tail(b_scr[p:p + t, :], c_scr[p:p + t, :], s8, s16, u, 0, w_ref, sc_ref, o_ref)


def pool_prompt(proj, w_bd, scale, b, t):
    return pl.pallas_call(
        _pool_prompt_kernel,
        grid=(b,),
        in_specs=[pl.BlockSpec((t, W_MIX), lambda i: (i, COL_U)),
                  pl.BlockSpec((W_MIX, W_MIX), lambda i: (0, 0)),
                  pl.BlockSpec((1, W_MIX), lambda i: (0, 0))],
        out_specs=pl.BlockSpec((t, W_MIX), lambda i: (i, 0)),
        out_shape=jax.ShapeDtypeStruct((b * t, W_MIX), BF16),
        scratch_shapes=[pltpu.VMEM((t + _POOL_PAD, W_MIX), F32)] * 3,
        compiler_params=_params("parallel"),
        name="pool_prompt",
    )(proj, w_bd, scale)


def _pool_sample_kernel(buf_ref, u_ref, w_ref, sc_ref, o_ref, *, pos0):
    u = u_ref[...]
    s2 = u + buf_ref[14]
    s4 = s2 + buf_ref[13] + buf_ref[12]
    s8 = s4 + buf_ref[11] + buf_ref[10] + buf_ref[9] + buf_ref[8]
    s16 = s8
    for r in range(7, -1, -1):
        s16 = s16 + buf_ref[r]
    _pool_tail(s2, s4, s8, s16, u, pos0, w_ref, sc_ref, o_ref)


def pool_sample(buf_t, proj, w_bd, scale, pos0):
    rows = proj.shape[0]
    return pl.pallas_call(
        functools.partial(_pool_sample_kernel, pos0=pos0),
        grid=(1,),
        in_specs=[pl.BlockSpec((POOL_BUF, rows, W_MIX), lambda i: (0, 0, 0)),
                  pl.BlockSpec((rows, W_MIX), lambda i: (0, COL_U)),
                  pl.BlockSpec((W_MIX, W_MIX), lambda i: (0, 0)),
                  pl.BlockSpec((1, W_MIX), lambda i: (0, 0))],
        out_specs=pl.BlockSpec((rows, W_MIX), lambda i: (0, 0)),
        out_shape=jax.ShapeDtypeStruct((rows, W_MIX), F32),
        compiler_params=_params("arbitrary"),
        name="pool_sample",
    )(buf_t, proj, w_bd, scale)


_ATT_TILE = 256
_ATT_CHAINS = 1


def _head_of_lane():
    return _iota((1, W_MIX), 1) // HEAD_DIM


def _stack_heads(q):
    head = _head_of_lane()
    return jnp.concatenate([jnp.where(head == h, q, 0.0) for h in range(N_HEADS)], axis=0).astype(BF16)


def _unstack_heads(acc, tq):
    head = _head_of_lane()
    out = jnp.where(head == 0, acc[0:tq], 0.0)
    for h in range(1, N_HEADS):
        out = out + jnp.where(head == h, acc[h * tq:(h + 1) * tq], 0.0)
    return out


def _load_kv_once(k_ref, v_ref, k_scr, v_scr):
    @pl.when(pl.program_id(1) == 0)
    def _():
        k_scr[...] = k_ref[...].astype(BF16)
        v_scr[...] = v_ref[...].astype(BF16)


def _sb_prompt_kernel(q_ref, k_ref, v_ref, o_ref, k_scr, v_scr):
    tq = tk = _ATT_TILE
    i = pl.program_id(1)
    _load_kv_once(k_ref, v_ref, k_scr, v_scr)
    q4 = _stack_heads(q_ref[...] * ATTN_SCALE)
    rows = N_HEADS * tq // _ATT_CHAINS
    q_parts = [q4[c * rows:(c + 1) * rows] for c in range(_ATT_CHAINS)]
    below_diag = _iota((rows, tk), 1) < (_iota((rows, tk), 0) % tq)
    upper = (_iota((tk, tk), 0) > _iota((tk, tk), 1)).astype(BF16)

    def block(off, diagonal, carry):
        kb = k_scr[pl.ds(off, tk), :]
        vb = v_scr[pl.ds(off, tk), :]
        out = []
        for qp, (acc, run) in zip(q_parts, carry):
            z = _dot_nt(qp, kb)
            sp = jnp.maximum(z, 0.0) + jnp.log(1.0 + jnp.exp(-jnp.abs(z)))
            log_not = jnp.where(below_diag, -sp, 0.0) if diagonal else -sp
            suffix = _dot_split(log_not, upper)
            a = jnp.exp(z - sp + suffix + run)
            if diagonal:
                a = jnp.where(below_diag, a, 0.0)
            out.append((acc + _dot(a.astype(BF16), vb), run + suffix[:, 0:1] + log_not[:, 0:1]))
        return tuple(out)

    zero = (jnp.zeros((rows, W_MIX), F32), jnp.zeros((rows, 1), F32))
    carry = block(pl.multiple_of(i * tk, tk), True, (zero,) * _ATT_CHAINS)
    carry = lax.fori_loop(
        0, i, lambda s, c: block(pl.multiple_of((i - 1 - s) * tk, tk), False, c), carry)
    acc = jnp.concatenate([c[0] for c in carry], axis=0)
    o_ref[...] = _unstack_heads(acc, tq).astype(o_ref.dtype)


def _attn_prompt_specs(t, col_q, col_k, col_v):
    tq = _ATT_TILE
    nq = t // tq
    return [pl.BlockSpec((tq, W_MIX), lambda b, i: (b * nq + i, col_q)),
            pl.BlockSpec((t, W_MIX), lambda b, i: (b, col_k)),
            pl.BlockSpec((t, W_MIX), lambda b, i: (b, col_v))]


def sb_prompt(proj, b, t):
    tq = _ATT_TILE
    nq = t // tq
    return pl.pallas_call(
        _sb_prompt_kernel,
        grid=(b, nq),
        in_specs=_attn_prompt_specs(t, COL_SBQ, COL_SBK, COL_SBV),
        out_specs=pl.BlockSpec((tq, W_MIX), lambda bb, i: (bb * nq + i, 0)),
        out_shape=jax.ShapeDtypeStruct((b * t, W_MIX), BF16),
        scratch_shapes=[pltpu.VMEM((t, W_MIX), BF16)] * 2,
        compiler_params=_params("parallel", "arbitrary"),
        name="sb_prompt",
    )(proj, proj, proj)


def _logf_cum_kernel(x_ref, bias_ref, logf_ref, cum_ref, cumt_ref):
    t = x_ref.shape[0]
    blk = 256
    logf = jax.nn.log_sigmoid(x_ref[...] + bias_ref[...])
    logf_ref[...] = logf
    lower = (_iota((blk, blk), 1) <= _iota((blk, blk), 0)).astype(BF16)
    carry = jnp.zeros((1, LANES), F32)
    for c in range(t // blk):
        part = logf[c * blk:(c + 1) * blk]
        hi, mid, lo = _split3(part)
        cum = _dot(lower, hi) + _dot(lower, mid) + _dot(lower, lo) + carry
        cum_ref[c * blk:(c + 1) * blk, :] = cum
        carry = cum[blk - 1:blk, :]
    cumt_ref[...] = jnp.transpose(cum_ref[...])[0:SUBLANES, :]


def logf_cum_prompt(proj, bias_pad, b, t):
    m = b * t
    return pl.pallas_call(
        _logf_cum_kernel,
        grid=(b,),
        in_specs=[pl.BlockSpec((t, LANES), lambda i: (i, COL_LOGIT * W_MIX // LANES)),
                  pl.BlockSpec((1, LANES), lambda i: (0, 0))],
        out_specs=[pl.BlockSpec((t, LANES), lambda i: (i, 0)),
                   pl.BlockSpec((t, LANES), lambda i: (i, 0)),
                   pl.BlockSpec((None, SUBLANES, t), lambda i: (i, 0, 0))],
        out_shape=[jax.ShapeDtypeStruct((m, LANES), F32),
                   jax.ShapeDtypeStruct((m, LANES), F32),
                   jax.ShapeDtypeStruct((b, SUBLANES, t), F32)],
        compiler_params=_params("parallel"),
        name="logf_cum",
    )(proj, bias_pad)


def _fox_prompt_kernel(q_ref, k_ref, v_ref, cq_ref, ck_ref, o_ref, k_scr, v_scr):
    tq = tk = _ATT_TILE
    i = pl.program_id(1)
    _load_kv_once(k_ref, v_ref, k_scr, v_scr)
    q4 = _stack_heads(q_ref[...] * ATTN_SCALE)
    rows = N_HEADS * tq
    on_or_below_diag = _iota((rows, tk), 1) <= (_iota((rows, tk), 0) % tq)
    cq = cq_ref[...]

    def block(off, diagonal, carry):
        acc, m, l = carry
        kb = k_scr[pl.ds(off, tk), :]
        vb = v_scr[pl.ds(off, tk), :]
        z = _dot_nt(q4, kb)
        sc = jnp.concatenate(
            [z[h * tq:(h + 1) * tq] + (cq[:, h:h + 1] - ck_ref[h:h + 1, pl.ds(off, tk)])
             for h in range(N_HEADS)], axis=0)
        if diagonal:
            sc = jnp.where(on_or_below_diag, sc, -jnp.inf)
        m_new = jnp.maximum(m, jnp.max(sc, axis=1, keepdims=True))
        alpha = jnp.exp(m - m_new)
        p = jnp.exp(sc - m_new)
        return (alpha * acc + _dot(p.astype(BF16), vb), m_new,
                alpha * l + jnp.sum(p, axis=1, keepdims=True))

    carry = block(pl.multiple_of(i * tk, tk), True,
                  (jnp.zeros((rows, W_MIX), F32), jnp.full((rows, 1), -jnp.inf, F32),
                   jnp.zeros((rows, 1), F32)))
    acc, _, l = lax.fori_loop(
        0, i, lambda s, c: block(pl.multiple_of((i - 1 - s) * tk, tk), False, c), carry)
    o_ref[...] = _unstack_heads(acc / l, tq).astype(o_ref.dtype)


def fox_prompt(proj, cum, cum_t, b, t):
    tq = _ATT_TILE
    nq = t // tq
    specs = _attn_prompt_specs(t, COL_FQ, COL_FK, COL_FV)
    specs += [pl.BlockSpec((tq, LANES), lambda bb, i: (bb * nq + i, 0)),
              pl.BlockSpec((None, SUBLANES, t), lambda bb, i: (bb, 0, 0))]
    return pl.pallas_call(
        _fox_prompt_kernel,
        grid=(b, nq),
        in_specs=specs,
        out_specs=pl.BlockSpec((tq, W_MIX), lambda bb, i: (bb * nq + i, 0)),
        out_shape=jax.ShapeDtypeStruct((b * t, W_MIX), BF16),
        scratch_shapes=[pltpu.VMEM((t, W_MIX), BF16)] * 2,
        compiler_params=_params("parallel", "arbitrary"),
        name="fox_prompt",
    )(proj, proj, proj, cum, cum_t)


def _rotary(x, cos, sin_signed):
    half = HEAD_DIM // 2
    first = (_iota((1, W_MIX), 1) % HEAD_DIM) < half
    swapped = jnp.where(first, pltpu.roll(x, W_MIX - half, axis=1), pltpu.roll(x, half, axis=1))
    return x * cos + swapped * sin_signed


def _head_mean_matrix():
    same = (_iota((W_MIX, W_MIX), 0) // HEAD_DIM) == (_iota((W_MIX, W_MIX), 1) // HEAD_DIM)
    return jnp.where(same, 1.0 / HEAD_DIM, 0.0).astype(BF16)


def _head_norm_gate(o, gate_logit):
    avg = _head_mean_matrix()
    mu = _dot_split(o, avg)
    dev = o - mu
    var = _dot_split(dev * dev, avg)
    return _silu(gate_logit) * (dev * lax.rsqrt(var + NORM_EPS))


def _ret_prompt_kernel(q_ref, k_ref, v_ref, g_ref, cos_ref, sin_ref, inner_ref, qd_ref, kd_ref,
                       cd_ref, o_ref, s_ref):
    c = pl.program_id(1)

    @pl.when(c == 0)
    def _():
        s_ref[...] = jnp.zeros_like(s_ref)

    cos = cos_ref[...]
    sin = sin_ref[...]
    ch = cos.shape[0]
    same = (_iota((W_MIX, W_MIX), 0) // HEAD_DIM) == (_iota((W_MIX, W_MIX), 1) // HEAD_DIM)
    inner = inner_ref[...].reshape(N_HEADS * ch, ch)
    for s in range(q_ref.shape[0]):
        q = _rotary(q_ref[s], cos, sin)
        k = _rotary(k_ref[s], cos, sin) * ATTN_SCALE
        kb = k.astype(BF16)
        vb = v_ref[s].astype(BF16)
        state = s_ref[s]
        att = _dot_nt(_stack_heads(q), kb) * inner
        o = _dot(q.astype(BF16), state.astype(BF16)) * qd_ref[...]
        o = o + _unstack_heads(_dot(att.astype(BF16), vb), ch)
        kv = _dot_tn((k * kd_ref[...]).astype(BF16), vb)
        s_ref[s] = state * cd_ref[...] + jnp.where(same, kv, 0.0)
        o_ref[s] = _head_norm_gate(o, g_ref[s]).astype(o_ref.dtype)


def ret_prompt(proj, tabs, b, t):
    ch = RET_CHUNK
    nc = t // ch
    nb = max(n for n in (8, 4, 2, 1) if b % n == 0)
    proj3 = proj.reshape(b, t, proj.shape[1])

    def col(cb):
        return pl.BlockSpec((nb, ch, W_MIX), lambda bb, c: (bb, c, cb))

    full = lambda shape: pl.BlockSpec(shape, lambda bb, c: (0,) * len(shape))
    out, state = pl.pallas_call(
        _ret_prompt_kernel,
        grid=(b // nb, nc),
        in_specs=[col(COL_RQ), col(COL_RK), col(COL_RV), col(COL_RG),
                  pl.BlockSpec((ch, W_MIX), lambda bb, c: (c, 0)),
                  pl.BlockSpec((ch, W_MIX), lambda bb, c: (c, 0)),
                  full((N_HEADS, ch, ch)), full((ch, W_MIX)), full((ch, W_MIX)), full((1, W_MIX))],
        out_specs=[pl.BlockSpec((nb, ch, W_MIX), lambda bb, c: (bb, c, 0)),
                   pl.BlockSpec((nb, W_MIX, W_MIX), lambda bb, c: (bb, 0, 0))],
        out_shape=[jax.ShapeDtypeStruct((b, t, W_MIX), BF16),
                   jax.ShapeDtypeStruct((b, W_MIX, W_MIX), F32)],
        compiler_params=_params("parallel", "arbitrary"),
        name="ret_prompt",
    )(proj3, proj3, proj3, proj3, tabs["cos"], tabs["sin"], tabs["inner"], tabs["q_decay"],
      tabs["k_decay"], tabs["chunk_decay"])
    return out.reshape(b * t, W_MIX), state


def _ret_sample_kernel(gam_ref, q_ref, k_ref, v_ref, g_ref, cos_ref, sin_ref, s_ref,
                       o_ref, sn_ref, qt_scr, kt_scr, vt_scr, ot_scr):
    h = pl.program_id(0)
    rows = q_ref.shape[0]

    @pl.when(h == 0)
    def _():
        cos = cos_ref[...]
        sin = sin_ref[...]
        qt_scr[...] = jnp.transpose(_rotary(q_ref[...], cos, sin))
        kt_scr[...] = jnp.transpose(_rotary(k_ref[...], cos, sin) * ATTN_SCALE)
        vt_scr[...] = jnp.transpose(v_ref[...])

    row0 = pl.multiple_of(h * HEAD_DIM, HEAD_DIM)
    gamma = gam_ref[h]
    vh = vt_scr[pl.ds(row0, HEAD_DIM), :]

    def body(d, cross):
        qd = qt_scr[pl.ds(row0 + d, 1), :]
        kd = kt_scr[pl.ds(row0 + d, 1), :]
        s_d = s_ref[d]
        sn_ref[d] = gamma * s_d + kd * vh
        return cross + qd * s_d

    cross = lax.fori_loop(0, HEAD_DIM, body, jnp.zeros((HEAD_DIM, rows), F32))
    qk = jnp.sum(qt_scr[pl.ds(row0, HEAD_DIM), :] * kt_scr[pl.ds(row0, HEAD_DIM), :],
                 axis=0, keepdims=True)
    ot_scr[pl.ds(row0, HEAD_DIM), :] = qk * vh + cross * gamma

    @pl.when(h == N_HEADS - 1)
    def _():
        o_ref[...] = _head_norm_gate(jnp.transpose(ot_scr[...]), g_ref[...])


def ret_sample(proj, state_t, layer, tabs):
    rows = proj.shape[0]

    def col(cb):
        return pl.BlockSpec((rows, W_MIX), lambda h: (0, cb))

    row_tab = pl.BlockSpec((1, W_MIX), lambda h: (0, 0))
    state_blk = (None, HEAD_DIM, HEAD_DIM, rows)
    return pl.pallas_call(
        _ret_sample_kernel,
        grid=(N_HEADS,),
        in_specs=[pl.BlockSpec(memory_space=pltpu.SMEM),
                  col(COL_RQ), col(COL_RK), col(COL_RV), col(COL_RG), row_tab, row_tab,
                  pl.BlockSpec(state_blk, lambda h: (layer * N_HEADS + h, 0, 0, 0))],
        out_specs=[pl.BlockSpec((rows, W_MIX), lambda h: (0, 0)),
                   pl.BlockSpec(state_blk, lambda h: (h, 0, 0, 0))],
        out_shape=[jax.ShapeDtypeStruct((rows, W_MIX), F32),
                   jax.ShapeDtypeStruct((N_HEADS, HEAD_DIM, HEAD_DIM, rows), F32)],
        scratch_shapes=[pltpu.VMEM((W_MIX, rows), F32)] * 4,
        compiler_params=_params("arbitrary"),
        name="ret_sample",
    )(tabs["gamma"], proj, proj, proj, proj, tabs["cos"], tabs["sin"], state_t)


def _suffix_sums(x3, upper):
    n_pages, rows, page = x3.shape
    flat = x3.reshape(n_pages * rows, page)
    within = _dot_split(flat, upper, parts=3)
    total = (within[:, 0:1] + flat[:, 0:1]).reshape(n_pages, rows, 1)
    within = within.reshape(n_pages, rows, page)
    carry = jnp.zeros((rows, 1), F32)
    outs = [None] * n_pages
    for j in range(n_pages - 1, -1, -1):
        outs[j] = within[j] + carry
        carry = carry + total[j]
    return jnp.stack(outs)


def _rows_to_col(row):
    pick = _iota((SUBLANES, W_MIX), 0) == _iota((SUBLANES, W_MIX), 1)
    return jnp.sum(jnp.where(pick, row, 0.0), axis=1, keepdims=True)


def _lane_bcast_col(row):
    return jnp.transpose(jnp.broadcast_to(row, (LANES, W_MIX)))


def _col_to_row(col):
    return jnp.transpose(jnp.broadcast_to(col, (W_MIX, LANES)))[0:1, :]


def _head_scores(kt, qcol):
    page = kt.shape[1]
    prod = kt * qcol
    row = _iota((SUBLANES, page), 0)
    out = jnp.zeros((SUBLANES, page), F32)
    for h in range(N_HEADS):
        part = prod[h * HEAD_DIM:(h + 1) * HEAD_DIM].reshape(HEAD_DIM // SUBLANES, SUBLANES, page)
        tot = jnp.sum(jnp.sum(part, axis=0), axis=0, keepdims=True)
        out = jnp.where(row == h, tot, out)
    return out


def _head_weights(a8):
    page = a8.shape[1]
    return jnp.concatenate([jnp.broadcast_to(a8[h:h + 1, :], (HEAD_DIM, page))
                            for h in range(N_HEADS)], axis=0)


def _decode_kernel(pt_ref, qs_ref, qf_ref, kf_ref, vf_ref, lg_ref, bias_ref, *refs, n_pages):
    sb_pages = refs[0:n_pages]
    fx_pages = refs[n_pages:2 * n_pages]
    lf_pages = refs[2 * n_pages:3 * n_pages]
    sb_o, fx_o, lf_o, lf_scr = refs[3 * n_pages:]
    page = sb_pages[0].shape[2]
    head_rows = _iota((SUBLANES, W_MIX), 0) == (_iota((SUBLANES, W_MIX), 1) // HEAD_DIM)
    upper = (_iota((page, page), 0) > _iota((page, page), 1)).astype(BF16)

    def per_head_col(row):
        return jnp.sum(jnp.where(head_rows, row, 0.0), axis=1, keepdims=True)

    def per_head_lanes(col):
        return jnp.sum(jnp.where(head_rows, col, 0.0), axis=0, keepdims=True)

    qs_col = _lane_bcast_col(qs_ref[...] * ATTN_SCALE)
    z = jnp.stack([_head_scores(sb_pages[j][0], qs_col) for j in range(n_pages)])
    sp = _softplus(z)
    after = _suffix_sums(-sp, upper)
    a = jnp.exp(z - sp + after)
    acc = jnp.zeros((W_MIX, page), F32)
    for j in range(n_pages):
        acc = acc + sb_pages[j][1] * _head_weights(a[j])
    sb_o[...] = _col_to_row(jnp.sum(acc, axis=1, keepdims=True))

    logf_new = jax.nn.log_sigmoid(lg_ref[...] + bias_ref[...])
    lf_o[...] = logf_new
    qf_row = qf_ref[...] * ATTN_SCALE
    qf_col = _lane_bcast_col(qf_row)
    lf_scr[...] = jnp.zeros_like(lf_scr)
    for j in range(n_pages):
        lf_scr[j, 0:N_HEADS, :] = lf_pages[j][...]
    decay = _suffix_sums(lf_scr[...], upper)
    s = jnp.stack([_head_scores(fx_pages[j][0], qf_col) for j in range(n_pages)])
    s = s + decay + _rows_to_col(logf_new)
    s_self = per_head_col(qf_row * kf_ref[...])
    m = jnp.max(jnp.max(s, axis=0), axis=1, keepdims=True)
    m = jnp.maximum(m, s_self)
    p = jnp.exp(s - m)
    p_self = jnp.exp(s_self - m)
    l = jnp.sum(jnp.sum(p, axis=0), axis=1, keepdims=True) + p_self
    p = p / l
    acc = jnp.zeros((W_MIX, page), F32)
    for j in range(n_pages):
        acc = acc + fx_pages[j][1] * _head_weights(p[j])
    fx_o[...] = (_col_to_row(jnp.sum(acc, axis=1, keepdims=True))
                 + per_head_lanes(p_self / l) * vf_ref[...])


def decode_attn(page_table, proj, sb_cache, fox_cache, logf_t, bias_row, layer):
    rows, n_pages = page_table.shape
    page = sb_cache.shape[4]
    proj3 = proj.reshape(rows, 1, proj.shape[1])

    def col(cb):
        return pl.BlockSpec((None, 1, W_MIX), lambda b, pt: (b, 0, cb))

    def kv_page(j):
        return pl.BlockSpec((None, None, 2, W_MIX, page),
                            lambda b, pt, j=j: (pt[b * n_pages + j], layer, 0, 0, 0))

    def lf_page(j):
        return pl.BlockSpec((None, None, N_HEADS, page),
                            lambda b, pt, j=j: (pt[b * n_pages + j], layer, 0, 0))

    out_row = pl.BlockSpec((None, 1, W_MIX), lambda b, pt: (b, 0, 0))
    grid_spec = pltpu.PrefetchScalarGridSpec(
        num_scalar_prefetch=1,
        grid=(rows,),
        in_specs=[col(COL_SBQ), col(COL_FQ), col(COL_FK), col(COL_FV), col(COL_LOGIT),
                  pl.BlockSpec((1, W_MIX), lambda b, pt: (0, 0))]
                 + [kv_page(j) for j in range(n_pages)]
                 + [kv_page(j) for j in range(n_pages)]
                 + [lf_page(j) for j in range(n_pages)],
        out_specs=[out_row, out_row, out_row],
        scratch_shapes=[pltpu.VMEM((n_pages, SUBLANES, page), F32)],
    )
    sb_o, fx_o, lf_o = pl.pallas_call(
        functools.partial(_decode_kernel, n_pages=n_pages),
        grid_spec=grid_spec,
        out_shape=[jax.ShapeDtypeStruct((rows, 1, W_MIX), F32)] * 3,
        compiler_params=_params("arbitrary"),
        name="decode_attn",
    )(page_table.reshape(-1), proj3, proj3, proj3, proj3, proj3, bias_row,
      *([sb_cache] * n_pages), *([fox_cache] * n_pages), *([logf_t] * n_pages))
    return sb_o.reshape(rows, W_MIX), fx_o.reshape(rows, W_MIX), lf_o.reshape(rows, W_MIX)


def _merge_kernel(b0_ref, b1_ref, b2_ref, b3_ref, h_ref, *refs, precise):
    n_w = 6 if precise else 3
    w_refs, (x_ref, gate_ref, o_ref) = refs[:n_w], refs[n_w:]
    wg_ref, wb_ref, wo_ref = w_refs[:3]
    d = x_ref.shape[1]
    h = h_ref[...]
    merged = None
    for i, br in enumerate((b0_ref, b1_ref, b2_ref, b3_ref)):
        cols = slice(i * d, (i + 1) * d)
        if precise:
            gate_logit = _dot3(h, wg_ref[:, cols], w_refs[3][:, cols])
            mixed = _dot3(br[...].astype(F32), wb_ref[i], w_refs[4][i])
        else:
            gate_logit = _dot(h, wg_ref[:, cols])
            mixed = _dot(br[...].astype(BF16), wb_ref[i])
        y = jax.nn.sigmoid(gate_logit) * mixed
        merged = y if merged is None else merged + y
    if precise:
        out = _dot3(merged, wo_ref[...], w_refs[5][...])
    else:
        out = _dot(merged.astype(BF16), wo_ref[...])
    o_ref[...] = x_ref[...] + gate_ref[...] * out


def merge_out(branches, h, weights, layer, x, mod, tm):
    m, d = x.shape
    precise = len(weights) == 6
    br_spec = pl.BlockSpec((tm, W_MIX), lambda i: (i, 0))
    w_specs = [pl.BlockSpec((None, d, N_GATE), lambda i: (layer, 0, 0)),
               pl.BlockSpec((None, N_BRANCH, W_MIX, d), lambda i: (layer, 0, 0, 0)),
               pl.BlockSpec((None, d, d), lambda i: (layer, 0, 0))]
    return pl.pallas_call(
        functools.partial(_merge_kernel, precise=precise),
        grid=(m // tm,),
        in_specs=[br_spec] * N_BRANCH
                 + [pl.BlockSpec((tm, d), lambda i: (i, 0))]
                 + w_specs * (2 if precise else 1)
                 + [pl.BlockSpec((tm, d), lambda i: (i, 0)), mod.spec(2)],
        out_specs=pl.BlockSpec((tm, d), lambda i: (i, 0)),
        out_shape=jax.ShapeDtypeStruct((m, d), F32),
        compiler_params=_params("parallel"),
        name="merge_out",
    )(*branches, h, *weights, x, mod.arr)


def _route(scores, biased):
    lane_i = _iota(scores.shape, 1)
    real = lane_i < N_EXPERTS
    pos = lane_i % EXPERTS_PER_GROUP
    lane = lane_i.astype(F32)
    group = (lane_i // EXPERTS_PER_GROUP).astype(F32)
    neg = -jnp.inf
    n = scores.shape[1]
    mates = [biased]
    for k in range(1, EXPERTS_PER_GROUP):
        fwd = pltpu.roll(biased, n - k, axis=1)
        back = pltpu.roll(biased, EXPERTS_PER_GROUP - k, axis=1)
        mates.append(jnp.where(pos + k < EXPERTS_PER_GROUP, fwd, back))
    group_score = None
    for a in range(EXPERTS_PER_GROUP):
        for b in range(a + 1, EXPERTS_PER_GROUP):
            pair = mates[a] + mates[b]
            group_score = pair if group_score is None else jnp.maximum(group_score, pair)
    group_score = jnp.where(real, group_score, neg)
    best_score = jnp.max(group_score, axis=1, keepdims=True)
    best = jnp.min(jnp.where(group_score == best_score, group, float(n)), axis=1, keepdims=True)
    cand = jnp.where(real & (group == best), biased, neg)
    top0 = jnp.max(cand, axis=1, keepdims=True)
    idx0 = jnp.min(jnp.where(cand == top0, lane, float(n)), axis=1, keepdims=True)
    cand = jnp.where(lane == idx0, neg, cand)
    top1 = jnp.max(cand, axis=1, keepdims=True)
    idx1 = jnp.min(jnp.where(cand == top1, lane, float(n)), axis=1, keepdims=True)
    chosen = (lane == idx0) | (lane == idx1)
    sel = jnp.where(chosen, scores, 0.0)
    return sel / jnp.sum(sel, axis=1, keepdims=True), best


def _moe_kernel(x_ref, g_ref, sh_ref, sc_ref, gate_ref, rw_ref, rb_ref, w1_ref, w3_ref, w2_ref,
                gf_ref, o_ref, w1b_ref, w3b_ref, w2b_ref, h_scr, comb_scr, acc_scr, *, final_norm):
    e = pl.program_id(1)

    @pl.when(e == 0)
    def _():
        h = _norm_mod(x_ref[...], g_ref[...], sc_ref[...], sh_ref[...])
        h_scr[...] = h
        scores = jax.nn.sigmoid(_dot3(h, *_split2(rw_ref[...])))
        comb_scr[...] = _route(scores, scores + rb_ref[...])[0]
        acc_scr[...] = jnp.zeros_like(acc_scr)

    h = h_scr[...]
    w1_hi, w1_lo = _split2(w1_ref[...])
    w3_hi, w3_lo = _split2(w3_ref[...])
    w2_hi, w2_lo = _split2(w2_ref[...])
    w1b_ref[...] = w1_hi
    w3b_ref[...] = w3_hi
    w2b_ref[...] = w2_hi
    a = _dot3(h, w1_hi, w1_lo)
    b = _dot3(h, w3_hi, w3_lo)
    y = _dot3(_silu(a) * b, w2_hi, w2_lo)
    lane = _iota(comb_scr.shape, 1)
    w_e = jnp.sum(jnp.where(lane == e, comb_scr[...], 0.0), axis=1, keepdims=True)
    acc_scr[...] = acc_scr[...] + w_e * y

    @pl.when(e == N_EXPERTS - 1)
    def _():
        out = x_ref[...] + gate_ref[...] * acc_scr[...]
        if final_norm:
            ms = jnp.mean(out * out, axis=-1, keepdims=True)
            out = out * lax.rsqrt(ms + NORM_EPS) * gf_ref[...]
        o_ref[...] = out


def moe(x, g, mod, router_w_pad, router_b_pad, w1, w3, w2, layer, tm, final_gain=None):
    m, d = x.shape
    assert m == tm
    f = w1.shape[-1]
    n_e = w1.shape[1]
    expert = lambda i, e: (layer, e, 0, 0)
    per_layer = lambda i, e: (e, 0, 0)
    final_norm = final_gain is not None
    gf = (final_gain if final_norm else g).reshape(1, d)
    return pl.pallas_call(
        functools.partial(_moe_kernel, final_norm=final_norm),
        grid=(m // tm, N_EXPERTS),
        in_specs=[pl.BlockSpec((tm, d), lambda i, e: (i, 0)),
                  pl.BlockSpec((1, d), lambda i, e: (0, 0)),
                  mod.spec(3), mod.spec(4), mod.spec(5),
                  pl.BlockSpec((d, LANES), lambda i, e: (0, 0)),
                  pl.BlockSpec((1, LANES), lambda i, e: (0, 0)),
                  pl.BlockSpec((None, None, d, f), expert),
                  pl.BlockSpec((None, None, d, f), expert),
                  pl.BlockSpec((None, None, f, d), expert),
                  pl.BlockSpec((1, d), lambda i, e: (0, 0))],
        out_specs=[pl.BlockSpec((tm, d), lambda i, e: (i, 0)),
                   pl.BlockSpec((None, d, f), per_layer),
                   pl.BlockSpec((None, d, f), per_layer),
                   pl.BlockSpec((None, f, d), per_layer)],
        out_shape=[jax.ShapeDtypeStruct((m, d), F32),
                   jax.ShapeDtypeStruct((n_e, d, f), BF16),
                   jax.ShapeDtypeStruct((n_e, d, f), BF16),
                   jax.ShapeDtypeStruct((n_e, f, d), BF16)],
        scratch_shapes=[pltpu.VMEM((tm, d), F32), pltpu.VMEM((tm, LANES), F32),
                        pltpu.VMEM((tm, d), F32)],
        compiler_params=_params("arbitrary", "arbitrary"),
        name="moe",
    )(x, g.reshape(1, d), mod.arr, mod.arr, mod.arr, router_w_pad, router_b_pad, w1, w3, w2, gf)


_MOE_SUB = 128
N_GROUPS = N_EXPERTS // EXPERTS_PER_GROUP


def _moe_grouped_kernel(x_ref, g_ref, sh_ref, sc_ref, gate_ref, rw_ref, rb_ref, w1_ref, w3_ref,
                        w2_ref, gf_ref, o_ref, hs_scr, ys_scr, cs_scr, pt_scr, meta_ref, *,
                        final_norm):
    e = pl.program_id(1)
    tm = x_ref.shape[0]
    slots = hs_scr.shape[0]
    sub = _MOE_SUB

    @pl.when(e == 0)
    def _():
        h = _norm_mod(x_ref[...], g_ref[...], sc_ref[...], sh_ref[...])
        hb = h.astype(BF16)
        scores = jax.nn.sigmoid(_dot(hb, rw_ref[...].astype(BF16)))
        comb, best = _route(scores, scores + rb_ref[...])
        lane = _iota((tm, LANES), 1).astype(F32)
        member = jnp.where(lane == best, 1.0, 0.0)
        before = (_iota((tm, tm), 1) < _iota((tm, tm), 0)).astype(BF16)
        rank = _dot(before, member.astype(BF16))
        count = rank[tm - 1:tm, :] + member[tm - 1:tm, :]
        room = jnp.floor((count + (sub - 1)) * (1.0 / sub)) * sub
        lane_row = _iota((1, LANES), 1)
        start = jnp.zeros((1, LANES), F32)
        for k in range(1, N_GROUPS):
            start = start + jnp.where(lane_row >= k, pltpu.roll(room, k, axis=1), 0.0)
        dest = jnp.sum(member * (start + rank), axis=1, keepdims=True)
        place = jnp.where(_iota((tm, slots), 1).astype(F32) == dest, 1.0, 0.0).astype(BF16)
        pt_scr[...] = place
        hs_scr[...] = _dot_tn(place, hb).astype(BF16)
        c_hi, c_lo = _split2(comb)
        cs_scr[...] = _dot_tn(place, c_hi) + _dot_tn(place, c_lo)
        ys_scr[...] = jnp.zeros_like(ys_scr)
        for k in range(N_GROUPS):
            pick = lane_row == k
            meta_ref[k] = jnp.sum(jnp.where(pick, start, 0.0)).astype(jnp.int32)
            meta_ref[N_GROUPS + k] = jnp.sum(jnp.where(pick, room, 0.0)).astype(jnp.int32) // sub

    group = e // EXPERTS_PER_GROUP
    first = meta_ref[group]
    lane = _iota((sub, LANES), 1)

    def body(s, carry):
        r0 = pl.multiple_of(first + s * sub, sub)
        rows = hs_scr[pl.ds(r0, sub), :]
        a = _dot(rows, w1_ref[...])
        b = _dot(rows, w3_ref[...])
        y = _dot((_silu(a) * b).astype(BF16), w2_ref[...])
        w_e = jnp.sum(jnp.where(lane == e, cs_scr[pl.ds(r0, sub), :], 0.0), axis=1, keepdims=True)
        ys_scr[pl.ds(r0, sub), :] = ys_scr[pl.ds(r0, sub), :] + w_e * y
        return carry

    lax.fori_loop(0, meta_ref[N_GROUPS + group], body, 0)

    @pl.when(e == N_EXPERTS - 1)
    def _():
        y_hi, y_lo = _split2(ys_scr[...])
        place = pt_scr[...]
        out = x_ref[...] + gate_ref[...] * (_dot(place, y_hi) + _dot(place, y_lo))
        if final_norm:
            ms = jnp.mean(out * out, axis=-1, keepdims=True)
            out = out * lax.rsqrt(ms + NORM_EPS) * gf_ref[...]
        o_ref[...] = out


def moe_grouped(x, g, mod, router_w_pad, router_b_pad, w1, w3, w2, tm, final_gain=None):
    m, d = x.shape
    f = w1.shape[-1]
    slots = tm + N_GROUPS * _MOE_SUB
    expert = lambda i, e: (e, 0, 0)
    final_norm = final_gain is not None
    gf = (final_gain if final_norm else g).reshape(1, d)
    return pl.pallas_call(
        functools.partial(_moe_grouped_kernel, final_norm=final_norm),
        grid=(m // tm, N_EXPERTS),
        in_specs=[pl.BlockSpec((tm, d), lambda i, e: (i, 0)),
                  pl.BlockSpec((1, d), lambda i, e: (0, 0)),
                  mod.spec(3), mod.spec(4), mod.spec(5),
                  pl.BlockSpec((d, LANES), lambda i, e: (0, 0)),
                  pl.BlockSpec((1, LANES), lambda i, e: (0, 0)),
                  pl.BlockSpec((None, d, f), expert),
                  pl.BlockSpec((None, d, f), expert),
                  pl.BlockSpec((None, f, d), expert),
                  pl.BlockSpec((1, d), lambda i, e: (0, 0))],
        out_specs=pl.BlockSpec((tm, d), lambda i, e: (i, 0)),
        out_shape=jax.ShapeDtypeStruct((m, d), F32),
        scratch_shapes=[pltpu.VMEM((slots, d), BF16), pltpu.VMEM((slots, d), F32),
                        pltpu.VMEM((slots, LANES), F32), pltpu.VMEM((tm, slots), BF16),
                        pltpu.SMEM((2 * N_GROUPS,), jnp.int32)],
        compiler_params=_params("parallel", "arbitrary"),
        name="moe_grouped",
    )(x, g.reshape(1, d), mod.arr, mod.arr, mod.arr, router_w_pad, router_b_pad, w1, w3, w2, gf)


def _rope_tables(pos):
    half = HEAD_DIM // 2
    inv = ROPE_BASE ** (-jnp.arange(half, dtype=F32) / half)
    ang = pos.astype(F32)[:, None] * inv[None, :]
    cos, sin = jnp.cos(ang), jnp.sin(ang)
    cos_row = jnp.tile(jnp.concatenate([cos, cos], axis=1), (1, N_HEADS))
    sin_row = jnp.tile(jnp.concatenate([-sin, sin], axis=1), (1, N_HEADS))
    return cos_row, sin_row


def _retention_tables(pos, chunk):
    lg = jnp.log1p(-(2.0 ** (-5.0 - jnp.arange(N_HEADS, dtype=F32))))
    idx = jnp.arange(chunk, dtype=F32)
    diff = idx[:, None] - idx[None, :]
    inner = jnp.where(diff >= 0, jnp.exp(lg[:, None, None] * jnp.maximum(diff, 0.0)), 0.0)
    q_decay = jnp.exp(lg[None, :] * (idx[:, None] + 1.0))
    k_decay = jnp.exp(lg[None, :] * (chunk - 1.0 - idx[:, None]))
    chunk_decay = jnp.exp(lg * chunk)
    lanes = lambda a: jnp.repeat(a, HEAD_DIM, axis=-1)
    cos, sin = _rope_tables(pos)
    return {"cos": cos, "sin": sin, "inner": inner, "q_decay": lanes(q_decay),
            "k_decay": lanes(k_decay), "chunk_decay": lanes(chunk_decay[None, :]),
            "gamma": chunk_decay}


def _pack_w_kernel(w_ref, wm_ref, wg_ref, wm_lo_ref, wg_lo_ref):
    rows = w_ref.shape[0]

    def put(hi_ref, lo_ref, cols, val):
        hi, lo = _split2(val)
        hi_ref[:, cols] = hi
        lo_ref[:, cols] = lo

    put(wm_ref, wm_lo_ref, slice(0, N_MAIN), w_ref[:, 0:N_MAIN])
    lane = _iota((rows, W_MIX), 1)
    put(wm_ref, wm_lo_ref, slice(N_MAIN, GATE_COL0),
        jnp.where(lane < N_HEADS, w_ref[:, N_MAIN:GATE_COL0], 0.0))
    step = 1024
    for c in range(0, N_GATE, step):
        width = min(step + LANES, N_GATE + N_HEADS - c)
        win = w_ref[:, N_MAIN + c:N_MAIN + c + width]
        put(wg_ref, wg_lo_ref, slice(c, c + step), win[:, N_HEADS:N_HEADS + step])


def pack_w_in(w_in):
    depth, d, n_in = w_in.shape
    assert n_in == N_MAIN + N_HEADS + N_GATE
    tr = 256
    main = pl.BlockSpec((None, tr, GATE_COL0), lambda l, r: (l, r, 0))
    gate = pl.BlockSpec((None, tr, N_GATE), lambda l, r: (l, r, 0))
    main_shape = jax.ShapeDtypeStruct((depth, d, GATE_COL0), BF16)
    gate_shape = jax.ShapeDtypeStruct((depth, d, N_GATE), BF16)
    return pl.pallas_call(
        _pack_w_kernel,
        grid=(depth, d // tr),
        in_specs=[pl.BlockSpec((None, tr, n_in), lambda l, r: (l, r, 0))],
        out_specs=[main, gate, main, gate],
        out_shape=[main_shape, gate_shape, main_shape, gate_shape],
        compiler_params=_params("parallel", "parallel"),
        name="pack_w_in",
    )(w_in)


def _split_w_kernel(w_ref, hi_ref, lo_ref):
    hi, lo = _split2(w_ref[...])
    hi_ref[...] = hi
    lo_ref[...] = lo


def split_weights(w):
    cols = w.shape[-1]
    flat = w.reshape(-1, cols)
    tr = 256
    spec = pl.BlockSpec((tr, cols), lambda i: (i, 0))
    shape = jax.ShapeDtypeStruct(flat.shape, BF16)
    hi, lo = pl.pallas_call(
        _split_w_kernel,
        grid=(flat.shape[0] // tr,),
        in_specs=[spec],
        out_specs=[spec, spec],
        out_shape=[shape, shape],
        compiler_params=_params("parallel"),
        name="split_weights",
    )(flat)
    return hi.reshape(w.shape), lo.reshape(w.shape)


def _block_diag_pool(w_pool_l):
    g, c, _ = w_pool_l.shape
    out = jnp.zeros((g * c, g * c), w_pool_l.dtype)
    for i in range(g):
        out = out.at[i * c:(i + 1) * c, i * c:(i + 1) * c].set(w_pool_l[i])
    return out


def _pad_lanes(row, n):
    return jnp.pad(row, ((0, 0), (0, n - row.shape[1])))


def _diag_blocks(s_bd):
    return jnp.stack([s_bd[:, h * HEAD_DIM:(h + 1) * HEAD_DIM, h * HEAD_DIM:(h + 1) * HEAD_DIM]
                      for h in range(N_HEADS)], axis=1)


def kernel(x_prompt, x_sample, cache_sb_kv, cache_fox_kv, cache_fox_logf, state_pool, state_ret,
           page_table, c_prompt, c_sample, w_ada, b_ada, norm_mix, w_in, w_pool, pool_scale,
           fox_bias, w_branch, w_out, norm_ffn, router_w, router_b, w1, w3, w2, norm_final):
    bp, tp, d = x_prompt.shape
    db, ts, _ = x_sample.shape
    assert ts == 1 and d == D_MODEL
    depth = w_in.shape[0]
    n_phys, _, _, page, _, _ = cache_sb_kv.shape
    n_pages = page_table.shape[1]
    past_len = n_pages * page
    mp = bp * tp

    mod_all = adaln(jnp.concatenate([c_prompt, c_sample], axis=0), w_ada, b_ada)

    sb_cache = jnp.transpose(cache_sb_kv, (0, 1, 2, 4, 5, 3)).reshape(n_phys, depth, 2, W_MIX, page)
    fox_cache = jnp.transpose(cache_fox_kv, (0, 1, 2, 4, 5, 3)).reshape(n_phys, depth, 2, W_MIX, page)
    logf_t = jnp.swapaxes(cache_fox_logf, 2, 3)
    state_t = jnp.transpose(state_ret, (1, 2, 3, 4, 0)).reshape(depth * N_HEADS, HEAD_DIM, HEAD_DIM, db)
    w_main, w_gate, w_main_lo, w_gate_lo = pack_w_in(w_in)
    router_w_pad = _pad_lanes(router_w, LANES)
    router_b_pad = _pad_lanes(router_b[None, :], LANES)
    tabs_p = _retention_tables(jnp.arange(tp), RET_CHUNK if tp % RET_CHUNK == 0 else tp)
    tabs_s = _retention_tables(past_len + jnp.arange(ts), ts)
    wb, wb_lo = split_weights(w_branch)
    wo, wo_lo = split_weights(w_out)
    w_merge_p = (w_gate, wb, wo)
    w_merge_s = w_merge_p + (w_gate_lo, wb_lo, wo_lo)

    tm_p = min(1024, tp)
    tm_mrg = min(512, tp)
    xp = x_prompt.reshape(mp, d)
    xs = x_sample.reshape(db, d)
    outs_p = {k: [] for k in ("logf", "pool", "ret")}
    outs_s = {k: [] for k in ("sb", "fox", "logf", "pool", "ret")}
    kv_prev = None

    for l in range(depth):
        w_bd_f32 = _block_diag_pool(w_pool[l])
        w_bd = w_bd_f32.astype(BF16)
        scale_row = pool_scale[l][None, :]
        bias_lanes = _pad_lanes(fox_bias[l][None, :], LANES)
        bias_row = _pad_lanes(fox_bias[l][None, :], W_MIX)

        final_gain = norm_final if l == depth - 1 else None

        mod_s = Mod(mod_all[l, bp:], 1, db)
        proj_s, h_s = in_proj(xs, norm_mix[l], mod_s, w_main, l, db, w_lo=w_main_lo)
        buf_t = jnp.swapaxes(state_pool[:, l], 0, 1)
        pool_s = pool_sample(buf_t, proj_s, w_bd_f32, scale_row, past_len)
        sb_s, fox_s, logf_s = decode_attn(page_table, proj_s, sb_cache, fox_cache, logf_t, bias_row, l)
        ret_s, state_new = ret_sample(proj_s, state_t, l, tabs_s)
        xs = merge_out((pool_s, sb_s, ret_s, fox_s), h_s, w_merge_s, l, xs, mod_s, db)
        xs, w1b, w3b, w2b = moe(xs, norm_ffn[l], mod_s, router_w_pad, router_b_pad, w1, w3, w2, l,
                                db, final_gain)

        mod_in = Mod(mod_all[l, :bp], tp, tm_p)
        mod_mrg = Mod(mod_all[l, :bp], tp, tm_mrg)
        kv_sb, kv_fox, proj, h = in_proj(xp, norm_mix[l], mod_in, w_main, l, tm_p,
                                         kv_seq=(bp, tp), kv_prev=kv_prev)
        kv_prev = (kv_sb, kv_fox)
        pool_o = pool_prompt(proj, w_bd, scale_row, bp, tp)
        sb_o = sb_prompt(proj, bp, tp)
        ret_o, ret_state = ret_prompt(proj, tabs_p, bp, tp)
        logf, cum, cum_t = logf_cum_prompt(proj, bias_lanes, bp, tp)
        fox_o = fox_prompt(proj, cum, cum_t, bp, tp)
        xp = merge_out((pool_o, sb_o, ret_o, fox_o), h, w_merge_p, l, xp, mod_mrg, tm_mrg)
        xp = moe_grouped(xp, norm_ffn[l], mod_in, router_w_pad, router_b_pad, w1b, w3b, w2b, tm_p,
                         final_gain)
        p3 = proj.reshape(bp, tp, GATE_COL0)
        outs_p["logf"].append(logf.reshape(bp, tp, LANES)[:, :, :N_HEADS])
        outs_p["pool"].append(p3[:, tp - POOL_BUF:, :W_MIX])
        outs_p["ret"].append(_diag_blocks(ret_state))

        def kv_s(cb, proj_s=proj_s):
            return jnp.stack([proj_s[:, cb * W_MIX:(cb + 1) * W_MIX],
                              proj_s[:, (cb + 1) * W_MIX:(cb + 2) * W_MIX]], axis=1)

        outs_s["sb"].append(kv_s(COL_SBK))
        outs_s["fox"].append(kv_s(COL_FK))
        outs_s["logf"].append(logf_s[:, :N_HEADS])
        outs_s["pool"].append(jnp.concatenate([state_pool[:, l, 1:], proj_s[:, None, :W_MIX]], axis=1))
        outs_s["ret"].append(state_new)

    y_p = xp.reshape(bp, tp, d)
    y_s = xs.reshape(db, ts, d)

    def heads(a, t):
        return a.reshape(a.shape[0], depth, 2, t, N_HEADS, HEAD_DIM)

    def from_transposed(kv):
        return heads(jnp.transpose(kv, (2, 0, 1, 4, 3)), tp)

    return (y_p, y_s,
            from_transposed(kv_prev[0]),
            heads(jnp.stack(outs_s["sb"], axis=1), ts),
            from_transposed(kv_prev[1]),
            heads(jnp.stack(outs_s["fox"], axis=1), ts),
            jnp.stack(outs_p["logf"], axis=1),
            jnp.stack(outs_s["logf"], axis=1).reshape(db, depth, ts, N_HEADS),
            jnp.stack(outs_p["pool"], axis=1),
            jnp.stack(outs_s["pool"], axis=1),
            jnp.stack(outs_p["ret"], axis=1),
            jnp.transpose(jnp.stack(outs_s["ret"], axis=0), (4, 0, 1, 2, 3)))
```

```python
import functools

import numpy as np
import jax
import jax.numpy as jnp
from jax import lax
from jax.experimental import pallas as pl
from jax.experimental.pallas import tpu as pltpu

F32 = jnp.float32
BF16 = jnp.bfloat16

D_MODEL = 1024
HEAD_DIM = 64
W_MIX = 256
N_HEADS = W_MIX // HEAD_DIM
N_BRANCH = 4
POOL_WINDOWS = (2, 4, 8, 16)
POOL_BUF = 15
RET_CHUNK = 128
ROPE_BASE = 10000.0
N_EXPERTS = 16
EXPERTS_PER_GROUP = 4
D_EXPERT = 512
N_MOD = 6
NORM_EPS = 1e-6
ATTN_SCALE = HEAD_DIM ** -0.5
LANES = 128
SUBLANES = 8
VMEM_LIMIT = 56 * 1024 * 1024

COL_U, COL_SBQ, COL_SBK, COL_SBV = 0, 1, 2, 3
COL_RQ, COL_RK, COL_RV, COL_RG = 4, 5, 6, 7
COL_FQ, COL_FK, COL_FV, COL_LOGIT = 8, 9, 10, 11
N_MAIN = 11 * W_MIX
GATE_COL0 = 12 * W_MIX
N_GATE = N_BRANCH * D_MODEL


def _params(*sem):
    return pltpu.CompilerParams(dimension_semantics=sem, vmem_limit_bytes=VMEM_LIMIT)


def _split2(x):
    hi = x.astype(BF16)
    lo = (x - hi.astype(F32)).astype(BF16)
    return hi, lo


def _split3(x):
    hi = x.astype(BF16)
    r = x - hi.astype(F32)
    mid = r.astype(BF16)
    lo = (r - mid.astype(F32)).astype(BF16)
    return hi, mid, lo


def _dot(a, b):
    return jnp.dot(a, b, preferred_element_type=F32)


def _dot_nt(a, b):
    return lax.dot_general(a, b, (((1,), (1,)), ((), ())), preferred_element_type=F32)


def _dot_tn(a, b):
    return lax.dot_general(a, b, (((0,), (0,)), ((), ())), preferred_element_type=F32)


def _dot_split(x, w, parts=2):
    ps = _split2(x) if parts == 2 else _split3(x)
    acc = _dot(ps[0], w)
    for p in ps[1:]:
        acc = acc + _dot(p, w)
    return acc


def _dot3(x, w_hi, w_lo):
    x_hi, x_lo = _split2(x)
    return _dot(x_hi, w_hi) + (_dot(x_lo, w_hi) + _dot(x_hi, w_lo))


def _iota(shape, dim):
    return lax.broadcasted_iota(jnp.int32, shape, dim)


def _softplus(z):
    return jnp.maximum(z, 0.0) + jnp.log1p(jnp.exp(-jnp.abs(z)))


def _silu(x):
    return x * jax.nn.sigmoid(x)


def _adaln_kernel(c_ref, w_ref, b_ref, o_ref):
    w_hi, w_lo = _split2(w_ref[...])
    o_ref[...] = _dot3(_silu(c_ref[...]), w_hi, w_lo) + b_ref[...]


def adaln(c_all, w_ada, b_ada):
    rows, d = c_all.shape
    depth, _, n = w_ada.shape
    tn = 1024
    return pl.pallas_call(
        _adaln_kernel,
        grid=(depth, n // tn),
        in_specs=[pl.BlockSpec((rows, d), lambda l, j: (0, 0)),
                  pl.BlockSpec((None, d, tn), lambda l, j: (l, 0, j)),
                  pl.BlockSpec((None, 1, tn), lambda l, j: (l, 0, j))],
        out_specs=pl.BlockSpec((None, rows, tn), lambda l, j: (l, 0, j)),
        out_shape=jax.ShapeDtypeStruct((depth, rows, n), F32),
        compiler_params=_params("parallel", "parallel"),
        name="adaln",
    )(c_all, w_ada, b_ada.reshape(depth, 1, n))


class Mod:
    def __init__(self, arr, rows_per_vec, tm):
        self.per_row = rows_per_vec == 1
        self.arr = arr if self.per_row else arr.reshape(arr.shape[0], 1, arr.shape[-1])
        self.tiles_per_vec = 1 if self.per_row else rows_per_vec // tm
        self.tm = tm

    def spec(self, k):
        if self.per_row:
            return pl.BlockSpec((self.tm, D_MODEL), lambda i, *_: (i, k))
        t = self.tiles_per_vec
        return pl.BlockSpec((None, 1, D_MODEL), lambda i, *_: (i // t, 0, k))


def _norm_mod(x, g, sc, sh):
    ms = jnp.mean(x * x, axis=-1, keepdims=True)
    y = x * lax.rsqrt(ms + NORM_EPS) * g
    return y * (1.0 + sc) + sh


_PROJ_TN = 1024


def _inproj_kernel(x_ref, g_ref, sh_ref, sc_ref, w_ref, *refs, kv_out, precise):
    j = pl.program_id(1)
    o_ref, h_ref = refs[-2], refs[-1]

    @pl.when(j == 0)
    def _():
        h_ref[...] = _norm_mod(x_ref[...], g_ref[...], sc_ref[...], sh_ref[...]).astype(h_ref.dtype)

    if precise:
        res = _dot3(h_ref[...], w_ref[...], refs[0][...])
    else:
        res = _dot(h_ref[...], w_ref[...])
    o_ref[...] = res
    if kv_out:
        per_tile = _PROJ_TN // W_MIX
        for col_k, kv_ref in ((COL_SBK, refs[-4]), (COL_FK, refs[-3])):
            c0 = (col_k % per_tile) * W_MIX

            @pl.when(j == col_k // per_tile)
            def _(c0=c0, kv_ref=kv_ref):
                kv_ref[0] = jnp.transpose(res[:, c0:c0 + W_MIX])
                kv_ref[1] = jnp.transpose(res[:, c0 + W_MIX:c0 + 2 * W_MIX])


def in_proj(x, g, mod, w_all, layer, tm, kv_seq=None, kv_prev=None, w_lo=None):
    m, d = x.shape
    depth, _, n = w_all.shape
    tn = _PROJ_TN
    precise = w_lo is not None
    w_spec = pl.BlockSpec((None, d, tn), lambda i, j: (layer, 0, j))
    in_specs = [pl.BlockSpec((tm, d), lambda i, j: (i, 0)),
                pl.BlockSpec((1, d), lambda i, j: (0, 0)),
                mod.spec(0), mod.spec(1), w_spec]
    args = [x, g.reshape(1, d), mod.arr, mod.arr, w_all]
    if precise:
        in_specs.append(w_spec)
        args.append(w_lo)
    out_specs = [pl.BlockSpec((tm, tn), lambda i, j: (i, j)),
                 pl.BlockSpec((tm, d), lambda i, j: (i, 0))]
    out_shape = [jax.ShapeDtypeStruct((m, n), F32),
                 jax.ShapeDtypeStruct((m, d), F32 if precise else BF16)]
    aliases = {}
    if kv_seq is not None:
        b, t = kv_seq
        per_b = t // tm
        kv_spec = pl.BlockSpec((None, 2, None, W_MIX, tm),
                               lambda i, j: (layer, 0, i // per_b, 0, i % per_b))
        kv_shape = jax.ShapeDtypeStruct((depth, 2, b, W_MIX, t), F32)
        out_specs = [kv_spec, kv_spec] + out_specs
        out_shape = [kv_shape, kv_shape] + out_shape
        if kv_prev is not None:
            aliases = {len(args): 0, len(args) + 1: 1}
            in_specs += [pl.BlockSpec(memory_space=pl.ANY)] * 2
            args += list(kv_prev)
    outs = pl.pallas_call(
        functools.partial(_inproj_kernel, kv_out=kv_seq is not None, precise=precise),
        grid=(m // tm, n // tn),
        in_specs=in_specs,
        out_specs=out_specs,
        out_shape=out_shape,
        input_output_aliases=aliases,
        compiler_params=_params("parallel", "arbitrary"),
        name="in_proj",
    )(*args)
    return outs


def _pool_tail(s2, s4, s8, s16, u, pos0, w_ref, sc_ref, o_ref):
    t = u.shape[0]
    lane = _iota((1, W_MIX), 1) // (W_MIX // len(POOL_WINDOWS))
    win = jnp.where(lane == 0, s2, jnp.where(lane == 1, s4, jnp.where(lane == 2, s8, s16)))
    width = jnp.where(lane == 0, 2, jnp.where(lane == 1, 4, jnp.where(lane == 2, 8, 16)))
    count = jnp.minimum(width, pos0 + 1 + _iota((t, W_MIX), 0)).astype(F32)
    resid = win / count - u
    if w_ref.dtype == F32:
        w_hi, w_lo = _split2(w_ref[...])
        mixed = _dot3(resid, w_hi, w_lo)
    else:
        mixed = _dot(resid.astype(BF16), w_ref[...])
    o_ref[...] = (mixed * sc_ref[...]).astype(o_ref.dtype)


_POOL_PAD = 32


def _pool_prompt_kernel(u_ref, w_ref, sc_ref, o_ref, a_scr, b_scr, c_scr):
    t = u_ref.shape[0]
    p = _POOL_PAD
    u = u_ref[...]
    a_scr[0:p, :] = jnp.zeros((p, W_MIX), F32)
    a_scr[p:p + t, :] = u

    def stage(src, dst, k, lo):
        n = p + t - lo
        dst[lo:lo + n, :] = src[lo:lo + n, :] + src[lo - k:lo - k + n, :]

    stage(a_scr, b_scr, 1, 8)
    stage(b_scr, c_scr, 2, 16)
    stage(c_scr, a_scr, 4, 24)
    s8 = a_scr[p:p + t, :]
    s16 = s8 + a_scr[p - 8:p - 8 + t, :]
    _pool_tail(b_scr[p:p + t, :], c_scr[p:p + t, :], s8, s16, u, 0, w_ref, sc_ref, o_ref)


def pool_prompt(proj, w_bd, scale, b, t):
    return pl.pallas_call(
        _pool_prompt_kernel,
        grid=(b,),
        in_specs=[pl.BlockSpec((t, W_MIX), lambda i: (i, COL_U)),
                  pl.BlockSpec((W_MIX, W_MIX), lambda i: (0, 0)),
                  pl.BlockSpec((1, W_MIX), lambda i: (0, 0))],
        out_specs=pl.BlockSpec((t, W_MIX), lambda i: (i, 0)),
        out_shape=jax.ShapeDtypeStruct((b * t, W_MIX), BF16),
        scratch_shapes=[pltpu.VMEM((t + _POOL_PAD, W_MIX), F32)] * 3,
        compiler_params=_params("parallel"),
        name="pool_prompt",
    )(proj, w_bd, scale)


def _pool_sample_kernel(buf_ref, u_ref, w_ref, sc_ref, o_ref, *, pos0):
    u = u_ref[...]
    s2 = u + buf_ref[14]
    s4 = s2 + buf_ref[13] + buf_ref[12]
    s8 = s4 + buf_ref[11] + buf_ref[10] + buf_ref[9] + buf_ref[8]
    s16 = s8
    for r in range(7, -1, -1):
        s16 = s16 + buf_ref[r]
    _pool_tail(s2, s4, s8, s16, u, pos0, w_ref, sc_ref, o_ref)


def pool_sample(buf_t, proj, w_bd, scale, pos0):
    rows = proj.shape[0]
    return pl.pallas_call(
        functools.partial(_pool_sample_kernel, pos0=pos0),
        grid=(1,),
        in_specs=[pl.BlockSpec((POOL_BUF, rows, W_MIX), lambda i: (0, 0, 0)),
                  pl.BlockSpec((rows, W_MIX), lambda i: (0, COL_U)),
                  pl.BlockSpec((W_MIX, W_MIX), lambda i: (0, 0)),
                  pl.BlockSpec((1, W_MIX), lambda i: (0, 0))],
        out_specs=pl.BlockSpec((rows, W_MIX), lambda i: (0, 0)),
        out_shape=jax.ShapeDtypeStruct((rows, W_MIX), F32),
        compiler_params=_params("arbitrary"),
        name="pool_sample",
    )(buf_t, proj, w_bd, scale)


_ATT_TILE = 256
_ATT_CHAINS = 1


def _head_of_lane():
    return _iota((1, W_MIX), 1) // HEAD_DIM


def _stack_heads(q):
    head = _head_of_lane()
    return jnp.concatenate([jnp.where(head == h, q, 0.0) for h in range(N_HEADS)], axis=0).astype(BF16)


def _unstack_heads(acc, tq):
    head = _head_of_lane()
    out = jnp.where(head == 0, acc[0:tq], 0.0)
    for h in range(1, N_HEADS):
        out = out + jnp.where(head == h, acc[h * tq:(h + 1) * tq], 0.0)
    return out


def _load_kv_once(k_ref, v_ref, k_scr, v_scr):
    @pl.when(pl.program_id(1) == 0)
    def _():
        k_scr[...] = k_ref[...].astype(BF16)
        v_scr[...] = v_ref[...].astype(BF16)


def _sb_prompt_kernel(q_ref, k_ref, v_ref, o_ref, k_scr, v_scr):
    tq = tk = _ATT_TILE
    i = pl.program_id(1)
    _load_kv_once(k_ref, v_ref, k_scr, v_scr)
    q4 = _stack_heads(q_ref[...] * ATTN_SCALE)
    rows = N_HEADS * tq // _ATT_CHAINS
    q_parts = [q4[c * rows:(c + 1) * rows] for c in range(_ATT_CHAINS)]
    below_diag = _iota((rows, tk), 1) < (_iota((rows, tk), 0) % tq)
    upper = (_iota((tk, tk), 0) > _iota((tk, tk), 1)).astype(BF16)

    def block(off, diagonal, carry):
        kb = k_scr[pl.ds(off, tk), :]
        vb = v_scr[pl.ds(off, tk), :]
        out = []
        for qp, (acc, run) in zip(q_parts, carry):
            z = _dot_nt(qp, kb)
            sp = jnp.maximum(z, 0.0) + jnp.log(1.0 + jnp.exp(-jnp.abs(z)))
            log_not = jnp.where(below_diag, -sp, 0.0) if diagonal else -sp
            suffix = _dot_split(log_not, upper)
            a = jnp.exp(z - sp + suffix + run)
            if diagonal:
                a = jnp.where(below_diag, a, 0.0)
            out.append((acc + _dot(a.astype(BF16), vb), run + suffix[:, 0:1] + log_not[:, 0:1]))
        return tuple(out)

    zero = (jnp.zeros((rows, W_MIX), F32), jnp.zeros((rows, 1), F32))
    carry = block(pl.multiple_of(i * tk, tk), True, (zero,) * _ATT_CHAINS)
    carry = lax.fori_loop(
        0, i, lambda s, c: block(pl.multiple_of((i - 1 - s) * tk, tk), False, c), carry)
    acc = jnp.concatenate([c[0] for c in carry], axis=0)
    o_ref[...] = _unstack_heads(acc, tq).astype(o_ref.dtype)


def _attn_prompt_specs(t, col_q, col_k, col_v):
    tq = _ATT_TILE
    nq = t // tq
    return [pl.BlockSpec((tq, W_MIX), lambda b, i: (b * nq + i, col_q)),
            pl.BlockSpec((t, W_MIX), lambda b, i: (b, col_k)),
            pl.BlockSpec((t, W_MIX), lambda b, i: (b, col_v))]


def sb_prompt(proj, b, t):
    tq = _ATT_TILE
    nq = t // tq
    return pl.pallas_call(
        _sb_prompt_kernel,
        grid=(b, nq),
        in_specs=_attn_prompt_specs(t, COL_SBQ, COL_SBK, COL_SBV),
        out_specs=pl.BlockSpec((tq, W_MIX), lambda bb, i: (bb * nq + i, 0)),
        out_shape=jax.ShapeDtypeStruct((b * t, W_MIX), BF16),
        scratch_shapes=[pltpu.VMEM((t, W_MIX), BF16)] * 2,
        compiler_params=_params("parallel", "arbitrary"),
        name="sb_prompt",
    )(proj, proj, proj)


def _logf_cum_kernel(x_ref, bias_ref, logf_ref, cum_ref, cumt_ref):
    t = x_ref.shape[0]
    blk = 256
    logf = jax.nn.log_sigmoid(x_ref[...] + bias_ref[...])
    logf_ref[...] = logf
    lower = (_iota((blk, blk), 1) <= _iota((blk, blk), 0)).astype(BF16)
    carry = jnp.zeros((1, LANES), F32)
    for c in range(t // blk):
        part = logf[c * blk:(c + 1) * blk]
        hi, mid, lo = _split3(part)
        cum = _dot(lower, hi) + _dot(lower, mid) + _dot(lower, lo) + carry
        cum_ref[c * blk:(c + 1) * blk, :] = cum
        carry = cum[blk - 1:blk, :]
    cumt_ref[...] = jnp.transpose(cum_ref[...])[0:SUBLANES, :]


def logf_cum_prompt(proj, bias_pad, b, t):
    m = b * t
    return pl.pallas_call(
        _logf_cum_kernel,
        grid=(b,),
        in_specs=[pl.BlockSpec((t, LANES), lambda i: (i, COL_LOGIT * W_MIX // LANES)),
                  pl.BlockSpec((1, LANES), lambda i: (0, 0))],
        out_specs=[pl.BlockSpec((t, LANES), lambda i: (i, 0)),
                   pl.BlockSpec((t, LANES), lambda i: (i, 0)),
                   pl.BlockSpec((None, SUBLANES, t), lambda i: (i, 0, 0))],
        out_shape=[jax.ShapeDtypeStruct((m, LANES), F32),
                   jax.ShapeDtypeStruct((m, LANES), F32),
                   jax.ShapeDtypeStruct((b, SUBLANES, t), F32)],
        compiler_params=_params("parallel"),
        name="logf_cum",
    )(proj, bias_pad)


def _fox_prompt_kernel(q_ref, k_ref, v_ref, cq_ref, ck_ref, o_ref, k_scr, v_scr):
    tq = tk = _ATT_TILE
    i = pl.program_id(1)
    _load_kv_once(k_ref, v_ref, k_scr, v_scr)
    q4 = _stack_heads(q_ref[...] * ATTN_SCALE)
    rows = N_HEADS * tq
    on_or_below_diag = _iota((rows, tk), 1) <= (_iota((rows, tk), 0) % tq)
    cq = cq_ref[...]

    def block(off, diagonal, carry):
        acc, m, l = carry
        kb = k_scr[pl.ds(off, tk), :]
        vb = v_scr[pl.ds(off, tk), :]
        z = _dot_nt(q4, kb)
        sc = jnp.concatenate(
            [z[h * tq:(h + 1) * tq] + (cq[:, h:h + 1] - ck_ref[h:h + 1, pl.ds(off, tk)])
             for h in range(N_HEADS)], axis=0)
        if diagonal:
            sc = jnp.where(on_or_below_diag, sc, -jnp.inf)
        m_new = jnp.maximum(m, jnp.max(sc, axis=1, keepdims=True))
        alpha = jnp.exp(m - m_new)
        p = jnp.exp(sc - m_new)
        return (alpha * acc + _dot(p.astype(BF16), vb), m_new,
                alpha * l + jnp.sum(p, axis=1, keepdims=True))

    carry = block(pl.multiple_of(i * tk, tk), True,
                  (jnp.zeros((rows, W_MIX), F32), jnp.full((rows, 1), -jnp.inf, F32),
                   jnp.zeros((rows, 1), F32)))
    acc, _, l = lax.fori_loop(
        0, i, lambda s, c: block(pl.multiple_of((i - 1 - s) * tk, tk), False, c), carry)
    o_ref[...] = _unstack_heads(acc / l, tq).astype(o_ref.dtype)


def fox_prompt(proj, cum, cum_t, b, t):
    tq = _ATT_TILE
    nq = t // tq
    specs = _attn_prompt_specs(t, COL_FQ, COL_FK, COL_FV)
    specs += [pl.BlockSpec((tq, LANES), lambda bb, i: (bb * nq + i, 0)),
              pl.BlockSpec((None, SUBLANES, t), lambda bb, i: (bb, 0, 0))]
    return pl.pallas_call(
        _fox_prompt_kernel,
        grid=(b, nq),
        in_specs=specs,
        out_specs=pl.BlockSpec((tq, W_MIX), lambda bb, i: (bb * nq + i, 0)),
        out_shape=jax.ShapeDtypeStruct((b * t, W_MIX), BF16),
        scratch_shapes=[pltpu.VMEM((t, W_MIX), BF16)] * 2,
        compiler_params=_params("parallel", "arbitrary"),
        name="fox_prompt",
    )(proj, proj, proj, cum, cum_t)


def _rotary(x, cos, sin_signed):
    half = HEAD_DIM // 2
    first = (_iota((1, W_MIX), 1) % HEAD_DIM) < half
    swapped = jnp.where(first, pltpu.roll(x, W_MIX - half, axis=1), pltpu.roll(x, half, axis=1))
    return x * cos + swapped * sin_signed


def _head_mean_matrix():
    same = (_iota((W_MIX, W_MIX), 0) // HEAD_DIM) == (_iota((W_MIX, W_MIX), 1) // HEAD_DIM)
    return jnp.where(same, 1.0 / HEAD_DIM, 0.0).astype(BF16)


def _head_norm_gate(o, gate_logit):
    avg = _head_mean_matrix()
    mu = _dot_split(o, avg)
    dev = o - mu
    var = _dot_split(dev * dev, avg)
    return _silu(gate_logit) * (dev * lax.rsqrt(var + NORM_EPS))


def _ret_prompt_kernel(q_ref, k_ref, v_ref, g_ref, cos_ref, sin_ref, inner_ref, qd_ref, kd_ref,
                       cd_ref, o_ref, s_ref):
    c = pl.program_id(1)

    @pl.when(c == 0)
    def _():
        s_ref[...] = jnp.zeros_like(s_ref)

    cos = cos_ref[...]
    sin = sin_ref[...]
    ch = cos.shape[0]
    same = (_iota((W_MIX, W_MIX), 0) // HEAD_DIM) == (_iota((W_MIX, W_MIX), 1) // HEAD_DIM)
    inner = inner_ref[...].reshape(N_HEADS * ch, ch)
    for s in range(q_ref.shape[0]):
        q = _rotary(q_ref[s], cos, sin)
        k = _rotary(k_ref[s], cos, sin) * ATTN_SCALE
        kb = k.astype(BF16)
        vb = v_ref[s].astype(BF16)
        state = s_ref[s]
        att = _dot_nt(_stack_heads(q), kb) * inner
        o = _dot(q.astype(BF16), state.astype(BF16)) * qd_ref[...]
        o = o + _unstack_heads(_dot(att.astype(BF16), vb), ch)
        kv = _dot_tn((k * kd_ref[...]).astype(BF16), vb)
        s_ref[s] = state * cd_ref[...] + jnp.where(same, kv, 0.0)
        o_ref[s] = _head_norm_gate(o, g_ref[s]).astype(o_ref.dtype)


def ret_prompt(proj, tabs, b, t):
    ch = RET_CHUNK
    nc = t // ch
    nb = max(n for n in (8, 4, 2, 1) if b % n == 0)
    proj3 = proj.reshape(b, t, proj.shape[1])

    def col(cb):
        return pl.BlockSpec((nb, ch, W_MIX), lambda bb, c: (bb, c, cb))

    full = lambda shape: pl.BlockSpec(shape, lambda bb, c: (0,) * len(shape))
    out, state = pl.pallas_call(
        _ret_prompt_kernel,
        grid=(b // nb, nc),
        in_specs=[col(COL_RQ), col(COL_RK), col(COL_RV), col(COL_RG),
                  pl.BlockSpec((ch, W_MIX), lambda bb, c: (c, 0)),
                  pl.BlockSpec((ch, W_MIX), lambda bb, c: (c, 0)),
                  full((N_HEADS, ch, ch)), full((ch, W_MIX)), full((ch, W_MIX)), full((1, W_MIX))],
        out_specs=[pl.BlockSpec((nb, ch, W_MIX), lambda bb, c: (bb, c, 0)),
                   pl.BlockSpec((nb, W_MIX, W_MIX), lambda bb, c: (bb, 0, 0))],
        out_shape=[jax.ShapeDtypeStruct((b, t, W_MIX), BF16),
                   jax.ShapeDtypeStruct((b, W_MIX, W_MIX), F32)],
        compiler_params=_params("parallel", "arbitrary"),
        name="ret_prompt",
    )(proj3, proj3, proj3, proj3, tabs["cos"], tabs["sin"], tabs["inner"], tabs["q_decay"],
      tabs["k_decay"], tabs["chunk_decay"])
    return out.reshape(b * t, W_MIX), state


def _ret_sample_kernel(gam_ref, q_ref, k_ref, v_ref, g_ref, cos_ref, sin_ref, s_ref,
                       o_ref, sn_ref, qt_scr, kt_scr, vt_scr, ot_scr):
    h = pl.program_id(0)
    rows = q_ref.shape[0]

    @pl.when(h == 0)
    def _():
        cos = cos_ref[...]
        sin = sin_ref[...]
        qt_scr[...] = jnp.transpose(_rotary(q_ref[...], cos, sin))
        kt_scr[...] = jnp.transpose(_rotary(k_ref[...], cos, sin) * ATTN_SCALE)
        vt_scr[...] = jnp.transpose(v_ref[...])

    row0 = pl.multiple_of(h * HEAD_DIM, HEAD_DIM)
    gamma = gam_ref[h]
    vh = vt_scr[pl.ds(row0, HEAD_DIM), :]

    def body(d, cross):
        qd = qt_scr[pl.ds(row0 + d, 1), :]
        kd = kt_scr[pl.ds(row0 + d, 1), :]
        s_d = s_ref[d]
        sn_ref[d] = gamma * s_d + kd * vh
        return cross + qd * s_d

    cross = lax.fori_loop(0, HEAD_DIM, body, jnp.zeros((HEAD_DIM, rows), F32))
    qk = jnp.sum(qt_scr[pl.ds(row0, HEAD_DIM), :] * kt_scr[pl.ds(row0, HEAD_DIM), :],
                 axis=0, keepdims=True)
    ot_scr[pl.ds(row0, HEAD_DIM), :] = qk * vh + cross * gamma

    @pl.when(h == N_HEADS - 1)
    def _():
        o_ref[...] = _head_norm_gate(jnp.transpose(ot_scr[...]), g_ref[...])


def ret_sample(proj, state_t, layer, tabs):
    rows = proj.shape[0]

    def col(cb):
        return pl.BlockSpec((rows, W_MIX), lambda h: (0, cb))

    row_tab = pl.BlockSpec((1, W_MIX), lambda h: (0, 0))
    state_blk = (None, HEAD_DIM, HEAD_DIM, rows)
    return pl.pallas_call(
        _ret_sample_kernel,
        grid=(N_HEADS,),
        in_specs=[pl.BlockSpec(memory_space=pltpu.SMEM),
                  col(COL_RQ), col(COL_RK), col(COL_RV), col(COL_RG), row_tab, row_tab,
                  pl.BlockSpec(state_blk, lambda h: (layer * N_HEADS + h, 0, 0, 0))],
        out_specs=[pl.BlockSpec((rows, W_MIX), lambda h: (0, 0)),
                   pl.BlockSpec(state_blk, lambda h: (h, 0, 0, 0))],
        out_shape=[jax.ShapeDtypeStruct((rows, W_MIX), F32),
                   jax.ShapeDtypeStruct((N_HEADS, HEAD_DIM, HEAD_DIM, rows), F32)],
        scratch_shapes=[pltpu.VMEM((W_MIX, rows), F32)] * 4,
        compiler_params=_params("arbitrary"),
        name="ret_sample",
    )(tabs["gamma"], proj, proj, proj, proj, tabs["cos"], tabs["sin"], state_t)


def _suffix_sums(x3, upper):
    n_pages, rows, page = x3.shape
    flat = x3.reshape(n_pages * rows, page)
    within = _dot_split(flat, upper, parts=3)
    total = (within[:, 0:1] + flat[:, 0:1]).reshape(n_pages, rows, 1)
    within = within.reshape(n_pages, rows, page)
    carry = jnp.zeros((rows, 1), F32)
    outs = [None] * n_pages
    for j in range(n_pages - 1, -1, -1):
        outs[j] = within[j] + carry
        carry = carry + total[j]
    return jnp.stack(outs)


def _rows_to_col(row):
    pick = _iota((SUBLANES, W_MIX), 0) == _iota((SUBLANES, W_MIX), 1)
    return jnp.sum(jnp.where(pick, row, 0.0), axis=1, keepdims=True)


def _lane_bcast_col(row):
    return jnp.transpose(jnp.broadcast_to(row, (LANES, W_MIX)))


def _col_to_row(col):
    return jnp.transpose(jnp.broadcast_to(col, (W_MIX, LANES)))[0:1, :]


def _head_scores(kt, qcol):
    page = kt.shape[1]
    prod = kt * qcol
    row = _iota((SUBLANES, page), 0)
    out = jnp.zeros((SUBLANES, page), F32)
    for h in range(N_HEADS):
        part = prod[h * HEAD_DIM:(h + 1) * HEAD_DIM].reshape(HEAD_DIM // SUBLANES, SUBLANES, page)
        tot = jnp.sum(jnp.sum(part, axis=0), axis=0, keepdims=True)
        out = jnp.where(row == h, tot, out)
    return out


def _head_weights(a8):
    page = a8.shape[1]
    return jnp.concatenate([jnp.broadcast_to(a8[h:h + 1, :], (HEAD_DIM, page))
                            for h in range(N_HEADS)], axis=0)


def _decode_kernel(pt_ref, qs_ref, qf_ref, kf_ref, vf_ref, lg_ref, bias_ref, *refs, n_pages):
    sb_pages = refs[0:n_pages]
    fx_pages = refs[n_pages:2 * n_pages]
    lf_pages = refs[2 * n_pages:3 * n_pages]
    sb_o, fx_o, lf_o, lf_scr = refs[3 * n_pages:]
    page = sb_pages[0].shape[2]
    head_rows = _iota((SUBLANES, W_MIX), 0) == (_iota((SUBLANES, W_MIX), 1) // HEAD_DIM)
    upper = (_iota((page, page), 0) > _iota((page, page), 1)).astype(BF16)

    def per_head_col(row):
        return jnp.sum(jnp.where(head_rows, row, 0.0), axis=1, keepdims=True)

    def per_head_lanes(col):
        return jnp.sum(jnp.where(head_rows, col, 0.0), axis=0, keepdims=True)

    qs_col = _lane_bcast_col(qs_ref[...] * ATTN_SCALE)
    z = jnp.stack([_head_scores(sb_pages[j][0], qs_col) for j in range(n_pages)])
    sp = _softplus(z)
    after = _suffix_sums(-sp, upper)
    a = jnp.exp(z - sp + after)
    acc = jnp.zeros((W_MIX, page), F32)
    for j in range(n_pages):
        acc = acc + sb_pages[j][1] * _head_weights(a[j])
    sb_o[...] = _col_to_row(jnp.sum(acc, axis=1, keepdims=True))

    logf_new = jax.nn.log_sigmoid(lg_ref[...] + bias_ref[...])
    lf_o[...] = logf_new
    qf_row = qf_ref[...] * ATTN_SCALE
    qf_col = _lane_bcast_col(qf_row)
    lf_scr[...] = jnp.zeros_like(lf_scr)
    for j in range(n_pages):
        lf_scr[j, 0:N_HEADS, :] = lf_pages[j][...]
    decay = _suffix_sums(lf_scr[...], upper)
    s = jnp.stack([_head_scores(fx_pages[j][0], qf_col) for j in range(n_pages)])
    s = s + decay + _rows_to_col(logf_new)
    s_self = per_head_col(qf_row * kf_ref[...])
    m = jnp.max(jnp.max(s, axis=0), axis=1, keepdims=True)
    m = jnp.maximum(m, s_self)
    p = jnp.exp(s - m)
    p_self = jnp.exp(s_self - m)
    l = jnp.sum(jnp.sum(p, axis=0), axis=1, keepdims=True) + p_self
    p = p / l
    acc = jnp.zeros((W_MIX, page), F32)
    for j in range(n_pages):
        acc = acc + fx_pages[j][1] * _head_weights(p[j])
    fx_o[...] = (_col_to_row(jnp.sum(acc, axis=1, keepdims=True))
                 + per_head_lanes(p_self / l) * vf_ref[...])


def decode_attn(page_table, proj, sb_cache, fox_cache, logf_t, bias_row, layer):
    rows, n_pages = page_table.shape
    page = sb_cache.shape[4]
    proj3 = proj.reshape(rows, 1, proj.shape[1])

    def col(cb):
        return pl.BlockSpec((None, 1, W_MIX), lambda b, pt: (b, 0, cb))

    def kv_page(j):
        return pl.BlockSpec((None, None, 2, W_MIX, page),
                            lambda b, pt, j=j: (pt[b * n_pages + j], layer, 0, 0, 0))

    def lf_page(j):
        return pl.BlockSpec((None, None, N_HEADS, page),
                            lambda b, pt, j=j: (pt[b * n_pages + j], layer, 0, 0))

    out_row = pl.BlockSpec((None, 1, W_MIX), lambda b, pt: (b, 0, 0))
    grid_spec = pltpu.PrefetchScalarGridSpec(
        num_scalar_prefetch=1,
        grid=(rows,),
        in_specs=[col(COL_SBQ), col(COL_FQ), col(COL_FK), col(COL_FV), col(COL_LOGIT),
                  pl.BlockSpec((1, W_MIX), lambda b, pt: (0, 0))]
                 + [kv_page(j) for j in range(n_pages)]
                 + [kv_page(j) for j in range(n_pages)]
                 + [lf_page(j) for j in range(n_pages)],
        out_specs=[out_row, out_row, out_row],
        scratch_shapes=[pltpu.VMEM((n_pages, SUBLANES, page), F32)],
    )
    sb_o, fx_o, lf_o = pl.pallas_call(
        functools.partial(_decode_kernel, n_pages=n_pages),
        grid_spec=grid_spec,
        out_shape=[jax.ShapeDtypeStruct((rows, 1, W_MIX), F32)] * 3,
        compiler_params=_params("arbitrary"),
        name="decode_attn",
    )(page_table.reshape(-1), proj3, proj3, proj3, proj3, proj3, bias_row,
      *([sb_cache] * n_pages), *([fox_cache] * n_pages), *([logf_t] * n_pages))
    return sb_o.reshape(rows, W_MIX), fx_o.reshape(rows, W_MIX), lf_o.reshape(rows, W_MIX)


def _merge_kernel(b0_ref, b1_ref, b2_ref, b3_ref, h_ref, *refs, precise):
    n_w = 6 if precise else 3
    w_refs, (x_ref, gate_ref, o_ref) = refs[:n_w], refs[n_w:]
    wg_ref, wb_ref, wo_ref = w_refs[:3]
    d = x_ref.shape[1]
    h = h_ref[...]
    merged = None
    for i, br in enumerate((b0_ref, b1_ref, b2_ref, b3_ref)):
        cols = slice(i * d, (i + 1) * d)
        if precise:
            gate_logit = _dot3(h, wg_ref[:, cols], w_refs[3][:, cols])
            mixed = _dot3(br[...].astype(F32), wb_ref[i], w_refs[4][i])
        else:
            gate_logit = _dot(h, wg_ref[:, cols])
            mixed = _dot(br[...].astype(BF16), wb_ref[i])
        y = jax.nn.sigmoid(gate_logit) * mixed
        merged = y if merged is None else merged + y
    if precise:
        out = _dot3(merged, wo_ref[...], w_refs[5][...])
    else:
        out = _dot(merged.astype(BF16), wo_ref[...])
    o_ref[...] = x_ref[...] + gate_ref[...] * out


def merge_out(branches, h, weights, layer, x, mod, tm):
    m, d = x.shape
    precise = len(weights) == 6
    br_spec = pl.BlockSpec((tm, W_MIX), lambda i: (i, 0))
    w_specs = [pl.BlockSpec((None, d, N_GATE), lambda i: (layer, 0, 0)),
               pl.BlockSpec((None, N_BRANCH, W_MIX, d), lambda i: (layer, 0, 0, 0)),
               pl.BlockSpec((None, d, d), lambda i: (layer, 0, 0))]
    return pl.pallas_call(
        functools.partial(_merge_kernel, precise=precise),
        grid=(m // tm,),
        in_specs=[br_spec] * N_BRANCH
                 + [pl.BlockSpec((tm, d), lambda i: (i, 0))]
                 + w_specs * (2 if precise else 1)
                 + [pl.BlockSpec((tm, d), lambda i: (i, 0)), mod.spec(2)],
        out_specs=pl.BlockSpec((tm, d), lambda i: (i, 0)),
        out_shape=jax.ShapeDtypeStruct((m, d), F32),
        compiler_params=_params("parallel"),
        name="merge_out",
    )(*branches, h, *weights, x, mod.arr)


def _route(scores, biased):
    lane_i = _iota(scores.shape, 1)
    real = lane_i < N_EXPERTS
    pos = lane_i % EXPERTS_PER_GROUP
    lane = lane_i.astype(F32)
    group = (lane_i // EXPERTS_PER_GROUP).astype(F32)
    neg = -jnp.inf
    n = scores.shape[1]
    mates = [biased]
    for k in range(1, EXPERTS_PER_GROUP):
        fwd = pltpu.roll(biased, n - k, axis=1)
        back = pltpu.roll(biased, EXPERTS_PER_GROUP - k, axis=1)
        mates.append(jnp.where(pos + k < EXPERTS_PER_GROUP, fwd, back))
    group_score = None
    for a in range(EXPERTS_PER_GROUP):
        for b in range(a + 1, EXPERTS_PER_GROUP):
            pair = mates[a] + mates[b]
            group_score = pair if group_score is None else jnp.maximum(group_score, pair)
    group_score = jnp.where(real, group_score, neg)
    best_score = jnp.max(group_score, axis=1, keepdims=True)
    best = jnp.min(jnp.where(group_score == best_score, group, float(n)), axis=1, keepdims=True)
    cand = jnp.where(real & (group == best), biased, neg)
    top0 = jnp.max(cand, axis=1, keepdims=True)
    idx0 = jnp.min(jnp.where(cand == top0, lane, float(n)), axis=1, keepdims=True)
    cand = jnp.where(lane == idx0, neg, cand)
    top1 = jnp.max(cand, axis=1, keepdims=True)
    idx1 = jnp.min(jnp.where(cand == top1, lane, float(n)), axis=1, keepdims=True)
    chosen = (lane == idx0) | (lane == idx1)
    sel = jnp.where(chosen, scores, 0.0)
    return sel / jnp.sum(sel, axis=1, keepdims=True), best


def _moe_kernel(x_ref, g_ref, sh_ref, sc_ref, gate_ref, rw_ref, rb_ref, w1_ref, w3_ref, w2_ref,
                gf_ref, o_ref, w1b_ref, w3b_ref, w2b_ref, h_scr, comb_scr, acc_scr, *, final_norm):
    e = pl.program_id(1)

    @pl.when(e == 0)
    def _():
        h = _norm_mod(x_ref[...], g_ref[...], sc_ref[...], sh_ref[...])
        h_scr[...] = h
        scores = jax.nn.sigmoid(_dot3(h, *_split2(rw_ref[...])))
        comb_scr[...] = _route(scores, scores + rb_ref[...])[0]
        acc_scr[...] = jnp.zeros_like(acc_scr)

    h = h_scr[...]
    w1_hi, w1_lo = _split2(w1_ref[...])
    w3_hi, w3_lo = _split2(w3_ref[...])
    w2_hi, w2_lo = _split2(w2_ref[...])
    w1b_ref[...] = w1_hi
    w3b_ref[...] = w3_hi
    w2b_ref[...] = w2_hi
    a = _dot3(h, w1_hi, w1_lo)
    b = _dot3(h, w3_hi, w3_lo)
    y = _dot3(_silu(a) * b, w2_hi, w2_lo)
    lane = _iota(comb_scr.shape, 1)
    w_e = jnp.sum(jnp.where(lane == e, comb_scr[...], 0.0), axis=1, keepdims=True)
    acc_scr[...] = acc_scr[...] + w_e * y

    @pl.when(e == N_EXPERTS - 1)
    def _():
        out = x_ref[...] + gate_ref[...] * acc_scr[...]
        if final_norm:
            ms = jnp.mean(out * out, axis=-1, keepdims=True)
            out = out * lax.rsqrt(ms + NORM_EPS) * gf_ref[...]
        o_ref[...] = out


def moe(x, g, mod, router_w_pad, router_b_pad, w1, w3, w2, layer, tm, final_gain=None):
    m, d = x.shape
    assert m == tm
    f = w1.shape[-1]
    n_e = w1.shape[1]
    expert = lambda i, e: (layer, e, 0, 0)
    per_layer = lambda i, e: (e, 0, 0)
    final_norm = final_gain is not None
    gf = (final_gain if final_norm else g).reshape(1, d)
    return pl.pallas_call(
        functools.partial(_moe_kernel, final_norm=final_norm),
        grid=(m // tm, N_EXPERTS),
        in_specs=[pl.BlockSpec((tm, d), lambda i, e: (i, 0)),
                  pl.BlockSpec((1, d), lambda i, e: (0, 0)),
                  mod.spec(3), mod.spec(4), mod.spec(5),
                  pl.BlockSpec((d, LANES), lambda i, e: (0, 0)),
                  pl.BlockSpec((1, LANES), lambda i, e: (0, 0)),
                  pl.BlockSpec((None, None, d, f), expert),
                  pl.BlockSpec((None, None, d, f), expert),
                  pl.BlockSpec((None, None, f, d), expert),
                  pl.BlockSpec((1, d), lambda i, e: (0, 0))],
        out_specs=[pl.BlockSpec((tm, d), lambda i, e: (i, 0)),
                   pl.BlockSpec((None, d, f), per_layer),
                   pl.BlockSpec((None, d, f), per_layer),
                   pl.BlockSpec((None, f, d), per_layer)],
        out_shape=[jax.ShapeDtypeStruct((m, d), F32),
                   jax.ShapeDtypeStruct((n_e, d, f), BF16),
                   jax.ShapeDtypeStruct((n_e, d, f), BF16),
                   jax.ShapeDtypeStruct((n_e, f, d), BF16)],
        scratch_shapes=[pltpu.VMEM((tm, d), F32), pltpu.VMEM((tm, LANES), F32),
                        pltpu.VMEM((tm, d), F32)],
        compiler_params=_params("arbitrary", "arbitrary"),
        name="moe",
    )(x, g.reshape(1, d), mod.arr, mod.arr, mod.arr, router_w_pad, router_b_pad, w1, w3, w2, gf)


_MOE_SUB = 128
_MOE_EXPERTS_PER_STEP = 2
N_GROUPS = N_EXPERTS // EXPERTS_PER_GROUP


def _moe_grouped_kernel(x_ref, g_ref, sh_ref, sc_ref, gate_ref, rw_ref, rb_ref, w1_ref, w3_ref,
                        w2_ref, gf_ref, o_ref, hs_scr, ys_scr, cs_scr, pt_scr, meta_ref, *,
                        final_norm):
    step = pl.program_id(1)
    per_step = w1_ref.shape[0]
    tm = x_ref.shape[0]
    slots = hs_scr.shape[0]
    sub = _MOE_SUB

    @pl.when(step == 0)
    def _():
        h = _norm_mod(x_ref[...], g_ref[...], sc_ref[...], sh_ref[...])
        hb = h.astype(BF16)
        scores = jax.nn.sigmoid(_dot(hb, rw_ref[...].astype(BF16)))
        comb, best = _route(scores, scores + rb_ref[...])
        lane = _iota((tm, LANES), 1).astype(F32)
        member = jnp.where(lane == best, 1.0, 0.0)
        before = (_iota((tm, tm), 1) < _iota((tm, tm), 0)).astype(BF16)
        rank = _dot(before, member.astype(BF16))
        count = rank[tm - 1:tm, :] + member[tm - 1:tm, :]
        room = jnp.floor((count + (sub - 1)) * (1.0 / sub)) * sub
        lane_row = _iota((1, LANES), 1)
        start = jnp.zeros((1, LANES), F32)
        for k in range(1, N_GROUPS):
            start = start + jnp.where(lane_row >= k, pltpu.roll(room, k, axis=1), 0.0)
        dest = jnp.sum(member * (start + rank), axis=1, keepdims=True)
        place = jnp.where(_iota((tm, slots), 1).astype(F32) == dest, 1.0, 0.0).astype(BF16)
        pt_scr[...] = place
        hs_scr[...] = _dot_tn(place, hb).astype(BF16)
        c_hi, c_lo = _split2(comb)
        cs_scr[...] = _dot_tn(place, c_hi) + _dot_tn(place, c_lo)
        ys_scr[...] = jnp.zeros_like(ys_scr)
        for k in range(N_GROUPS):
            pick = lane_row == k
            meta_ref[k] = jnp.sum(jnp.where(pick, start, 0.0)).astype(jnp.int32)
            meta_ref[N_GROUPS + k] = jnp.sum(jnp.where(pick, room, 0.0)).astype(jnp.int32) // sub

    group = (step * per_step) // EXPERTS_PER_GROUP
    first = meta_ref[group]
    lane = _iota((sub, LANES), 1)

    def body(s, carry):
        r0 = pl.multiple_of(first + s * sub, sub)
        rows = hs_scr[pl.ds(r0, sub), :]
        comb = cs_scr[pl.ds(r0, sub), :]
        acc = ys_scr[pl.ds(r0, sub), :]
        for k in range(per_step):
            a = _dot(rows, w1_ref[k])
            b = _dot(rows, w3_ref[k])
            y = _dot((_silu(a) * b).astype(BF16), w2_ref[k])
            w_e = jnp.sum(jnp.where(lane == step * per_step + k, comb, 0.0), axis=1, keepdims=True)
            acc = acc + w_e * y
        ys_scr[pl.ds(r0, sub), :] = acc
        return carry

    lax.fori_loop(0, meta_ref[N_GROUPS + group], body, 0)

    @pl.when(step == pl.num_programs(1) - 1)
    def _():
        y_hi, y_lo = _split2(ys_scr[...])
        place = pt_scr[...]
        out = x_ref[...] + gate_ref[...] * (_dot(place, y_hi) + _dot(place, y_lo))
        if final_norm:
            ms = jnp.mean(out * out, axis=-1, keepdims=True)
            out = out * lax.rsqrt(ms + NORM_EPS) * gf_ref[...]
        o_ref[...] = out


def moe_grouped(x, g, mod, router_w_pad, router_b_pad, w1, w3, w2, tm, final_gain=None):
    m, d = x.shape
    f = w1.shape[-1]
    slots = tm + N_GROUPS * _MOE_SUB
    per_step = _MOE_EXPERTS_PER_STEP
    assert EXPERTS_PER_GROUP % per_step == 0
    expert = lambda i, e: (e, 0, 0)
    final_norm = final_gain is not None
    gf = (final_gain if final_norm else g).reshape(1, d)
    return pl.pallas_call(
        functools.partial(_moe_grouped_kernel, final_norm=final_norm),
        grid=(m // tm, N_EXPERTS // per_step),
        in_specs=[pl.BlockSpec((tm, d), lambda i, e: (i, 0)),
                  pl.BlockSpec((1, d), lambda i, e: (0, 0)),
                  mod.spec(3), mod.spec(4), mod.spec(5),
                  pl.BlockSpec((d, LANES), lambda i, e: (0, 0)),
                  pl.BlockSpec((1, LANES), lambda i, e: (0, 0)),
                  pl.BlockSpec((per_step, d, f), expert),
                  pl.BlockSpec((per_step, d, f), expert),
                  pl.BlockSpec((per_step, f, d), expert),
                  pl.BlockSpec((1, d), lambda i, e: (0, 0))],
        out_specs=pl.BlockSpec((tm, d), lambda i, e: (i, 0)),
        out_shape=jax.ShapeDtypeStruct((m, d), F32),
        scratch_shapes=[pltpu.VMEM((slots, d), BF16), pltpu.VMEM((slots, d), F32),
                        pltpu.VMEM((slots, LANES), F32), pltpu.VMEM((tm, slots), BF16),
                        pltpu.SMEM((2 * N_GROUPS,), jnp.int32)],
        compiler_params=_params("parallel", "arbitrary"),
        name="moe_grouped",
    )(x, g.reshape(1, d), mod.arr, mod.arr, mod.arr, router_w_pad, router_b_pad, w1, w3, w2, gf)


def _rope_tables(pos):
    half = HEAD_DIM // 2
    inv = ROPE_BASE ** (-jnp.arange(half, dtype=F32) / half)
    ang = pos.astype(F32)[:, None] * inv[None, :]
    cos, sin = jnp.cos(ang), jnp.sin(ang)
    cos_row = jnp.tile(jnp.concatenate([cos, cos], axis=1), (1, N_HEADS))
    sin_row = jnp.tile(jnp.concatenate([-sin, sin], axis=1), (1, N_HEADS))
    return cos_row, sin_row


def _retention_tables(pos, chunk):
    lg = jnp.log1p(-(2.0 ** (-5.0 - jnp.arange(N_HEADS, dtype=F32))))
    idx = jnp.arange(chunk, dtype=F32)
    diff = idx[:, None] - idx[None, :]
    inner = jnp.where(diff >= 0, jnp.exp(lg[:, None, None] * jnp.maximum(diff, 0.0)), 0.0)
    q_decay = jnp.exp(lg[None, :] * (idx[:, None] + 1.0))
    k_decay = jnp.exp(lg[None, :] * (chunk - 1.0 - idx[:, None]))
    chunk_decay = jnp.exp(lg * chunk)
    lanes = lambda a: jnp.repeat(a, HEAD_DIM, axis=-1)
    cos, sin = _rope_tables(pos)
    return {"cos": cos, "sin": sin, "inner": inner, "q_decay": lanes(q_decay),
            "k_decay": lanes(k_decay), "chunk_decay": lanes(chunk_decay[None, :]),
            "gamma": chunk_decay}


def _pack_w_kernel(w_ref, wm_ref, wg_ref, wm_lo_ref, wg_lo_ref):
    rows = w_ref.shape[0]

    def put(hi_ref, lo_ref, cols, val):
        hi, lo = _split2(val)
        hi_ref[:, cols] = hi
        lo_ref[:, cols] = lo

    put(wm_ref, wm_lo_ref, slice(0, N_MAIN), w_ref[:, 0:N_MAIN])
    lane = _iota((rows, W_MIX), 1)
    put(wm_ref, wm_lo_ref, slice(N_MAIN, GATE_COL0),
        jnp.where(lane < N_HEADS, w_ref[:, N_MAIN:GATE_COL0], 0.0))
    step = 1024
    for c in range(0, N_GATE, step):
        width = min(step + LANES, N_GATE + N_HEADS - c)
        win = w_ref[:, N_MAIN + c:N_MAIN + c + width]
        put(wg_ref, wg_lo_ref, slice(c, c + step), win[:, N_HEADS:N_HEADS + step])


def pack_w_in(w_in):
    depth, d, n_in = w_in.shape
    assert n_in == N_MAIN + N_HEADS + N_GATE
    tr = 256
    main = pl.BlockSpec((None, tr, GATE_COL0), lambda l, r: (l, r, 0))
    gate = pl.BlockSpec((None, tr, N_GATE), lambda l, r: (l, r, 0))
    main_shape = jax.ShapeDtypeStruct((depth, d, GATE_COL0), BF16)
    gate_shape = jax.ShapeDtypeStruct((depth, d, N_GATE), BF16)
    return pl.pallas_call(
        _pack_w_kernel,
        grid=(depth, d // tr),
        in_specs=[pl.BlockSpec((None, tr, n_in), lambda l, r: (l, r, 0))],
        out_specs=[main, gate, main, gate],
        out_shape=[main_shape, gate_shape, main_shape, gate_shape],
        compiler_params=_params("parallel", "parallel"),
        name="pack_w_in",
    )(w_in)


def _split_w_kernel(w_ref, hi_ref, lo_ref):
    hi, lo = _split2(w_ref[...])
    hi_ref[...] = hi
    lo_ref[...] = lo


def split_weights(w):
    cols = w.shape[-1]
    flat = w.reshape(-1, cols)
    tr = 256
    spec = pl.BlockSpec((tr, cols), lambda i: (i, 0))
    shape = jax.ShapeDtypeStruct(flat.shape, BF16)
    hi, lo = pl.pallas_call(
        _split_w_kernel,
        grid=(flat.shape[0] // tr,),
        in_specs=[spec],
        out_specs=[spec, spec],
        out_shape=[shape, shape],
        compiler_params=_params("parallel"),
        name="split_weights",
    )(flat)
    return hi.reshape(w.shape), lo.reshape(w.shape)


def _block_diag_pool(w_pool_l):
    g, c, _ = w_pool_l.shape
    out = jnp.zeros((g * c, g * c), w_pool_l.dtype)
    for i in range(g):
        out = out.at[i * c:(i + 1) * c, i * c:(i + 1) * c].set(w_pool_l[i])
    return out


def _pad_lanes(row, n):
    return jnp.pad(row, ((0, 0), (0, n - row.shape[1])))


def _diag_blocks(s_bd):
    return jnp.stack([s_bd[:, h * HEAD_DIM:(h + 1) * HEAD_DIM, h * HEAD_DIM:(h + 1) * HEAD_DIM]
                      for h in range(N_HEADS)], axis=1)


def kernel(x_prompt, x_sample, cache_sb_kv, cache_fox_kv, cache_fox_logf, state_pool, state_ret,
           page_table, c_prompt, c_sample, w_ada, b_ada, norm_mix, w_in, w_pool, pool_scale,
           fox_bias, w_branch, w_out, norm_ffn, router_w, router_b, w1, w3, w2, norm_final):
    bp, tp, d = x_prompt.shape
    db, ts, _ = x_sample.shape
    assert ts == 1 and d == D_MODEL
    depth = w_in.shape[0]
    n_phys, _, _, page, _, _ = cache_sb_kv.shape
    n_pages = page_table.shape[1]
    past_len = n_pages * page
    mp = bp * tp

    mod_all = adaln(jnp.concatenate([c_prompt, c_sample], axis=0), w_ada, b_ada)

    sb_cache = jnp.transpose(cache_sb_kv, (0, 1, 2, 4, 5, 3)).reshape(n_phys, depth, 2, W_MIX, page)
    fox_cache = jnp.transpose(cache_fox_kv, (0, 1, 2, 4, 5, 3)).reshape(n_phys, depth, 2, W_MIX, page)
    logf_t = jnp.swapaxes(cache_fox_logf, 2, 3)
    state_t = jnp.transpose(state_ret, (1, 2, 3, 4, 0)).reshape(depth * N_HEADS, HEAD_DIM, HEAD_DIM, db)
    w_main, w_gate, w_main_lo, w_gate_lo = pack_w_in(w_in)
    router_w_pad = _pad_lanes(router_w, LANES)
    router_b_pad = _pad_lanes(router_b[None, :], LANES)
    tabs_p = _retention_tables(jnp.arange(tp), RET_CHUNK if tp % RET_CHUNK == 0 else tp)
    tabs_s = _retention_tables(past_len + jnp.arange(ts), ts)
    wb, wb_lo = split_weights(w_branch)
    wo, wo_lo = split_weights(w_out)
    w_merge_p = (w_gate, wb, wo)
    w_merge_s = w_merge_p + (w_gate_lo, wb_lo, wo_lo)

    tm_p = min(1024, tp)
    tm_mrg = min(512, tp)
    xp = x_prompt.reshape(mp, d)
    xs = x_sample.reshape(db, d)
    outs_p = {k: [] for k in ("logf", "pool", "ret")}
    outs_s = {k: [] for k in ("sb", "fox", "logf", "pool", "ret")}
    kv_prev = None

    for l in range(depth):
        w_bd_f32 = _block_diag_pool(w_pool[l])
        w_bd = w_bd_f32.astype(BF16)
        scale_row = pool_scale[l][None, :]
        bias_lanes = _pad_lanes(fox_bias[l][None, :], LANES)
        bias_row = _pad_lanes(fox_bias[l][None, :], W_MIX)

        final_gain = norm_final if l == depth - 1 else None

        mod_s = Mod(mod_all[l, bp:], 1, db)
        proj_s, h_s = in_proj(xs, norm_mix[l], mod_s, w_main, l, db, w_lo=w_main_lo)
        buf_t = jnp.swapaxes(state_pool[:, l], 0, 1)
        pool_s = pool_sample(buf_t, proj_s, w_bd_f32, scale_row, past_len)
        sb_s, fox_s, logf_s = decode_attn(page_table, proj_s, sb_cache, fox_cache, logf_t, bias_row, l)
        ret_s, state_new = ret_sample(proj_s, state_t, l, tabs_s)
        xs = merge_out((pool_s, sb_s, ret_s, fox_s), h_s, w_merge_s, l, xs, mod_s, db)
        xs, w1b, w3b, w2b = moe(xs, norm_ffn[l], mod_s, router_w_pad, router_b_pad, w1, w3, w2, l,
                                db, final_gain)

        mod_in = Mod(mod_all[l, :bp], tp, tm_p)
        mod_mrg = Mod(mod_all[l, :bp], tp, tm_mrg)
        kv_sb, kv_fox, proj, h = in_proj(xp, norm_mix[l], mod_in, w_main, l, tm_p,
                                         kv_seq=(bp, tp), kv_prev=kv_prev)
        kv_prev = (kv_sb, kv_fox)
        pool_o = pool_prompt(proj, w_bd, scale_row, bp, tp)
        sb_o = sb_prompt(proj, bp, tp)
        ret_o, ret_state = ret_prompt(proj, tabs_p, bp, tp)
        logf, cum, cum_t = logf_cum_prompt(proj, bias_lanes, bp, tp)
        fox_o = fox_prompt(proj, cum, cum_t, bp, tp)
        xp = merge_out((pool_o, sb_o, ret_o, fox_o), h, w_merge_p, l, xp, mod_mrg, tm_mrg)
        xp = moe_grouped(xp, norm_ffn[l], mod_in, router_w_pad, router_b_pad, w1b, w3b, w2b, tm_p,
                         final_gain)
        p3 = proj.reshape(bp, tp, GATE_COL0)
        outs_p["logf"].append(logf.reshape(bp, tp, LANES)[:, :, :N_HEADS])
        outs_p["pool"].append(p3[:, tp - POOL_BUF:, :W_MIX])
        outs_p["ret"].append(_diag_blocks(ret_state))

        def kv_s(cb, proj_s=proj_s):
            return jnp.stack([proj_s[:, cb * W_MIX:(cb + 1) * W_MIX],
                              proj_s[:, (cb + 1) * W_MIX:(cb + 2) * W_MIX]], axis=1)

        outs_s["sb"].append(kv_s(COL_SBK))
        outs_s["fox"].append(kv_s(COL_FK))
        outs_s["logf"].append(logf_s[:, :N_HEADS])
        outs_s["pool"].append(jnp.concatenate([state_pool[:, l, 1:], proj_s[:, None, :W_MIX]], axis=1))
        outs_s["ret"].append(state_new)

    y_p = xp.reshape(bp, tp, d)
    y_s = xs.reshape(db, ts, d)

    def heads(a, t):
        return a.reshape(a.shape[0], depth, 2, t, N_HEADS, HEAD_DIM)

    def from_transposed(kv):
        return heads(jnp.transpose(kv, (2, 0, 1, 4, 3)), tp)

    return (y_p, y_s,
            from_transposed(kv_prev[0]),
            heads(jnp.stack(outs_s["sb"], axis=1), ts),
            from_transposed(kv_prev[1]),
            heads(jnp.stack(outs_s["fox"], axis=1), ts),
            jnp.stack(outs_p["logf"], axis=1),
            jnp.stack(outs_s["logf"], axis=1).reshape(db, depth, ts, N_HEADS),
            jnp.stack(outs_p["pool"], axis=1),
            jnp.stack(outs_s["pool"], axis=1),
            jnp.stack(outs_p["ret"], axis=1),
            jnp.transpose(jnp.stack(outs_s["ret"], axis=0), (4, 0, 1, 2, 3)))
```

```python
import functools

import jax
import jax.numpy as jnp
from jax import lax
from jax.experimental import pallas as pl
from jax.experimental.pallas import tpu as pltpu

F32 = jnp.float32
BF16 = jnp.bfloat16

D_MODEL = 1024
HEAD_DIM = 64
W_MIX = 256
N_HEADS = W_MIX // HEAD_DIM
N_BRANCH = 4
POOL_WINDOWS = (2, 4, 8, 16)
POOL_BUF = 15
RET_CHUNK = 128
ROPE_BASE = 10000.0
N_EXPERTS = 16
EXPERTS_PER_GROUP = 4
NORM_EPS = 1e-6
ATTN_SCALE = HEAD_DIM ** -0.5
LANES = 128
SUBLANES = 8
VMEM_LIMIT = 56 * 1024 * 1024

COL_U, COL_SBQ, COL_SBK, COL_SBV = 0, 1, 2, 3
COL_RQ, COL_RK, COL_RV, COL_RG = 4, 5, 6, 7
COL_FQ, COL_FK, COL_FV, COL_LOGIT = 8, 9, 10, 11
N_MAIN = 11 * W_MIX
GATE_COL0 = 12 * W_MIX
N_GATE = N_BRANCH * D_MODEL


def _params(*sem):
    return pltpu.CompilerParams(dimension_semantics=sem, vmem_limit_bytes=VMEM_LIMIT)


def _split2(x):
    hi = x.astype(BF16)
    lo = (x - hi.astype(F32)).astype(BF16)
    return hi, lo


def _split3(x):
    hi = x.astype(BF16)
    r = x - hi.astype(F32)
    mid = r.astype(BF16)
    lo = (r - mid.astype(F32)).astype(BF16)
    return hi, mid, lo


def _dot(a, b):
    return jnp.dot(a, b, preferred_element_type=F32)


def _dot_nt(a, b):
    return lax.dot_general(a, b, (((1,), (1,)), ((), ())), preferred_element_type=F32)


def _dot_tn(a, b):
    return lax.dot_general(a, b, (((0,), (0,)), ((), ())), preferred_element_type=F32)


def _dot_split(x, w, parts=2):
    ps = _split2(x) if parts == 2 else _split3(x)
    acc = _dot(ps[0], w)
    for p in ps[1:]:
        acc = acc + _dot(p, w)
    return acc


def _dot3(x, w_hi, w_lo):
    x_hi, x_lo = _split2(x)
    return _dot(x_hi, w_hi) + (_dot(x_lo, w_hi) + _dot(x_hi, w_lo))


def _iota(shape, dim):
    return lax.broadcasted_iota(jnp.int32, shape, dim)


def _softplus(z):
    return jnp.maximum(z, 0.0) + jnp.log1p(jnp.exp(-jnp.abs(z)))


def _silu(x):
    return x * jax.nn.sigmoid(x)


def _adaln_kernel(c_ref, w_ref, b_ref, o_ref):
    w_hi, w_lo = _split2(w_ref[...])
    o_ref[...] = _dot3(_silu(c_ref[...]), w_hi, w_lo) + b_ref[...]


def adaln(c_all, w_ada, b_ada):
    rows, d = c_all.shape
    depth, _, n = w_ada.shape
    tn = 1024
    return pl.pallas_call(
        _adaln_kernel,
        grid=(depth, n // tn),
        in_specs=[pl.BlockSpec((rows, d), lambda l, j: (0, 0)),
                  pl.BlockSpec((None, d, tn), lambda l, j: (l, 0, j)),
                  pl.BlockSpec((None, 1, tn), lambda l, j: (l, 0, j))],
        out_specs=pl.BlockSpec((None, rows, tn), lambda l, j: (l, 0, j)),
        out_shape=jax.ShapeDtypeStruct((depth, rows, n), F32),
        compiler_params=_params("parallel", "parallel"),
        name="adaln",
    )(c_all, w_ada, b_ada.reshape(depth, 1, n))


class Mod:
    def __init__(self, arr, rows_per_vec, tm):
        self.per_row = rows_per_vec == 1
        self.arr = arr if self.per_row else arr.reshape(arr.shape[0], 1, arr.shape[-1])
        self.tiles_per_vec = 1 if self.per_row else rows_per_vec // tm
        self.tm = tm

    def spec(self, k):
        if self.per_row:
            return pl.BlockSpec((self.tm, D_MODEL), lambda i, *_: (i, k))
        t = self.tiles_per_vec
        return pl.BlockSpec((None, 1, D_MODEL), lambda i, *_: (i // t, 0, k))


def _norm_mod(x, g, sc, sh):
    ms = jnp.mean(x * x, axis=-1, keepdims=True)
    y = x * lax.rsqrt(ms + NORM_EPS) * g
    return y * (1.0 + sc) + sh


_PROJ_TN = 1536


def _inproj_kernel(x_ref, g_ref, sh_ref, sc_ref, w_ref, *refs, kv_out, precise):
    j = pl.program_id(1)
    o_ref, h_ref = refs[-2], refs[-1]

    @pl.when(j == 0)
    def _():
        h_ref[...] = _norm_mod(x_ref[...], g_ref[...], sc_ref[...], sh_ref[...]).astype(h_ref.dtype)

    if precise:
        res = _dot3(h_ref[...], w_ref[...], refs[0][...])
    else:
        res = _dot(h_ref[...], w_ref[...])
    o_ref[...] = res
    if kv_out:
        per_tile = _PROJ_TN // W_MIX
        for col_k, kv_ref in ((COL_SBK, refs[-4]), (COL_FK, refs[-3])):
            c0 = (col_k % per_tile) * W_MIX

            @pl.when(j == col_k // per_tile)
            def _(c0=c0, kv_ref=kv_ref):
                kv_ref[0] = jnp.transpose(res[:, c0:c0 + W_MIX])
                kv_ref[1] = jnp.transpose(res[:, c0 + W_MIX:c0 + 2 * W_MIX])


def in_proj(x, g, mod, w_all, layer, tm, kv_seq=None, kv_prev=None, w_lo=None):
    m, d = x.shape
    depth, _, n = w_all.shape
    tn = _PROJ_TN
    precise = w_lo is not None
    w_spec = pl.BlockSpec((None, d, tn), lambda i, j: (layer, 0, j))
    in_specs = [pl.BlockSpec((tm, d), lambda i, j: (i, 0)),
                pl.BlockSpec((1, d), lambda i, j: (0, 0)),
                mod.spec(0), mod.spec(1), w_spec]
    args = [x, g.reshape(1, d), mod.arr, mod.arr, w_all]
    if precise:
        in_specs.append(w_spec)
        args.append(w_lo)
    out_specs = [pl.BlockSpec((tm, tn), lambda i, j: (i, j)),
                 pl.BlockSpec((tm, d), lambda i, j: (i, 0))]
    out_shape = [jax.ShapeDtypeStruct((m, n), F32),
                 jax.ShapeDtypeStruct((m, d), F32 if precise else BF16)]
    aliases = {}
    if kv_seq is not None:
        b, t = kv_seq
        per_b = t // tm
        kv_spec = pl.BlockSpec((None, 2, None, W_MIX, tm),
                               lambda i, j: (layer, 0, i // per_b, 0, i % per_b))
        kv_shape = jax.ShapeDtypeStruct((depth, 2, b, W_MIX, t), F32)
        out_specs = [kv_spec, kv_spec] + out_specs
        out_shape = [kv_shape, kv_shape] + out_shape
        if kv_prev is not None:
            aliases = {len(args): 0, len(args) + 1: 1}
            in_specs += [pl.BlockSpec(memory_space=pl.ANY)] * 2
            args += list(kv_prev)
    outs = pl.pallas_call(
        functools.partial(_inproj_kernel, kv_out=kv_seq is not None, precise=precise),
        grid=(m // tm, n // tn),
        in_specs=in_specs,
        out_specs=out_specs,
        out_shape=out_shape,
        input_output_aliases=aliases,
        compiler_params=_params("parallel", "arbitrary"),
        name="in_proj",
    )(*args)
    return outs


def _pool_tail(s2, s4, s8, s16, u, pos0, w_ref, sc_ref, o_ref):
    t = u.shape[0]
    lane = _iota((1, W_MIX), 1) // (W_MIX // len(POOL_WINDOWS))
    win = jnp.where(lane == 0, s2, jnp.where(lane == 1, s4, jnp.where(lane == 2, s8, s16)))
    width = jnp.where(lane == 0, 2, jnp.where(lane == 1, 4, jnp.where(lane == 2, 8, 16)))
    count = jnp.minimum(width, pos0 + 1 + _iota((t, W_MIX), 0)).astype(F32)
    resid = win / count - u
    if w_ref.dtype == F32:
        w_hi, w_lo = _split2(w_ref[...])
        mixed = _dot3(resid, w_hi, w_lo)
    else:
        mixed = _dot(resid.astype(BF16), w_ref[...])
    o_ref[...] = (mixed * sc_ref[...]).astype(o_ref.dtype)


_POOL_PAD = 32


def _pool_prompt_kernel(u_ref, w_ref, sc_ref, o_ref, a_scr, b_scr, c_scr):
    t = u_ref.shape[0]
    p = _POOL_PAD
    u = u_ref[...]
    a_scr[0:p, :] = jnp.zeros((p, W_MIX), F32)
    a_scr[p:p + t, :] = u

    def stage(src, dst, k, lo):
        n = p + t - lo
        dst[lo:lo + n, :] = src[lo:lo + n, :] + src[lo - k:lo - k + n, :]

    stage(a_scr, b_scr, 1, 8)
    stage(b_scr, c_scr, 2, 16)
    stage(c_scr, a_scr, 4, 24)
    s8 = a_scr[p:p + t, :]
    s16 = s8 + a_scr[p - 8:p - 8 + t, :]
    _pool_tail(b_scr[p:p + t, :], c_scr[p:p + t, :], s8, s16, u, 0, w_ref, sc_ref, o_ref)


def pool_prompt(proj, w_bd, scale, b, t):
    return pl.pallas_call(
        _pool_prompt_kernel,
        grid=(b,),
        in_specs=[pl.BlockSpec((t, W_MIX), lambda i: (i, COL_U)),
                  pl.BlockSpec((W_MIX, W_MIX), lambda i: (0, 0)),
                  pl.BlockSpec((1, W_MIX), lambda i: (0, 0))],
        out_specs=pl.BlockSpec((t, W_MIX), lambda i: (i, 0)),
        out_shape=jax.ShapeDtypeStruct((b * t, W_MIX), BF16),
        scratch_shapes=[pltpu.VMEM((t + _POOL_PAD, W_MIX), F32)] * 3,
        compiler_params=_params("parallel"),
        name="pool_prompt",
    )(proj, w_bd, scale)


def _pool_sample_kernel(buf_ref, u_ref, w_ref, sc_ref, o_ref, *, pos0):
    u = u_ref[...]
    s2 = u + buf_ref[14]
    s4 = s2 + buf_ref[13] + buf_ref[12]
    s8 = s4 + buf_ref[11] + buf_ref[10] + buf_ref[9] + buf_ref[8]
    s16 = s8
    for r in range(7, -1, -1):
        s16 = s16 + buf_ref[r]
    _pool_tail(s2, s4, s8, s16, u, pos0, w_ref, sc_ref, o_ref)


def pool_sample(buf_t, proj, w_bd, scale, pos0):
    rows = proj.shape[0]
    return pl.pallas_call(
        functools.partial(_pool_sample_kernel, pos0=pos0),
        grid=(1,),
        in_specs=[pl.BlockSpec((POOL_BUF, rows, W_MIX), lambda i: (0, 0, 0)),
                  pl.BlockSpec((rows, W_MIX), lambda i: (0, COL_U)),
                  pl.BlockSpec((W_MIX, W_MIX), lambda i: (0, 0)),
                  pl.BlockSpec((1, W_MIX), lambda i: (0, 0))],
        out_specs=pl.BlockSpec((rows, W_MIX), lambda i: (0, 0)),
        out_shape=jax.ShapeDtypeStruct((rows, W_MIX), F32),
        compiler_params=_params("arbitrary"),
        name="pool_sample",
    )(buf_t, proj, w_bd, scale)


_ATT_TILE = 256


def _head_of_lane():
    return _iota((1, W_MIX), 1) // HEAD_DIM


def _stack_heads(q):
    head = _head_of_lane()
    return jnp.concatenate([jnp.where(head == h, q, 0.0) for h in range(N_HEADS)], axis=0).astype(BF16)


def _unstack_heads(acc, tq):
    head = _head_of_lane()
    out = jnp.where(head == 0, acc[0:tq], 0.0)
    for h in range(1, N_HEADS):
        out = out + jnp.where(head == h, acc[h * tq:(h + 1) * tq], 0.0)
    return out


def _load_kv_once(k_ref, v_ref, k_scr, v_scr):
    @pl.when(pl.program_id(1) == 0)
    def _():
        k_scr[...] = k_ref[...].astype(BF16)
        v_scr[...] = v_ref[...].astype(BF16)


def _sb_prompt_kernel(q_ref, k_ref, v_ref, o_ref, k_scr, v_scr):
    tq = tk = _ATT_TILE
    i = pl.program_id(1)
    _load_kv_once(k_ref, v_ref, k_scr, v_scr)
    q4 = _stack_heads(q_ref[...] * ATTN_SCALE)
    rows = N_HEADS * tq
    below_diag = _iota((rows, tk), 1) < (_iota((rows, tk), 0) % tq)
    upper = (_iota((tk, tk), 0) > _iota((tk, tk), 1)).astype(BF16)

    def block(off, diagonal, carry):
        acc, run = carry
        kb = k_scr[pl.ds(off, tk), :]
        vb = v_scr[pl.ds(off, tk), :]
        z = _dot_nt(q4, kb)
        sp = jnp.maximum(z, 0.0) + jnp.log(1.0 + jnp.exp(-jnp.abs(z)))
        log_not = jnp.where(below_diag, -sp, 0.0) if diagonal else -sp
        suffix = _dot_split(log_not, upper)
        a = jnp.exp(z - sp + suffix + run)
        if diagonal:
            a = jnp.where(below_diag, a, 0.0)
        return acc + _dot(a.astype(BF16), vb), run + suffix[:, 0:1] + log_not[:, 0:1]

    carry = block(pl.multiple_of(i * tk, tk), True,
                  (jnp.zeros((rows, W_MIX), F32), jnp.zeros((rows, 1), F32)))
    acc, _ = lax.fori_loop(
        0, i, lambda s, c: block(pl.multiple_of((i - 1 - s) * tk, tk), False, c), carry)
    o_ref[...] = _unstack_heads(acc, tq).astype(o_ref.dtype)


def _attn_prompt_specs(t, col_q, col_k, col_v):
    tq = _ATT_TILE
    nq = t // tq
    return [pl.BlockSpec((tq, W_MIX), lambda b, i: (b * nq + i, col_q)),
            pl.BlockSpec((t, W_MIX), lambda b, i: (b, col_k)),
            pl.BlockSpec((t, W_MIX), lambda b, i: (b, col_v))]


def sb_prompt(proj, b, t):
    tq = _ATT_TILE
    nq = t // tq
    return pl.pallas_call(
        _sb_prompt_kernel,
        grid=(b, nq),
        in_specs=_attn_prompt_specs(t, COL_SBQ, COL_SBK, COL_SBV),
        out_specs=pl.BlockSpec((tq, W_MIX), lambda bb, i: (bb * nq + i, 0)),
        out_shape=jax.ShapeDtypeStruct((b * t, W_MIX), BF16),
        scratch_shapes=[pltpu.VMEM((t, W_MIX), BF16)] * 2,
        compiler_params=_params("parallel", "arbitrary"),
        name="sb_prompt",
    )(proj, proj, proj)


def _logf_cum_kernel(x_ref, bias_ref, logf_ref, cum_ref, cumt_ref):
    t = x_ref.shape[0]
    blk = 256
    logf = jax.nn.log_sigmoid(x_ref[...] + bias_ref[...])
    logf_ref[...] = logf
    lower = (_iota((blk, blk), 1) <= _iota((blk, blk), 0)).astype(BF16)
    carry = jnp.zeros((1, LANES), F32)
    for c in range(t // blk):
        part = logf[c * blk:(c + 1) * blk]
        hi, mid, lo = _split3(part)
        cum = _dot(lower, hi) + _dot(lower, mid) + _dot(lower, lo) + carry
        cum_ref[c * blk:(c + 1) * blk, :] = cum
        carry = cum[blk - 1:blk, :]
    cumt_ref[...] = jnp.transpose(cum_ref[...])[0:SUBLANES, :]


def logf_cum_prompt(proj, bias_pad, b, t):
    m = b * t
    return pl.pallas_call(
        _logf_cum_kernel,
        grid=(b,),
        in_specs=[pl.BlockSpec((t, LANES), lambda i: (i, COL_LOGIT * W_MIX // LANES)),
                  pl.BlockSpec((1, LANES), lambda i: (0, 0))],
        out_specs=[pl.BlockSpec((t, LANES), lambda i: (i, 0)),
                   pl.BlockSpec((t, LANES), lambda i: (i, 0)),
                   pl.BlockSpec((None, SUBLANES, t), lambda i: (i, 0, 0))],
        out_shape=[jax.ShapeDtypeStruct((m, LANES), F32),
                   jax.ShapeDtypeStruct((m, LANES), F32),
                   jax.ShapeDtypeStruct((b, SUBLANES, t), F32)],
        compiler_params=_params("parallel"),
        name="logf_cum",
    )(proj, bias_pad)


def _fox_prompt_kernel(q_ref, k_ref, v_ref, cq_ref, ck_ref, o_ref, k_scr, v_scr):
    tq = tk = _ATT_TILE
    i = pl.program_id(1)
    _load_kv_once(k_ref, v_ref, k_scr, v_scr)
    q4 = _stack_heads(q_ref[...] * ATTN_SCALE)
    rows = N_HEADS * tq
    on_or_below_diag = _iota((rows, tk), 1) <= (_iota((rows, tk), 0) % tq)
    cq = cq_ref[...]

    def block(off, diagonal, carry):
        acc, m, l = carry
        kb = k_scr[pl.ds(off, tk), :]
        vb = v_scr[pl.ds(off, tk), :]
        z = _dot_nt(q4, kb)
        sc = jnp.concatenate(
            [z[h * tq:(h + 1) * tq] + (cq[:, h:h + 1] - ck_ref[h:h + 1, pl.ds(off, tk)])
             for h in range(N_HEADS)], axis=0)
        if diagonal:
            sc = jnp.where(on_or_below_diag, sc, -jnp.inf)
        m_new = jnp.maximum(m, jnp.max(sc, axis=1, keepdims=True))
        alpha = jnp.exp(m - m_new)
        p = jnp.exp(sc - m_new)
        return (alpha * acc + _dot(p.astype(BF16), vb), m_new,
                alpha * l + jnp.sum(p, axis=1, keepdims=True))

    carry = block(pl.multiple_of(i * tk, tk), True,
                  (jnp.zeros((rows, W_MIX), F32), jnp.full((rows, 1), -jnp.inf, F32),
                   jnp.zeros((rows, 1), F32)))
    acc, _, l = lax.fori_loop(
        0, i, lambda s, c: block(pl.multiple_of((i - 1 - s) * tk, tk), False, c), carry)
    o_ref[...] = _unstack_heads(acc / l, tq).astype(o_ref.dtype)


def fox_prompt(proj, cum, cum_t, b, t):
    tq = _ATT_TILE
    nq = t // tq
    specs = _attn_prompt_specs(t, COL_FQ, COL_FK, COL_FV)
    specs += [pl.BlockSpec((tq, LANES), lambda bb, i: (bb * nq + i, 0)),
              pl.BlockSpec((None, SUBLANES, t), lambda bb, i: (bb, 0, 0))]
    return pl.pallas_call(
        _fox_prompt_kernel,
        grid=(b, nq),
        in_specs=specs,
        out_specs=pl.BlockSpec((tq, W_MIX), lambda bb, i: (bb * nq + i, 0)),
        out_shape=jax.ShapeDtypeStruct((b * t, W_MIX), BF16),
        scratch_shapes=[pltpu.VMEM((t, W_MIX), BF16)] * 2,
        compiler_params=_params("parallel", "arbitrary"),
        name="fox_prompt",
    )(proj, proj, proj, cum, cum_t)


def _rotary(x, cos, sin_signed):
    half = HEAD_DIM // 2
    first = (_iota((1, W_MIX), 1) % HEAD_DIM) < half
    swapped = jnp.where(first, pltpu.roll(x, W_MIX - half, axis=1), pltpu.roll(x, half, axis=1))
    return x * cos + swapped * sin_signed


def _head_mean_matrix():
    same = (_iota((W_MIX, W_MIX), 0) // HEAD_DIM) == (_iota((W_MIX, W_MIX), 1) // HEAD_DIM)
    return jnp.where(same, 1.0 / HEAD_DIM, 0.0).astype(BF16)


def _head_norm_gate(o, gate_logit):
    avg = _head_mean_matrix()
    mu = _dot_split(o, avg)
    dev = o - mu
    var = _dot_split(dev * dev, avg)
    return _silu(gate_logit) * (dev * lax.rsqrt(var + NORM_EPS))


def _ret_prompt_kernel(q_ref, k_ref, v_ref, g_ref, cos_ref, sin_ref, inner_ref, qd_ref, kd_ref,
                       cd_ref, o_ref, s_ref):
    c = pl.program_id(1)

    @pl.when(c == 0)
    def _():
        s_ref[...] = jnp.zeros_like(s_ref)

    cos = cos_ref[...]
    sin = sin_ref[...]
    ch = cos.shape[0]
    same = (_iota((W_MIX, W_MIX), 0) // HEAD_DIM) == (_iota((W_MIX, W_MIX), 1) // HEAD_DIM)
    inner = inner_ref[...].reshape(N_HEADS * ch, ch)
    for s in range(q_ref.shape[0]):
        q = _rotary(q_ref[s], cos, sin)
        k = _rotary(k_ref[s], cos, sin) * ATTN_SCALE
        kb = k.astype(BF16)
        vb = v_ref[s].astype(BF16)
        state = s_ref[s]
        att = _dot_nt(_stack_heads(q), kb) * inner
        o = _dot(q.astype(BF16), state.astype(BF16)) * qd_ref[...]
        o = o + _unstack_heads(_dot(att.astype(BF16), vb), ch)
        kv = _dot_tn((k * kd_ref[...]).astype(BF16), vb)
        s_ref[s] = state * cd_ref[...] + jnp.where(same, kv, 0.0)
        o_ref[s] = _head_norm_gate(o, g_ref[s]).astype(o_ref.dtype)


def ret_prompt(proj, tabs, b, t):
    ch = RET_CHUNK
    nc = t // ch
    nb = max(n for n in (8, 4, 2, 1) if b % n == 0)
    proj3 = proj.reshape(b, t, proj.shape[1])

    def col(cb):
        return pl.BlockSpec((nb, ch, W_MIX), lambda bb, c: (bb, c, cb))

    full = lambda shape: pl.BlockSpec(shape, lambda bb, c: (0,) * len(shape))
    out, state = pl.pallas_call(
        _ret_prompt_kernel,
        grid=(b // nb, nc),
        in_specs=[col(COL_RQ), col(COL_RK), col(COL_RV), col(COL_RG),
                  pl.BlockSpec((ch, W_MIX), lambda bb, c: (c, 0)),
                  pl.BlockSpec((ch, W_MIX), lambda bb, c: (c, 0)),
                  full((N_HEADS, ch, ch)), full((ch, W_MIX)), full((ch, W_MIX)), full((1, W_MIX))],
        out_specs=[pl.BlockSpec((nb, ch, W_MIX), lambda bb, c: (bb, c, 0)),
                   pl.BlockSpec((nb, W_MIX, W_MIX), lambda bb, c: (bb, 0, 0))],
        out_shape=[jax.ShapeDtypeStruct((b, t, W_MIX), BF16),
                   jax.ShapeDtypeStruct((b, W_MIX, W_MIX), F32)],
        compiler_params=_params("parallel", "arbitrary"),
        name="ret_prompt",
    )(proj3, proj3, proj3, proj3, tabs["cos"], tabs["sin"], tabs["inner"], tabs["q_decay"],
      tabs["k_decay"], tabs["chunk_decay"])
    return out.reshape(b * t, W_MIX), state


def _ret_sample_kernel(gam_ref, q_ref, k_ref, v_ref, g_ref, cos_ref, sin_ref, s_ref,
                       o_ref, sn_ref, qt_scr, kt_scr, vt_scr, ot_scr):
    h = pl.program_id(0)
    rows = q_ref.shape[0]

    @pl.when(h == 0)
    def _():
        cos = cos_ref[...]
        sin = sin_ref[...]
        qt_scr[...] = jnp.transpose(_rotary(q_ref[...], cos, sin))
        kt_scr[...] = jnp.transpose(_rotary(k_ref[...], cos, sin) * ATTN_SCALE)
        vt_scr[...] = jnp.transpose(v_ref[...])

    row0 = pl.multiple_of(h * HEAD_DIM, HEAD_DIM)
    gamma = gam_ref[h]
    vh = vt_scr[pl.ds(row0, HEAD_DIM), :]

    def body(d, cross):
        qd = qt_scr[pl.ds(row0 + d, 1), :]
        kd = kt_scr[pl.ds(row0 + d, 1), :]
        s_d = s_ref[d]
        sn_ref[d] = gamma * s_d + kd * vh
        return cross + qd * s_d

    cross = lax.fori_loop(0, HEAD_DIM, body, jnp.zeros((HEAD_DIM, rows), F32))
    qk = jnp.sum(qt_scr[pl.ds(row0, HEAD_DIM), :] * kt_scr[pl.ds(row0, HEAD_DIM), :],
                 axis=0, keepdims=True)
    ot_scr[pl.ds(row0, HEAD_DIM), :] = qk * vh + cross * gamma

    @pl.when(h == N_HEADS - 1)
    def _():
        o_ref[...] = _head_norm_gate(jnp.transpose(ot_scr[...]), g_ref[...])


def ret_sample(proj, state_t, layer, tabs):
    rows = proj.shape[0]

    def col(cb):
        return pl.BlockSpec((rows, W_MIX), lambda h: (0, cb))

    row_tab = pl.BlockSpec((1, W_MIX), lambda h: (0, 0))
    state_blk = (None, HEAD_DIM, HEAD_DIM, rows)
    return pl.pallas_call(
        _ret_sample_kernel,
        grid=(N_HEADS,),
        in_specs=[pl.BlockSpec(memory_space=pltpu.SMEM),
                  col(COL_RQ), col(COL_RK), col(COL_RV), col(COL_RG), row_tab, row_tab,
                  pl.BlockSpec(state_blk, lambda h: (layer * N_HEADS + h, 0, 0, 0))],
        out_specs=[pl.BlockSpec((rows, W_MIX), lambda h: (0, 0)),
                   pl.BlockSpec(state_blk, lambda h: (h, 0, 0, 0))],
        out_shape=[jax.ShapeDtypeStruct((rows, W_MIX), F32),
                   jax.ShapeDtypeStruct((N_HEADS, HEAD_DIM, HEAD_DIM, rows), F32)],
        scratch_shapes=[pltpu.VMEM((W_MIX, rows), F32)] * 4,
        compiler_params=_params("arbitrary"),
        name="ret_sample",
    )(tabs["gamma"], proj, proj, proj, proj, tabs["cos"], tabs["sin"], state_t)


def _suffix_sums(x3, upper):
    n_pages, rows, page = x3.shape
    flat = x3.reshape(n_pages * rows, page)
    within = _dot_split(flat, upper, parts=3)
    total = (within[:, 0:1] + flat[:, 0:1]).reshape(n_pages, rows, 1)
    within = within.reshape(n_pages, rows, page)
    carry = jnp.zeros((rows, 1), F32)
    outs = [None] * n_pages
    for j in range(n_pages - 1, -1, -1):
        outs[j] = within[j] + carry
        carry = carry + total[j]
    return jnp.stack(outs)


def _rows_to_col(row):
    pick = _iota((SUBLANES, W_MIX), 0) == _iota((SUBLANES, W_MIX), 1)
    return jnp.sum(jnp.where(pick, row, 0.0), axis=1, keepdims=True)


def _lane_bcast_col(row):
    return jnp.transpose(jnp.broadcast_to(row, (LANES, W_MIX)))


def _col_to_row(col):
    return jnp.transpose(jnp.broadcast_to(col, (W_MIX, LANES)))[0:1, :]


def _head_scores(kt, qcol):
    page = kt.shape[1]
    prod = kt * qcol
    row = _iota((SUBLANES, page), 0)
    out = jnp.zeros((SUBLANES, page), F32)
    for h in range(N_HEADS):
        part = prod[h * HEAD_DIM:(h + 1) * HEAD_DIM].reshape(HEAD_DIM // SUBLANES, SUBLANES, page)
        tot = jnp.sum(jnp.sum(part, axis=0), axis=0, keepdims=True)
        out = jnp.where(row == h, tot, out)
    return out


def _head_weights(a8):
    page = a8.shape[1]
    return jnp.concatenate([jnp.broadcast_to(a8[h:h + 1, :], (HEAD_DIM, page))
                            for h in range(N_HEADS)], axis=0)


def _decode_kernel(pt_ref, qs_ref, qf_ref, kf_ref, vf_ref, lg_ref, bias_ref, *refs, n_pages):
    sb_pages = refs[0:n_pages]
    fx_pages = refs[n_pages:2 * n_pages]
    lf_pages = refs[2 * n_pages:3 * n_pages]
    sb_o, fx_o, lf_o, lf_scr = refs[3 * n_pages:]
    page = sb_pages[0].shape[2]
    head_rows = _iota((SUBLANES, W_MIX), 0) == (_iota((SUBLANES, W_MIX), 1) // HEAD_DIM)
    upper = (_iota((page, page), 0) > _iota((page, page), 1)).astype(BF16)

    def per_head_col(row):
        return jnp.sum(jnp.where(head_rows, row, 0.0), axis=1, keepdims=True)

    def per_head_lanes(col):
        return jnp.sum(jnp.where(head_rows, col, 0.0), axis=0, keepdims=True)

    qs_col = _lane_bcast_col(qs_ref[...] * ATTN_SCALE)
    z = jnp.stack([_head_scores(sb_pages[j][0], qs_col) for j in range(n_pages)])
    sp = _softplus(z)
    after = _suffix_sums(-sp, upper)
    a = jnp.exp(z - sp + after)
    acc = jnp.zeros((W_MIX, page), F32)
    for j in range(n_pages):
        acc = acc + sb_pages[j][1] * _head_weights(a[j])
    sb_o[...] = _col_to_row(jnp.sum(acc, axis=1, keepdims=True))

    logf_new = jax.nn.log_sigmoid(lg_ref[...] + bias_ref[...])
    lf_o[...] = logf_new
    qf_row = qf_ref[...] * ATTN_SCALE
    qf_col = _lane_bcast_col(qf_row)
    lf_scr[...] = jnp.zeros_like(lf_scr)
    for j in range(n_pages):
        lf_scr[j, 0:N_HEADS, :] = lf_pages[j][...]
    decay = _suffix_sums(lf_scr[...], upper)
    s = jnp.stack([_head_scores(fx_pages[j][0], qf_col) for j in range(n_pages)])
    s = s + decay + _rows_to_col(logf_new)
    s_self = per_head_col(qf_row * kf_ref[...])
    m = jnp.max(jnp.max(s, axis=0), axis=1, keepdims=True)
    m = jnp.maximum(m, s_self)
    p = jnp.exp(s - m)
    p_self = jnp.exp(s_self - m)
    l = jnp.sum(jnp.sum(p, axis=0), axis=1, keepdims=True) + p_self
    p = p / l
    acc = jnp.zeros((W_MIX, page), F32)
    for j in range(n_pages):
        acc = acc + fx_pages[j][1] * _head_weights(p[j])
    fx_o[...] = (_col_to_row(jnp.sum(acc, axis=1, keepdims=True))
                 + per_head_lanes(p_self / l) * vf_ref[...])


def decode_attn(page_table, proj, sb_cache, fox_cache, logf_t, bias_row, layer):
    rows, n_pages = page_table.shape
    page = sb_cache.shape[4]
    proj3 = proj.reshape(rows, 1, proj.shape[1])

    def col(cb):
        return pl.BlockSpec((None, 1, W_MIX), lambda b, pt: (b, 0, cb))

    def kv_page(j):
        return pl.BlockSpec((None, None, 2, W_MIX, page),
                            lambda b, pt, j=j: (pt[b * n_pages + j], layer, 0, 0, 0))

    def lf_page(j):
        return pl.BlockSpec((None, None, N_HEADS, page),
                            lambda b, pt, j=j: (pt[b * n_pages + j], layer, 0, 0))

    out_row = pl.BlockSpec((None, 1, W_MIX), lambda b, pt: (b, 0, 0))
    grid_spec = pltpu.PrefetchScalarGridSpec(
        num_scalar_prefetch=1,
        grid=(rows,),
        in_specs=[col(COL_SBQ), col(COL_FQ), col(COL_FK), col(COL_FV), col(COL_LOGIT),
                  pl.BlockSpec((1, W_MIX), lambda b, pt: (0, 0))]
                 + [kv_page(j) for j in range(n_pages)]
                 + [kv_page(j) for j in range(n_pages)]
                 + [lf_page(j) for j in range(n_pages)],
        out_specs=[out_row, out_row, out_row],
        scratch_shapes=[pltpu.VMEM((n_pages, SUBLANES, page), F32)],
    )
    sb_o, fx_o, lf_o = pl.pallas_call(
        functools.partial(_decode_kernel, n_pages=n_pages),
        grid_spec=grid_spec,
        out_shape=[jax.ShapeDtypeStruct((rows, 1, W_MIX), F32)] * 3,
        compiler_params=_params("arbitrary"),
        name="decode_attn",
    )(page_table.reshape(-1), proj3, proj3, proj3, proj3, proj3, bias_row,
      *([sb_cache] * n_pages), *([fox_cache] * n_pages), *([logf_t] * n_pages))
    return sb_o.reshape(rows, W_MIX), fx_o.reshape(rows, W_MIX), lf_o.reshape(rows, W_MIX)


def _merge_kernel(b0_ref, b1_ref, b2_ref, b3_ref, h_ref, *refs, precise):
    n_w = 6 if precise else 3
    w_refs, (x_ref, gate_ref, o_ref) = refs[:n_w], refs[n_w:]
    wg_ref, wb_ref, wo_ref = w_refs[:3]
    d = x_ref.shape[1]
    h = h_ref[...]
    merged = None
    for i, br in enumerate((b0_ref, b1_ref, b2_ref, b3_ref)):
        cols = slice(i * d, (i + 1) * d)
        if precise:
            gate_logit = _dot3(h, wg_ref[:, cols], w_refs[3][:, cols])
            mixed = _dot3(br[...].astype(F32), wb_ref[i], w_refs[4][i])
        else:
            gate_logit = _dot(h, wg_ref[:, cols])
            mixed = _dot(br[...].astype(BF16), wb_ref[i])
        y = jax.nn.sigmoid(gate_logit) * mixed
        merged = y if merged is None else merged + y
    if precise:
        out = _dot3(merged, wo_ref[...], w_refs[5][...])
    else:
        out = _dot(merged.astype(BF16), wo_ref[...])
    o_ref[...] = x_ref[...] + gate_ref[...] * out


def merge_out(branches, h, weights, layer, x, mod, tm):
    m, d = x.shape
    precise = len(weights) == 6
    br_spec = pl.BlockSpec((tm, W_MIX), lambda i: (i, 0))
    w_specs = [pl.BlockSpec((None, d, N_GATE), lambda i: (layer, 0, 0)),
               pl.BlockSpec((None, N_BRANCH, W_MIX, d), lambda i: (layer, 0, 0, 0)),
               pl.BlockSpec((None, d, d), lambda i: (layer, 0, 0))]
    return pl.pallas_call(
        functools.partial(_merge_kernel, precise=precise),
        grid=(m // tm,),
        in_specs=[br_spec] * N_BRANCH
                 + [pl.BlockSpec((tm, d), lambda i: (i, 0))]
                 + w_specs * (2 if precise else 1)
                 + [pl.BlockSpec((tm, d), lambda i: (i, 0)), mod.spec(2)],
        out_specs=pl.BlockSpec((tm, d), lambda i: (i, 0)),
        out_shape=jax.ShapeDtypeStruct((m, d), F32),
        compiler_params=_params("parallel"),
        name="merge_out",
    )(*branches, h, *weights, x, mod.arr)


def _route(scores, biased):
    lane_i = _iota(scores.shape, 1)
    real = lane_i < N_EXPERTS
    pos = lane_i % EXPERTS_PER_GROUP
    lane = lane_i.astype(F32)
    group = (lane_i // EXPERTS_PER_GROUP).astype(F32)
    neg = -jnp.inf
    n = scores.shape[1]
    mates = [biased]
    for k in range(1, EXPERTS_PER_GROUP):
        fwd = pltpu.roll(biased, n - k, axis=1)
        back = pltpu.roll(biased, EXPERTS_PER_GROUP - k, axis=1)
        mates.append(jnp.where(pos + k < EXPERTS_PER_GROUP, fwd, back))
    group_score = None
    for a in range(EXPERTS_PER_GROUP):
        for b in range(a + 1, EXPERTS_PER_GROUP):
            pair = mates[a] + mates[b]
            group_score = pair if group_score is None else jnp.maximum(group_score, pair)
    group_score = jnp.where(real, group_score, neg)
    best_score = jnp.max(group_score, axis=1, keepdims=True)
    best = jnp.min(jnp.where(group_score == best_score, group, float(n)), axis=1, keepdims=True)
    cand = jnp.where(real & (group == best), biased, neg)
    top0 = jnp.max(cand, axis=1, keepdims=True)
    idx0 = jnp.min(jnp.where(cand == top0, lane, float(n)), axis=1, keepdims=True)
    cand = jnp.where(lane == idx0, neg, cand)
    top1 = jnp.max(cand, axis=1, keepdims=True)
    idx1 = jnp.min(jnp.where(cand == top1, lane, float(n)), axis=1, keepdims=True)
    chosen = (lane == idx0) | (lane == idx1)
    sel = jnp.where(chosen, scores, 0.0)
    return sel / jnp.sum(sel, axis=1, keepdims=True), best


def _moe_kernel(x_ref, g_ref, sh_ref, sc_ref, gate_ref, rw_ref, rb_ref, w1_ref, w3_ref, w2_ref,
                gf_ref, o_ref, w1b_ref, w3b_ref, w2b_ref, h_scr, comb_scr, acc_scr, *, final_norm):
    e = pl.program_id(1)

    @pl.when(e == 0)
    def _():
        h = _norm_mod(x_ref[...], g_ref[...], sc_ref[...], sh_ref[...])
        h_scr[...] = h
        scores = jax.nn.sigmoid(_dot3(h, *_split2(rw_ref[...])))
        comb_scr[...] = _route(scores, scores + rb_ref[...])[0]
        acc_scr[...] = jnp.zeros_like(acc_scr)

    h = h_scr[...]
    w1_hi, w1_lo = _split2(w1_ref[...])
    w3_hi, w3_lo = _split2(w3_ref[...])
    w2_hi, w2_lo = _split2(w2_ref[...])
    w1b_ref[...] = w1_hi
    w3b_ref[...] = w3_hi
    w2b_ref[...] = w2_hi
    a = _dot3(h, w1_hi, w1_lo)
    b = _dot3(h, w3_hi, w3_lo)
    y = _dot3(_silu(a) * b, w2_hi, w2_lo)
    lane = _iota(comb_scr.shape, 1)
    w_e = jnp.sum(jnp.where(lane == e, comb_scr[...], 0.0), axis=1, keepdims=True)
    acc_scr[...] = acc_scr[...] + w_e * y

    @pl.when(e == N_EXPERTS - 1)
    def _():
        out = x_ref[...] + gate_ref[...] * acc_scr[...]
        if final_norm:
            ms = jnp.mean(out * out, axis=-1, keepdims=True)
            out = out * lax.rsqrt(ms + NORM_EPS) * gf_ref[...]
        o_ref[...] = out


def moe(x, g, mod, router_w_pad, router_b_pad, w1, w3, w2, layer, tm, final_gain=None):
    m, d = x.shape
    assert m == tm
    f = w1.shape[-1]
    n_e = w1.shape[1]
    expert = lambda i, e: (layer, e, 0, 0)
    per_layer = lambda i, e: (e, 0, 0)
    final_norm = final_gain is not None
    gf = (final_gain if final_norm else g).reshape(1, d)
    return pl.pallas_call(
        functools.partial(_moe_kernel, final_norm=final_norm),
        grid=(m // tm, N_EXPERTS),
        in_specs=[pl.BlockSpec((tm, d), lambda i, e: (i, 0)),
                  pl.BlockSpec((1, d), lambda i, e: (0, 0)),
                  mod.spec(3), mod.spec(4), mod.spec(5),
                  pl.BlockSpec((d, LANES), lambda i, e: (0, 0)),
                  pl.BlockSpec((1, LANES), lambda i, e: (0, 0)),
                  pl.BlockSpec((None, None, d, f), expert),
                  pl.BlockSpec((None, None, d, f), expert),
                  pl.BlockSpec((None, None, f, d), expert),
                  pl.BlockSpec((1, d), lambda i, e: (0, 0))],
        out_specs=[pl.BlockSpec((tm, d), lambda i, e: (i, 0)),
                   pl.BlockSpec((None, d, f), per_layer),
                   pl.BlockSpec((None, d, f), per_layer),
                   pl.BlockSpec((None, f, d), per_layer)],
        out_shape=[jax.ShapeDtypeStruct((m, d), F32),
                   jax.ShapeDtypeStruct((n_e, d, f), BF16),
                   jax.ShapeDtypeStruct((n_e, d, f), BF16),
                   jax.ShapeDtypeStruct((n_e, f, d), BF16)],
        scratch_shapes=[pltpu.VMEM((tm, d), F32), pltpu.VMEM((tm, LANES), F32),
                        pltpu.VMEM((tm, d), F32)],
        compiler_params=_params("arbitrary", "arbitrary"),
        name="moe",
    )(x, g.reshape(1, d), mod.arr, mod.arr, mod.arr, router_w_pad, router_b_pad, w1, w3, w2, gf)


_MOE_SUB = 128
_MOE_EXPERTS_PER_STEP = 2
N_GROUPS = N_EXPERTS // EXPERTS_PER_GROUP


def _moe_grouped_kernel(x_ref, g_ref, sh_ref, sc_ref, gate_ref, rw_ref, rb_ref, w1_ref, w3_ref,
                        w2_ref, gf_ref, o_ref, hs_scr, ys_scr, cs_scr, pt_scr, meta_ref, *,
                        final_norm):
    step = pl.program_id(1)
    per_step = w1_ref.shape[0]
    tm = x_ref.shape[0]
    slots = hs_scr.shape[0]
    sub = _MOE_SUB

    @pl.when(step == 0)
    def _():
        h = _norm_mod(x_ref[...], g_ref[...], sc_ref[...], sh_ref[...])
        hb = h.astype(BF16)
        scores = jax.nn.sigmoid(_dot(hb, rw_ref[...].astype(BF16)))
        comb, best = _route(scores, scores + rb_ref[...])
        lane = _iota((tm, LANES), 1).astype(F32)
        member = jnp.where(lane == best, 1.0, 0.0)
        before = (_iota((tm, tm), 1) < _iota((tm, tm), 0)).astype(BF16)
        rank = _dot(before, member.astype(BF16))
        count = rank[tm - 1:tm, :] + member[tm - 1:tm, :]
        room = jnp.floor((count + (sub - 1)) * (1.0 / sub)) * sub
        lane_row = _iota((1, LANES), 1)
        start = jnp.zeros((1, LANES), F32)
        for k in range(1, N_GROUPS):
            start = start + jnp.where(lane_row >= k, pltpu.roll(room, k, axis=1), 0.0)
        dest = jnp.sum(member * (start + rank), axis=1, keepdims=True)
        place = jnp.where(_iota((tm, slots), 1).astype(F32) == dest, 1.0, 0.0).astype(BF16)
        pt_scr[...] = place
        hs_scr[...] = _dot_tn(place, hb).astype(BF16)
        c_hi, c_lo = _split2(comb)
        cs_scr[...] = _dot_tn(place, c_hi) + _dot_tn(place, c_lo)
        ys_scr[...] = jnp.zeros_like(ys_scr)
        for k in range(N_GROUPS):
            pick = lane_row == k
            meta_ref[k] = jnp.sum(jnp.where(pick, start, 0.0)).astype(jnp.int32)
            meta_ref[N_GROUPS + k] = jnp.sum(jnp.where(pick, room, 0.0)).astype(jnp.int32) // sub

    group = (step * per_step) // EXPERTS_PER_GROUP
    first = meta_ref[group]
    lane = _iota((sub, LANES), 1)

    def body(s, carry):
        r0 = pl.multiple_of(first + s * sub, sub)
        rows = hs_scr[pl.ds(r0, sub), :]
        comb = cs_scr[pl.ds(r0, sub), :]
        acc = ys_scr[pl.ds(r0, sub), :]
        for k in range(per_step):
            a = _dot(rows, w1_ref[k])
            b = _dot(rows, w3_ref[k])
            y = _dot((_silu(a) * b).astype(BF16), w2_ref[k])
            w_e = jnp.sum(jnp.where(lane == step * per_step + k, comb, 0.0), axis=1, keepdims=True)
            acc = acc + w_e * y
        ys_scr[pl.ds(r0, sub), :] = acc
        return carry

    lax.fori_loop(0, meta_ref[N_GROUPS + group], body, 0)

    @pl.when(step == pl.num_programs(1) - 1)
    def _():
        y_hi, y_lo = _split2(ys_scr[...])
        place = pt_scr[...]
        out = x_ref[...] + gate_ref[...] * (_dot(place, y_hi) + _dot(place, y_lo))
        if final_norm:
            ms = jnp.mean(out * out, axis=-1, keepdims=True)
            out = out * lax.rsqrt(ms + NORM_EPS) * gf_ref[...]
        o_ref[...] = out


def moe_grouped(x, g, mod, router_w_pad, router_b_pad, w1, w3, w2, tm, final_gain=None):
    m, d = x.shape
    f = w1.shape[-1]
    slots = tm + N_GROUPS * _MOE_SUB
    per_step = _MOE_EXPERTS_PER_STEP
    assert EXPERTS_PER_GROUP % per_step == 0
    expert = lambda i, e: (e, 0, 0)
    final_norm = final_gain is not None
    gf = (final_gain if final_norm else g).reshape(1, d)
    return pl.pallas_call(
        functools.partial(_moe_grouped_kernel, final_norm=final_norm),
        grid=(m // tm, N_EXPERTS // per_step),
        in_specs=[pl.BlockSpec((tm, d), lambda i, e: (i, 0)),
                  pl.BlockSpec((1, d), lambda i, e: (0, 0)),
                  mod.spec(3), mod.spec(4), mod.spec(5),
                  pl.BlockSpec((d, LANES), lambda i, e: (0, 0)),
                  pl.BlockSpec((1, LANES), lambda i, e: (0, 0)),
                  pl.BlockSpec((per_step, d, f), expert),
                  pl.BlockSpec((per_step, d, f), expert),
                  pl.BlockSpec((per_step, f, d), expert),
                  pl.BlockSpec((1, d), lambda i, e: (0, 0))],
        out_specs=pl.BlockSpec((tm, d), lambda i, e: (i, 0)),
        out_shape=jax.ShapeDtypeStruct((m, d), F32),
        scratch_shapes=[pltpu.VMEM((slots, d), BF16), pltpu.VMEM((slots, d), F32),
                        pltpu.VMEM((slots, LANES), F32), pltpu.VMEM((tm, slots), BF16),
                        pltpu.SMEM((2 * N_GROUPS,), jnp.int32)],
        compiler_params=_params("parallel", "arbitrary"),
        name="moe_grouped",
    )(x, g.reshape(1, d), mod.arr, mod.arr, mod.arr, router_w_pad, router_b_pad, w1, w3, w2, gf)


def _rope_tables(pos):
    half = HEAD_DIM // 2
    inv = ROPE_BASE ** (-jnp.arange(half, dtype=F32) / half)
    ang = pos.astype(F32)[:, None] * inv[None, :]
    cos, sin = jnp.cos(ang), jnp.sin(ang)
    cos_row = jnp.tile(jnp.concatenate([cos, cos], axis=1), (1, N_HEADS))
    sin_row = jnp.tile(jnp.concatenate([-sin, sin], axis=1), (1, N_HEADS))
    return cos_row, sin_row


def _retention_tables(pos, chunk):
    lg = jnp.log1p(-(2.0 ** (-5.0 - jnp.arange(N_HEADS, dtype=F32))))
    idx = jnp.arange(chunk, dtype=F32)
    diff = idx[:, None] - idx[None, :]
    inner = jnp.where(diff >= 0, jnp.exp(lg[:, None, None] * jnp.maximum(diff, 0.0)), 0.0)
    q_decay = jnp.exp(lg[None, :] * (idx[:, None] + 1.0))
    k_decay = jnp.exp(lg[None, :] * (chunk - 1.0 - idx[:, None]))
    chunk_decay = jnp.exp(lg * chunk)
    lanes = lambda a: jnp.repeat(a, HEAD_DIM, axis=-1)
    cos, sin = _rope_tables(pos)
    return {"cos": cos, "sin": sin, "inner": inner, "q_decay": lanes(q_decay),
            "k_decay": lanes(k_decay), "chunk_decay": lanes(chunk_decay[None, :]),
            "gamma": chunk_decay}


def _pack_w_kernel(w_ref, wm_ref, wg_ref, wm_lo_ref, wg_lo_ref):
    rows = w_ref.shape[0]

    def put(hi_ref, lo_ref, cols, val):
        hi, lo = _split2(val)
        hi_ref[:, cols] = hi
        lo_ref[:, cols] = lo

    put(wm_ref, wm_lo_ref, slice(0, N_MAIN), w_ref[:, 0:N_MAIN])
    lane = _iota((rows, W_MIX), 1)
    put(wm_ref, wm_lo_ref, slice(N_MAIN, GATE_COL0),
        jnp.where(lane < N_HEADS, w_ref[:, N_MAIN:GATE_COL0], 0.0))
    step = 1024
    for c in range(0, N_GATE, step):
        width = min(step + LANES, N_GATE + N_HEADS - c)
        win = w_ref[:, N_MAIN + c:N_MAIN + c + width]
        put(wg_ref, wg_lo_ref, slice(c, c + step), win[:, N_HEADS:N_HEADS + step])


def pack_w_in(w_in):
    depth, d, n_in = w_in.shape
    assert n_in == N_MAIN + N_HEADS + N_GATE
    tr = 256
    main = pl.BlockSpec((None, tr, GATE_COL0), lambda l, r: (l, r, 0))
    gate = pl.BlockSpec((None, tr, N_GATE), lambda l, r: (l, r, 0))
    main_shape = jax.ShapeDtypeStruct((depth, d, GATE_COL0), BF16)
    gate_shape = jax.ShapeDtypeStruct((depth, d, N_GATE), BF16)
    return pl.pallas_call(
        _pack_w_kernel,
        grid=(depth, d // tr),
        in_specs=[pl.BlockSpec((None, tr, n_in), lambda l, r: (l, r, 0))],
        out_specs=[main, gate, main, gate],
        out_shape=[main_shape, gate_shape, main_shape, gate_shape],
        compiler_params=_params("parallel", "parallel"),
        name="pack_w_in",
    )(w_in)


def _split_w_kernel(w_ref, hi_ref, lo_ref):
    hi, lo = _split2(w_ref[...])
    hi_ref[...] = hi
    lo_ref[...] = lo


def split_weights(w):
    cols = w.shape[-1]
    flat = w.reshape(-1, cols)
    tr = 256
    spec = pl.BlockSpec((tr, cols), lambda i: (i, 0))
    shape = jax.ShapeDtypeStruct(flat.shape, BF16)
    hi, lo = pl.pallas_call(
        _split_w_kernel,
        grid=(flat.shape[0] // tr,),
        in_specs=[spec],
        out_specs=[spec, spec],
        out_shape=[shape, shape],
        compiler_params=_params("parallel"),
        name="split_weights",
    )(flat)
    return hi.reshape(w.shape), lo.reshape(w.shape)


def _block_diag_pool(w_pool_l):
    g, c, _ = w_pool_l.shape
    out = jnp.zeros((g * c, g * c), w_pool_l.dtype)
    for i in range(g):
        out = out.at[i * c:(i + 1) * c, i * c:(i + 1) * c].set(w_pool_l[i])
    return out


def _pad_lanes(row, n):
    return jnp.pad(row, ((0, 0), (0, n - row.shape[1])))


def _diag_blocks(s_bd):
    return jnp.stack([s_bd[:, h * HEAD_DIM:(h + 1) * HEAD_DIM, h * HEAD_DIM:(h + 1) * HEAD_DIM]
                      for h in range(N_HEADS)], axis=1)


def kernel(x_prompt, x_sample, cache_sb_kv, cache_fox_kv, cache_fox_logf, state_pool, state_ret,
           page_table, c_prompt, c_sample, w_ada, b_ada, norm_mix, w_in, w_pool, pool_scale,
           fox_bias, w_branch, w_out, norm_ffn, router_w, router_b, w1, w3, w2, norm_final):
    bp, tp, d = x_prompt.shape
    db, ts, _ = x_sample.shape
    assert ts == 1 and d == D_MODEL
    depth = w_in.shape[0]
    n_phys, _, _, page, _, _ = cache_sb_kv.shape
    n_pages = page_table.shape[1]
    past_len = n_pages * page
    mp = bp * tp

    mod_all = adaln(jnp.concatenate([c_prompt, c_sample], axis=0), w_ada, b_ada)

    sb_cache = jnp.transpose(cache_sb_kv, (0, 1, 2, 4, 5, 3)).reshape(n_phys, depth, 2, W_MIX, page)
    fox_cache = jnp.transpose(cache_fox_kv, (0, 1, 2, 4, 5, 3)).reshape(n_phys, depth, 2, W_MIX, page)
    logf_t = jnp.swapaxes(cache_fox_logf, 2, 3)
    state_t = jnp.transpose(state_ret, (1, 2, 3, 4, 0)).reshape(depth * N_HEADS, HEAD_DIM, HEAD_DIM, db)
    w_main, w_gate, w_main_lo, w_gate_lo = pack_w_in(w_in)
    router_w_pad = _pad_lanes(router_w, LANES)
    router_b_pad = _pad_lanes(router_b[None, :], LANES)
    tabs_p = _retention_tables(jnp.arange(tp), RET_CHUNK if tp % RET_CHUNK == 0 else tp)
    tabs_s = _retention_tables(past_len + jnp.arange(ts), ts)
    wb, wb_lo = split_weights(w_branch)
    wo, wo_lo = split_weights(w_out)
    w_merge_p = (w_gate, wb, wo)
    w_merge_s = w_merge_p + (w_gate_lo, wb_lo, wo_lo)

    tm_p = min(1024, tp)
    tm_mrg = min(512, tp)
    xp = x_prompt.reshape(mp, d)
    xs = x_sample.reshape(db, d)
    outs_p = {k: [] for k in ("logf", "pool", "ret")}
    outs_s = {k: [] for k in ("sb", "fox", "logf", "pool", "ret")}
    kv_prev = None

    for l in range(depth):
        w_bd_f32 = _block_diag_pool(w_pool[l])
        w_bd = w_bd_f32.astype(BF16)
        scale_row = pool_scale[l][None, :]
        bias_lanes = _pad_lanes(fox_bias[l][None, :], LANES)
        bias_row = _pad_lanes(fox_bias[l][None, :], W_MIX)

        final_gain = norm_final if l == depth - 1 else None

        mod_s = Mod(mod_all[l, bp:], 1, db)
        proj_s, h_s = in_proj(xs, norm_mix[l], mod_s, w_main, l, db, w_lo=w_main_lo)
        buf_t = jnp.swapaxes(state_pool[:, l], 0, 1)
        pool_s = pool_sample(buf_t, proj_s, w_bd_f32, scale_row, past_len)
        sb_s, fox_s, logf_s = decode_attn(page_table, proj_s, sb_cache, fox_cache, logf_t, bias_row, l)
        ret_s, state_new = ret_sample(proj_s, state_t, l, tabs_s)
        xs = merge_out((pool_s, sb_s, ret_s, fox_s), h_s, w_merge_s, l, xs, mod_s, db)
        xs, w1b, w3b, w2b = moe(xs, norm_ffn[l], mod_s, router_w_pad, router_b_pad, w1, w3, w2, l,
                                db, final_gain)

        mod_in = Mod(mod_all[l, :bp], tp, tm_p)
        mod_mrg = Mod(mod_all[l, :bp], tp, tm_mrg)
        kv_sb, kv_fox, proj, h = in_proj(xp, norm_mix[l], mod_in, w_main, l, tm_p,
                                         kv_seq=(bp, tp), kv_prev=kv_prev)
        kv_prev = (kv_sb, kv_fox)
        pool_o = pool_prompt(proj, w_bd, scale_row, bp, tp)
        sb_o = sb_prompt(proj, bp, tp)
        ret_o, ret_state = ret_prompt(proj, tabs_p, bp, tp)
        logf, cum, cum_t = logf_cum_prompt(proj, bias_lanes, bp, tp)
        fox_o = fox_prompt(proj, cum, cum_t, bp, tp)
        xp = merge_out((pool_o, sb_o, ret_o, fox_o), h, w_merge_p, l, xp, mod_mrg, tm_mrg)
        xp = moe_grouped(xp, norm_ffn[l], mod_in, router_w_pad, router_b_pad, w1b, w3b, w2b, tm_p,
                         final_gain)
        p3 = proj.reshape(bp, tp, GATE_COL0)
        outs_p["logf"].append(logf.reshape(bp, tp, LANES)[:, :, :N_HEADS])
        outs_p["pool"].append(p3[:, tp - POOL_BUF:, :W_MIX])
        outs_p["ret"].append(_diag_blocks(ret_state))

        def kv_s(cb, proj_s=proj_s):
            return jnp.stack([proj_s[:, cb * W_MIX:(cb + 1) * W_MIX],
                              proj_s[:, (cb + 1) * W_MIX:(cb + 2) * W_MIX]], axis=1)

        outs_s["sb"].append(kv_s(COL_SBK))
        outs_s["fox"].append(kv_s(COL_FK))
        outs_s["logf"].append(logf_s[:, :N_HEADS])
        outs_s["pool"].append(jnp.concatenate([state_pool[:, l, 1:], proj_s[:, None, :W_MIX]], axis=1))
        outs_s["ret"].append(state_new)

    y_p = xp.reshape(bp, tp, d)
    y_s = xs.reshape(db, ts, d)

    def heads(a, t):
        return a.reshape(a.shape[0], depth, 2, t, N_HEADS, HEAD_DIM)

    def from_transposed(kv):
        return heads(jnp.transpose(kv, (2, 0, 1, 4, 3)), tp)

    return (y_p, y_s,
            from_transposed(kv_prev[0]),
            heads(jnp.stack(outs_s["sb"], axis=1), ts),
            from_transposed(kv_prev[1]),
            heads(jnp.stack(outs_s["fox"], axis=1), ts),
            jnp.stack(outs_p["logf"], axis=1),
            jnp.stack(outs_s["logf"], axis=1).reshape(db, depth, ts, N_HEADS),
            jnp.stack(outs_p["pool"], axis=1),
            jnp.stack(outs_s["pool"], axis=1),
            jnp.stack(outs_p["ret"], axis=1),
            jnp.transpose(jnp.stack(outs_s["ret"], axis=0), (4, 0, 1, 2, 3)))
```

```python
import functools

import jax
import jax.numpy as jnp
from jax import lax
from jax.experimental import pallas as pl
from jax.experimental.pallas import tpu as pltpu

F32 = jnp.float32
BF16 = jnp.bfloat16

D_MODEL = 1024
HEAD_DIM = 64
W_MIX = 256
N_HEADS = W_MIX // HEAD_DIM
N_BRANCH = 4
POOL_WINDOWS = (2, 4, 8, 16)
POOL_BUF = 15
RET_CHUNK = 128
ROPE_BASE = 10000.0
N_EXPERTS = 16
EXPERTS_PER_GROUP = 4
NORM_EPS = 1e-6
ATTN_SCALE = HEAD_DIM ** -0.5
LANES = 128
SUBLANES = 8
VMEM_LIMIT = 56 * 1024 * 1024

COL_U, COL_SBQ, COL_SBK, COL_SBV = 0, 1, 2, 3
COL_RQ, COL_RK, COL_RV, COL_RG = 4, 5, 6, 7
COL_FQ, COL_FK, COL_FV, COL_LOGIT = 8, 9, 10, 11
N_MAIN = 11 * W_MIX
GATE_COL0 = 12 * W_MIX
N_GATE = N_BRANCH * D_MODEL


def _params(*sem):
    return pltpu.CompilerParams(dimension_semantics=sem, vmem_limit_bytes=VMEM_LIMIT)


def _split2(x):
    hi = x.astype(BF16)
    lo = (x - hi.astype(F32)).astype(BF16)
    return hi, lo


def _split3(x):
    hi = x.astype(BF16)
    r = x - hi.astype(F32)
    mid = r.astype(BF16)
    lo = (r - mid.astype(F32)).astype(BF16)
    return hi, mid, lo


def _dot(a, b):
    return jnp.dot(a, b, preferred_element_type=F32)


def _dot_nt(a, b):
    return lax.dot_general(a, b, (((1,), (1,)), ((), ())), preferred_element_type=F32)


def _dot_tn(a, b):
    return lax.dot_general(a, b, (((0,), (0,)), ((), ())), preferred_element_type=F32)


def _dot_split(x, w, parts=2):
    ps = _split2(x) if parts == 2 else _split3(x)
    acc = _dot(ps[0], w)
    for p in ps[1:]:
        acc = acc + _dot(p, w)
    return acc


def _dot3(x, w_hi, w_lo):
    x_hi, x_lo = _split2(x)
    return _dot(x_hi, w_hi) + (_dot(x_lo, w_hi) + _dot(x_hi, w_lo))


def _iota(shape, dim):
    return lax.broadcasted_iota(jnp.int32, shape, dim)


def _softplus(z):
    return jnp.maximum(z, 0.0) + jnp.log1p(jnp.exp(-jnp.abs(z)))


def _silu(x):
    return x * jax.nn.sigmoid(x)


def _adaln_kernel(c_ref, w_ref, b_ref, o_ref):
    w_hi, w_lo = _split2(w_ref[...])
    o_ref[...] = _dot3(_silu(c_ref[...]), w_hi, w_lo) + b_ref[...]


def adaln(c_all, w_ada, b_ada):
    rows, d = c_all.shape
    depth, _, n = w_ada.shape
    tn = 1024
    return pl.pallas_call(
        _adaln_kernel,
        grid=(depth, n // tn),
        in_specs=[pl.BlockSpec((rows, d), lambda l, j: (0, 0)),
                  pl.BlockSpec((None, d, tn), lambda l, j: (l, 0, j)),
                  pl.BlockSpec((None, 1, tn), lambda l, j: (l, 0, j))],
        out_specs=pl.BlockSpec((None, rows, tn), lambda l, j: (l, 0, j)),
        out_shape=jax.ShapeDtypeStruct((depth, rows, n), F32),
        compiler_params=_params("parallel", "parallel"),
        name="adaln",
    )(c_all, w_ada, b_ada.reshape(depth, 1, n))


class Mod:
    def __init__(self, arr, rows_per_vec, tm):
        self.per_row = rows_per_vec == 1
        self.arr = arr if self.per_row else arr.reshape(arr.shape[0], 1, arr.shape[-1])
        self.tiles_per_vec = 1 if self.per_row else rows_per_vec // tm
        self.tm = tm

    def spec(self, k):
        if self.per_row:
            return pl.BlockSpec((self.tm, D_MODEL), lambda i, *_: (i, k))
        t = self.tiles_per_vec
        return pl.BlockSpec((None, 1, D_MODEL), lambda i, *_: (i // t, 0, k))


def _norm_mod(x, g, sc, sh):
    ms = jnp.mean(x * x, axis=-1, keepdims=True)
    y = x * lax.rsqrt(ms + NORM_EPS) * g
    return y * (1.0 + sc) + sh


_PROJ_TN = 1536


def _inproj_kernel(x_ref, g_ref, sh_ref, sc_ref, w_ref, *refs, kv_out, precise):
    j = pl.program_id(1)
    o_ref, h_ref = refs[-2], refs[-1]

    @pl.when(j == 0)
    def _():
        h_ref[...] = _norm_mod(x_ref[...], g_ref[...], sc_ref[...], sh_ref[...]).astype(h_ref.dtype)

    if precise:
        res = _dot3(h_ref[...], w_ref[...], refs[0][...])
    else:
        res = _dot(h_ref[...], w_ref[...])
    o_ref[...] = res
    if kv_out:
        per_tile = _PROJ_TN // W_MIX
        for col_k, kv_ref in ((COL_SBK, refs[-4]), (COL_FK, refs[-3])):
            c0 = (col_k % per_tile) * W_MIX

            @pl.when(j == col_k // per_tile)
            def _(c0=c0, kv_ref=kv_ref):
                kv_ref[0] = jnp.transpose(res[:, c0:c0 + W_MIX])
                kv_ref[1] = jnp.transpose(res[:, c0 + W_MIX:c0 + 2 * W_MIX])


def in_proj(x, g, mod, w_all, layer, tm, kv_seq=None, kv_prev=None, w_lo=None):
    m, d = x.shape
    depth, _, n = w_all.shape
    tn = _PROJ_TN
    precise = w_lo is not None
    w_spec = pl.BlockSpec((None, d, tn), lambda i, j: (layer, 0, j))
    in_specs = [pl.BlockSpec((tm, d), lambda i, j: (i, 0)),
                pl.BlockSpec((1, d), lambda i, j: (0, 0)),
                mod.spec(0), mod.spec(1), w_spec]
    args = [x, g.reshape(1, d), mod.arr, mod.arr, w_all]
    if precise:
        in_specs.append(w_spec)
        args.append(w_lo)
    out_specs = [pl.BlockSpec((tm, tn), lambda i, j: (i, j)),
                 pl.BlockSpec((tm, d), lambda i, j: (i, 0))]
    out_shape = [jax.ShapeDtypeStruct((m, n), F32),
                 jax.ShapeDtypeStruct((m, d), F32 if precise else BF16)]
    aliases = {}
    if kv_seq is not None:
        b, t = kv_seq
        per_b = t // tm
        kv_spec = pl.BlockSpec((None, 2, None, W_MIX, tm),
                               lambda i, j: (layer, 0, i // per_b, 0, i % per_b))
        kv_shape = jax.ShapeDtypeStruct((depth, 2, b, W_MIX, t), F32)
        out_specs = [kv_spec, kv_spec] + out_specs
        out_shape = [kv_shape, kv_shape] + out_shape
        if kv_prev is not None:
            aliases = {len(args): 0, len(args) + 1: 1}
            in_specs += [pl.BlockSpec(memory_space=pl.ANY)] * 2
            args += list(kv_prev)
    outs = pl.pallas_call(
        functools.partial(_inproj_kernel, kv_out=kv_seq is not None, precise=precise),
        grid=(m // tm, n // tn),
        in_specs=in_specs,
        out_specs=out_specs,
        out_shape=out_shape,
        input_output_aliases=aliases,
        compiler_params=_params("parallel", "arbitrary"),
        name="in_proj",
    )(*args)
    return outs


def _pool_tail(s2, s4, s8, s16, u, pos0, w_ref, sc_ref, o_ref):
    t = u.shape[0]
    lane = _iota((1, W_MIX), 1) // (W_MIX // len(POOL_WINDOWS))
    win = jnp.where(lane == 0, s2, jnp.where(lane == 1, s4, jnp.where(lane == 2, s8, s16)))
    width = jnp.where(lane == 0, 2, jnp.where(lane == 1, 4, jnp.where(lane == 2, 8, 16)))
    count = jnp.minimum(width, pos0 + 1 + _iota((t, W_MIX), 0)).astype(F32)
    resid = win / count - u
    if w_ref.dtype == F32:
        w_hi, w_lo = _split2(w_ref[...])
        mixed = _dot3(resid, w_hi, w_lo)
    else:
        mixed = _dot(resid.astype(BF16), w_ref[...])
    o_ref[...] = (mixed * sc_ref[...]).astype(o_ref.dtype)


_POOL_PAD = 32


def _pool_prompt_kernel(u_ref, w_ref, sc_ref, o_ref, a_scr, b_scr, c_scr):
    t = u_ref.shape[0]
    p = _POOL_PAD
    u = u_ref[...]
    a_scr[0:p, :] = jnp.zeros((p, W_MIX), F32)
    a_scr[p:p + t, :] = u

    def stage(src, dst, k, lo):
        n = p + t - lo
        dst[lo:lo + n, :] = src[lo:lo + n, :] + src[lo - k:lo - k + n, :]

    stage(a_scr, b_scr, 1, 8)
    stage(b_scr, c_scr, 2, 16)
    stage(c_scr, a_scr, 4, 24)
    s8 = a_scr[p:p + t, :]
    s16 = s8 + a_scr[p - 8:p - 8 + t, :]
    _pool_tail(b_scr[p:p + t, :], c_scr[p:p + t, :], s8, s16, u, 0, w_ref, sc_ref, o_ref)


def pool_prompt(proj, w_bd, scale, b, t):
    return pl.pallas_call(
        _pool_prompt_kernel,
        grid=(b,),
        in_specs=[pl.BlockSpec((t, W_MIX), lambda i: (i, COL_U)),
                  pl.BlockSpec((W_MIX, W_MIX), lambda i: (0, 0)),
                  pl.BlockSpec((1, W_MIX), lambda i: (0, 0))],
        out_specs=pl.BlockSpec((t, W_MIX), lambda i: (i, 0)),
        out_shape=jax.ShapeDtypeStruct((b * t, W_MIX), BF16),
        scratch_shapes=[pltpu.VMEM((t + _POOL_PAD, W_MIX), F32)] * 3,
        compiler_params=_params("parallel"),
        name="pool_prompt",
    )(proj, w_bd, scale)


def _pool_sample_kernel(buf_ref, u_ref, w_ref, sc_ref, o_ref, *, pos0):
    u = u_ref[...]
    s2 = u + buf_ref[14]
    s4 = s2 + buf_ref[13] + buf_ref[12]
    s8 = s4 + buf_ref[11] + buf_ref[10] + buf_ref[9] + buf_ref[8]
    s16 = s8
    for r in range(7, -1, -1):
        s16 = s16 + buf_ref[r]
    _pool_tail(s2, s4, s8, s16, u, pos0, w_ref, sc_ref, o_ref)


def pool_sample(buf_t, proj, w_bd, scale, pos0):
    rows = proj.shape[0]
    return pl.pallas_call(
        functools.partial(_pool_sample_kernel, pos0=pos0),
        grid=(1,),
        in_specs=[pl.BlockSpec((POOL_BUF, rows, W_MIX), lambda i: (0, 0, 0)),
                  pl.BlockSpec((rows, W_MIX), lambda i: (0, COL_U)),
                  pl.BlockSpec((W_MIX, W_MIX), lambda i: (0, 0)),
                  pl.BlockSpec((1, W_MIX), lambda i: (0, 0))],
        out_specs=pl.BlockSpec((rows, W_MIX), lambda i: (0, 0)),
        out_shape=jax.ShapeDtypeStruct((rows, W_MIX), F32),
        compiler_params=_params("arbitrary"),
        name="pool_sample",
    )(buf_t, proj, w_bd, scale)


_ATT_TILE = 256


def _head_of_lane():
    return _iota((1, W_MIX), 1) // HEAD_DIM


def _stack_heads(q):
    head = _head_of_lane()
    return jnp.concatenate([jnp.where(head == h, q, 0.0) for h in range(N_HEADS)], axis=0).astype(BF16)


def _unstack_heads(acc, tq):
    head = _head_of_lane()
    out = jnp.where(head == 0, acc[0:tq], 0.0)
    for h in range(1, N_HEADS):
        out = out + jnp.where(head == h, acc[h * tq:(h + 1) * tq], 0.0)
    return out


def _load_kv_once(k_ref, v_ref, k_scr, v_scr):
    @pl.when(pl.program_id(1) == 0)
    def _():
        k_scr[...] = k_ref[...].astype(BF16)
        v_scr[...] = v_ref[...].astype(BF16)


def _sb_prompt_kernel(q_ref, k_ref, v_ref, o_ref, k_scr, v_scr):
    tq = tk = _ATT_TILE
    i = pl.program_id(1)
    _load_kv_once(k_ref, v_ref, k_scr, v_scr)
    q4 = _stack_heads(q_ref[...] * ATTN_SCALE)
    rows = N_HEADS * tq
    below_diag = _iota((rows, tk), 1) < (_iota((rows, tk), 0) % tq)
    upper = (_iota((tk, tk), 0) > _iota((tk, tk), 1)).astype(BF16)

    def key_off(blk):
        return pl.multiple_of(blk * tk, tk)

    def blocks(carry, offs, diagonal=False):
        acc, run = carry
        for off in offs:
            z = _dot_nt(q4, k_scr[pl.ds(off, tk), :])
            sp = jnp.maximum(z, 0.0) + jnp.log(1.0 + jnp.exp(-jnp.abs(z)))
            log_not = jnp.where(below_diag, -sp, 0.0) if diagonal else -sp
            suffix = _dot_split(log_not, upper)
            a = jnp.exp(z - sp + suffix + run)
            if diagonal:
                a = jnp.where(below_diag, a, 0.0)
            acc = acc + _dot(a.astype(BF16), v_scr[pl.ds(off, tk), :])
            run = run + suffix[:, 0:1] + log_not[:, 0:1]
        return acc, run

    carry = blocks((jnp.zeros((rows, W_MIX), F32), jnp.zeros((rows, 1), F32)), [key_off(i)], True)
    carry = lax.fori_loop(
        0, i // 2, lambda s, c: blocks(c, [key_off(i - 1 - 2 * s), key_off(i - 2 - 2 * s)]), carry)
    acc, _ = lax.fori_loop(0, i % 2, lambda s, c: blocks(c, [0]), carry)
    o_ref[...] = _unstack_heads(acc, tq).astype(o_ref.dtype)


def _attn_prompt_specs(t, col_q, col_k, col_v):
    tq = _ATT_TILE
    nq = t // tq
    return [pl.BlockSpec((tq, W_MIX), lambda b, i: (b * nq + i, col_q)),
            pl.BlockSpec((t, W_MIX), lambda b, i: (b, col_k)),
            pl.BlockSpec((t, W_MIX), lambda b, i: (b, col_v))]


def sb_prompt(proj, b, t):
    tq = _ATT_TILE
    nq = t // tq
    return pl.pallas_call(
        _sb_prompt_kernel,
        grid=(b, nq),
        in_specs=_attn_prompt_specs(t, COL_SBQ, COL_SBK, COL_SBV),
        out_specs=pl.BlockSpec((tq, W_MIX), lambda bb, i: (bb * nq + i, 0)),
        out_shape=jax.ShapeDtypeStruct((b * t, W_MIX), BF16),
        scratch_shapes=[pltpu.VMEM((t, W_MIX), BF16)] * 2,
        compiler_params=_params("parallel", "arbitrary"),
        name="sb_prompt",
    )(proj, proj, proj)


def _logf_cum_kernel(x_ref, bias_ref, logf_ref, cum_ref, cumt_ref):
    t = x_ref.shape[0]
    blk = 256
    logf = jax.nn.log_sigmoid(x_ref[...] + bias_ref[...])
    logf_ref[...] = logf
    lower = (_iota((blk, blk), 1) <= _iota((blk, blk), 0)).astype(BF16)
    carry = jnp.zeros((1, LANES), F32)
    for c in range(t // blk):
        part = logf[c * blk:(c + 1) * blk]
        hi, mid, lo = _split3(part)
        cum = _dot(lower, hi) + _dot(lower, mid) + _dot(lower, lo) + carry
        cum_ref[c * blk:(c + 1) * blk, :] = cum
        carry = cum[blk - 1:blk, :]
    cumt_ref[...] = jnp.transpose(cum_ref[...])[0:SUBLANES, :]


def logf_cum_prompt(proj, bias_pad, b, t):
    m = b * t
    return pl.pallas_call(
        _logf_cum_kernel,
        grid=(b,),
        in_specs=[pl.BlockSpec((t, LANES), lambda i: (i, COL_LOGIT * W_MIX // LANES)),
                  pl.BlockSpec((1, LANES), lambda i: (0, 0))],
        out_specs=[pl.BlockSpec((t, LANES), lambda i: (i, 0)),
                   pl.BlockSpec((t, LANES), lambda i: (i, 0)),
                   pl.BlockSpec((None, SUBLANES, t), lambda i: (i, 0, 0))],
        out_shape=[jax.ShapeDtypeStruct((m, LANES), F32),
                   jax.ShapeDtypeStruct((m, LANES), F32),
                   jax.ShapeDtypeStruct((b, SUBLANES, t), F32)],
        compiler_params=_params("parallel"),
        name="logf_cum",
    )(proj, bias_pad)


def _fox_prompt_kernel(q_ref, k_ref, v_ref, cq_ref, ck_ref, o_ref, k_scr, v_scr):
    tq = tk = _ATT_TILE
    i = pl.program_id(1)
    _load_kv_once(k_ref, v_ref, k_scr, v_scr)
    q4 = _stack_heads(q_ref[...] * ATTN_SCALE)
    rows = N_HEADS * tq
    on_or_below_diag = _iota((rows, tk), 1) <= (_iota((rows, tk), 0) % tq)
    cq = cq_ref[...]

    def scores(off):
        z = _dot_nt(q4, k_scr[pl.ds(off, tk), :])
        return jnp.concatenate(
            [z[h * tq:(h + 1) * tq] + (cq[:, h:h + 1] - ck_ref[h:h + 1, pl.ds(off, tk)])
             for h in range(N_HEADS)], axis=0)

    def update(carry, scs, offs):
        acc, m, l = carry
        m_new = m
        for sc in scs:
            m_new = jnp.maximum(m_new, jnp.max(sc, axis=1, keepdims=True))
        alpha = jnp.exp(m - m_new)
        acc = alpha * acc
        l = alpha * l
        for sc, off in zip(scs, offs):
            p = jnp.exp(sc - m_new)
            acc = acc + _dot(p.astype(BF16), v_scr[pl.ds(off, tk), :])
            l = l + jnp.sum(p, axis=1, keepdims=True)
        return acc, m_new, l

    def key_off(blk):
        return pl.multiple_of(blk * tk, tk)

    diag = jnp.where(on_or_below_diag, scores(key_off(i)), -jnp.inf)
    carry = update((jnp.zeros((rows, W_MIX), F32), jnp.full((rows, 1), -jnp.inf, F32),
                    jnp.zeros((rows, 1), F32)), [diag], [key_off(i)])

    def pair(s, c):
        offs = [key_off(i - 1 - 2 * s), key_off(i - 2 - 2 * s)]
        return update(c, [scores(o) for o in offs], offs)

    carry = lax.fori_loop(0, i // 2, pair, carry)
    acc, _, l = lax.fori_loop(0, i % 2, lambda s, c: update(c, [scores(0)], [0]), carry)
    o_ref[...] = _unstack_heads(acc / l, tq).astype(o_ref.dtype)


def fox_prompt(proj, cum, cum_t, b, t):
    tq = _ATT_TILE
    nq = t // tq
    specs = _attn_prompt_specs(t, COL_FQ, COL_FK, COL_FV)
    specs += [pl.BlockSpec((tq, LANES), lambda bb, i: (bb * nq + i, 0)),
              pl.BlockSpec((None, SUBLANES, t), lambda bb, i: (bb, 0, 0))]
    return pl.pallas_call(
        _fox_prompt_kernel,
        grid=(b, nq),
        in_specs=specs,
        out_specs=pl.BlockSpec((tq, W_MIX), lambda bb, i: (bb * nq + i, 0)),
        out_shape=jax.ShapeDtypeStruct((b * t, W_MIX), BF16),
        scratch_shapes=[pltpu.VMEM((t, W_MIX), BF16)] * 2,
        compiler_params=_params("parallel", "arbitrary"),
        name="fox_prompt",
    )(proj, proj, proj, cum, cum_t)


def _rotary(x, cos, sin_signed):
    half = HEAD_DIM // 2
    first = (_iota((1, W_MIX), 1) % HEAD_DIM) < half
    swapped = jnp.where(first, pltpu.roll(x, W_MIX - half, axis=1), pltpu.roll(x, half, axis=1))
    return x * cos + swapped * sin_signed


def _head_mean_matrix():
    same = (_iota((W_MIX, W_MIX), 0) // HEAD_DIM) == (_iota((W_MIX, W_MIX), 1) // HEAD_DIM)
    return jnp.where(same, 1.0 / HEAD_DIM, 0.0).astype(BF16)


def _head_norm_gate(o, gate_logit):
    avg = _head_mean_matrix()
    mu = _dot_split(o, avg)
    dev = o - mu
    var = _dot_split(dev * dev, avg)
    return _silu(gate_logit) * (dev * lax.rsqrt(var + NORM_EPS))


def _ret_prompt_kernel(q_ref, k_ref, v_ref, g_ref, cos_ref, sin_ref, inner_ref, qd_ref, kd_ref,
                       cd_ref, o_ref, s_ref):
    c = pl.program_id(1)

    @pl.when(c == 0)
    def _():
        s_ref[...] = jnp.zeros_like(s_ref)

    cos = cos_ref[...]
    sin = sin_ref[...]
    ch = cos.shape[0]
    same = (_iota((W_MIX, W_MIX), 0) // HEAD_DIM) == (_iota((W_MIX, W_MIX), 1) // HEAD_DIM)
    inner = inner_ref[...].reshape(N_HEADS * ch, ch)
    for s in range(q_ref.shape[0]):
        q = _rotary(q_ref[s], cos, sin)
        k = _rotary(k_ref[s], cos, sin) * ATTN_SCALE
        kb = k.astype(BF16)
        vb = v_ref[s].astype(BF16)
        state = s_ref[s]
        att = _dot_nt(_stack_heads(q), kb) * inner
        o = _dot(q.astype(BF16), state.astype(BF16)) * qd_ref[...]
        o = o + _unstack_heads(_dot(att.astype(BF16), vb), ch)
        kv = _dot_tn((k * kd_ref[...]).astype(BF16), vb)
        s_ref[s] = state * cd_ref[...] + jnp.where(same, kv, 0.0)
        o_ref[s] = _head_norm_gate(o, g_ref[s]).astype(o_ref.dtype)


def ret_prompt(proj, tabs, b, t):
    ch = RET_CHUNK
    nc = t // ch
    nb = max(n for n in (8, 4, 2, 1) if b % n == 0)
    proj3 = proj.reshape(b, t, proj.shape[1])

    def col(cb):
        return pl.BlockSpec((nb, ch, W_MIX), lambda bb, c: (bb, c, cb))

    full = lambda shape: pl.BlockSpec(shape, lambda bb, c: (0,) * len(shape))
    out, state = pl.pallas_call(
        _ret_prompt_kernel,
        grid=(b // nb, nc),
        in_specs=[col(COL_RQ), col(COL_RK), col(COL_RV), col(COL_RG),
                  pl.BlockSpec((ch, W_MIX), lambda bb, c: (c, 0)),
                  pl.BlockSpec((ch, W_MIX), lambda bb, c: (c, 0)),
                  full((N_HEADS, ch, ch)), full((ch, W_MIX)), full((ch, W_MIX)), full((1, W_MIX))],
        out_specs=[pl.BlockSpec((nb, ch, W_MIX), lambda bb, c: (bb, c, 0)),
                   pl.BlockSpec((nb, W_MIX, W_MIX), lambda bb, c: (bb, 0, 0))],
        out_shape=[jax.ShapeDtypeStruct((b, t, W_MIX), BF16),
                   jax.ShapeDtypeStruct((b, W_MIX, W_MIX), F32)],
        compiler_params=_params("parallel", "arbitrary"),
        name="ret_prompt",
    )(proj3, proj3, proj3, proj3, tabs["cos"], tabs["sin"], tabs["inner"], tabs["q_decay"],
      tabs["k_decay"], tabs["chunk_decay"])
    return out.reshape(b * t, W_MIX), state


def _ret_sample_kernel(gam_ref, q_ref, k_ref, v_ref, g_ref, cos_ref, sin_ref, s_ref,
                       o_ref, sn_ref, qt_scr, kt_scr, vt_scr, ot_scr):
    h = pl.program_id(0)
    rows = q_ref.shape[0]

    @pl.when(h == 0)
    def _():
        cos = cos_ref[...]
        sin = sin_ref[...]
        qt_scr[...] = jnp.transpose(_rotary(q_ref[...], cos, sin))
        kt_scr[...] = jnp.transpose(_rotary(k_ref[...], cos, sin) * ATTN_SCALE)
        vt_scr[...] = jnp.transpose(v_ref[...])

    row0 = pl.multiple_of(h * HEAD_DIM, HEAD_DIM)
    gamma = gam_ref[h]
    vh = vt_scr[pl.ds(row0, HEAD_DIM), :]

    def body(d, cross):
        qd = qt_scr[pl.ds(row0 + d, 1), :]
        kd = kt_scr[pl.ds(row0 + d, 1), :]
        s_d = s_ref[d]
        sn_ref[d] = gamma * s_d + kd * vh
        return cross + qd * s_d

    cross = lax.fori_loop(0, HEAD_DIM, body, jnp.zeros((HEAD_DIM, rows), F32))
    qk = jnp.sum(qt_scr[pl.ds(row0, HEAD_DIM), :] * kt_scr[pl.ds(row0, HEAD_DIM), :],
                 axis=0, keepdims=True)
    ot_scr[pl.ds(row0, HEAD_DIM), :] = qk * vh + cross * gamma

    @pl.when(h == N_HEADS - 1)
    def _():
        o_ref[...] = _head_norm_gate(jnp.transpose(ot_scr[...]), g_ref[...])


def ret_sample(proj, state_t, layer, tabs):
    rows = proj.shape[0]

    def col(cb):
        return pl.BlockSpec((rows, W_MIX), lambda h: (0, cb))

    row_tab = pl.BlockSpec((1, W_MIX), lambda h: (0, 0))
    state_blk = (None, HEAD_DIM, HEAD_DIM, rows)
    return pl.pallas_call(
        _ret_sample_kernel,
        grid=(N_HEADS,),
        in_specs=[pl.BlockSpec(memory_space=pltpu.SMEM),
                  col(COL_RQ), col(COL_RK), col(COL_RV), col(COL_RG), row_tab, row_tab,
                  pl.BlockSpec(state_blk, lambda h: (layer * N_HEADS + h, 0, 0, 0))],
        out_specs=[pl.BlockSpec((rows, W_MIX), lambda h: (0, 0)),
                   pl.BlockSpec(state_blk, lambda h: (h, 0, 0, 0))],
        out_shape=[jax.ShapeDtypeStruct((rows, W_MIX), F32),
                   jax.ShapeDtypeStruct((N_HEADS, HEAD_DIM, HEAD_DIM, rows), F32)],
        scratch_shapes=[pltpu.VMEM((W_MIX, rows), F32)] * 4,
        compiler_params=_params("arbitrary"),
        name="ret_sample",
    )(tabs["gamma"], proj, proj, proj, proj, tabs["cos"], tabs["sin"], state_t)


def _suffix_sums(x3, upper):
    n_pages, rows, page = x3.shape
    flat = x3.reshape(n_pages * rows, page)
    within = _dot_split(flat, upper, parts=3)
    total = (within[:, 0:1] + flat[:, 0:1]).reshape(n_pages, rows, 1)
    within = within.reshape(n_pages, rows, page)
    carry = jnp.zeros((rows, 1), F32)
    outs = [None] * n_pages
    for j in range(n_pages - 1, -1, -1):
        outs[j] = within[j] + carry
        carry = carry + total[j]
    return jnp.stack(outs)


def _rows_to_col(row):
    pick = _iota((SUBLANES, W_MIX), 0) == _iota((SUBLANES, W_MIX), 1)
    return jnp.sum(jnp.where(pick, row, 0.0), axis=1, keepdims=True)


def _lane_bcast_col(row):
    return jnp.transpose(jnp.broadcast_to(row, (LANES, W_MIX)))


def _col_to_row(col):
    return jnp.transpose(jnp.broadcast_to(col, (W_MIX, LANES)))[0:1, :]


def _head_scores(kt, qcol):
    page = kt.shape[1]
    prod = kt * qcol
    row = _iota((SUBLANES, page), 0)
    out = jnp.zeros((SUBLANES, page), F32)
    for h in range(N_HEADS):
        part = prod[h * HEAD_DIM:(h + 1) * HEAD_DIM].reshape(HEAD_DIM // SUBLANES, SUBLANES, page)
        tot = jnp.sum(jnp.sum(part, axis=0), axis=0, keepdims=True)
        out = jnp.where(row == h, tot, out)
    return out


def _head_weights(a8):
    page = a8.shape[1]
    return jnp.concatenate([jnp.broadcast_to(a8[h:h + 1, :], (HEAD_DIM, page))
                            for h in range(N_HEADS)], axis=0)


def _decode_kernel(pt_ref, qs_ref, qf_ref, kf_ref, vf_ref, lg_ref, bias_ref, *refs, n_pages):
    sb_pages = refs[0:n_pages]
    fx_pages = refs[n_pages:2 * n_pages]
    lf_pages = refs[2 * n_pages:3 * n_pages]
    sb_o, fx_o, lf_o, lf_scr = refs[3 * n_pages:]
    page = sb_pages[0].shape[2]
    head_rows = _iota((SUBLANES, W_MIX), 0) == (_iota((SUBLANES, W_MIX), 1) // HEAD_DIM)
    upper = (_iota((page, page), 0) > _iota((page, page), 1)).astype(BF16)

    def per_head_col(row):
        return jnp.sum(jnp.where(head_rows, row, 0.0), axis=1, keepdims=True)

    def per_head_lanes(col):
        return jnp.sum(jnp.where(head_rows, col, 0.0), axis=0, keepdims=True)

    qs_col = _lane_bcast_col(qs_ref[...] * ATTN_SCALE)
    z = jnp.stack([_head_scores(sb_pages[j][0], qs_col) for j in range(n_pages)])
    sp = _softplus(z)
    after = _suffix_sums(-sp, upper)
    a = jnp.exp(z - sp + after)
    acc = jnp.zeros((W_MIX, page), F32)
    for j in range(n_pages):
        acc = acc + sb_pages[j][1] * _head_weights(a[j])
    sb_o[...] = _col_to_row(jnp.sum(acc, axis=1, keepdims=True))

    logf_new = jax.nn.log_sigmoid(lg_ref[...] + bias_ref[...])
    lf_o[...] = logf_new
    qf_row = qf_ref[...] * ATTN_SCALE
    qf_col = _lane_bcast_col(qf_row)
    lf_scr[...] = jnp.zeros_like(lf_scr)
    for j in range(n_pages):
        lf_scr[j, 0:N_HEADS, :] = lf_pages[j][...]
    decay = _suffix_sums(lf_scr[...], upper)
    s = jnp.stack([_head_scores(fx_pages[j][0], qf_col) for j in range(n_pages)])
    s = s + decay + _rows_to_col(logf_new)
    s_self = per_head_col(qf_row * kf_ref[...])
    m = jnp.max(jnp.max(s, axis=0), axis=1, keepdims=True)
    m = jnp.maximum(m, s_self)
    p = jnp.exp(s - m)
    p_self = jnp.exp(s_self - m)
    l = jnp.sum(jnp.sum(p, axis=0), axis=1, keepdims=True) + p_self
    p = p / l
    acc = jnp.zeros((W_MIX, page), F32)
    for j in range(n_pages):
        acc = acc + fx_pages[j][1] * _head_weights(p[j])
    fx_o[...] = (_col_to_row(jnp.sum(acc, axis=1, keepdims=True))
                 + per_head_lanes(p_self / l) * vf_ref[...])


def decode_attn(page_table, proj, sb_cache, fox_cache, logf_t, bias_row, layer):
    rows, n_pages = page_table.shape
    page = sb_cache.shape[4]
    proj3 = proj.reshape(rows, 1, proj.shape[1])

    def col(cb):
        return pl.BlockSpec((None, 1, W_MIX), lambda b, pt: (b, 0, cb))

    def kv_page(j):
        return pl.BlockSpec((None, None, 2, W_MIX, page),
                            lambda b, pt, j=j: (pt[b * n_pages + j], layer, 0, 0, 0))

    def lf_page(j):
        return pl.BlockSpec((None, None, N_HEADS, page),
                            lambda b, pt, j=j: (pt[b * n_pages + j], layer, 0, 0))

    out_row = pl.BlockSpec((None, 1, W_MIX), lambda b, pt: (b, 0, 0))
    grid_spec = pltpu.PrefetchScalarGridSpec(
        num_scalar_prefetch=1,
        grid=(rows,),
        in_specs=[col(COL_SBQ), col(COL_FQ), col(COL_FK), col(COL_FV), col(COL_LOGIT),
                  pl.BlockSpec((1, W_MIX), lambda b, pt: (0, 0))]
                 + [kv_page(j) for j in range(n_pages)]
                 + [kv_page(j) for j in range(n_pages)]
                 + [lf_page(j) for j in range(n_pages)],
        out_specs=[out_row, out_row, out_row],
        scratch_shapes=[pltpu.VMEM((n_pages, SUBLANES, page), F32)],
    )
    sb_o, fx_o, lf_o = pl.pallas_call(
        functools.partial(_decode_kernel, n_pages=n_pages),
        grid_spec=grid_spec,
        out_shape=[jax.ShapeDtypeStruct((rows, 1, W_MIX), F32)] * 3,
        compiler_params=_params("arbitrary"),
        name="decode_attn",
    )(page_table.reshape(-1), proj3, proj3, proj3, proj3, proj3, bias_row,
      *([sb_cache] * n_pages), *([fox_cache] * n_pages), *([logf_t] * n_pages))
    return sb_o.reshape(rows, W_MIX), fx_o.reshape(rows, W_MIX), lf_o.reshape(rows, W_MIX)


def _merge_kernel(b0_ref, b1_ref, b2_ref, b3_ref, h_ref, *refs, precise):
    n_w = 6 if precise else 3
    w_refs, (x_ref, gate_ref, o_ref) = refs[:n_w], refs[n_w:]
    wg_ref, wb_ref, wo_ref = w_refs[:3]
    d = x_ref.shape[1]
    h = h_ref[...]
    merged = None
    for i, br in enumerate((b0_ref, b1_ref, b2_ref, b3_ref)):
        cols = slice(i * d, (i + 1) * d)
        if precise:
            gate_logit = _dot3(h, wg_ref[:, cols], w_refs[3][:, cols])
            mixed = _dot3(br[...].astype(F32), wb_ref[i], w_refs[4][i])
        else:
            gate_logit = _dot(h, wg_ref[:, cols])
            mixed = _dot(br[...].astype(BF16), wb_ref[i])
        y = jax.nn.sigmoid(gate_logit) * mixed
        merged = y if merged is None else merged + y
    if precise:
        out = _dot3(merged, wo_ref[...], w_refs[5][...])
    else:
        out = _dot(merged.astype(BF16), wo_ref[...])
    o_ref[...] = x_ref[...] + gate_ref[...] * out


def merge_out(branches, h, weights, layer, x, mod, tm):
    m, d = x.shape
    precise = len(weights) == 6
    br_spec = pl.BlockSpec((tm, W_MIX), lambda i: (i, 0))
    w_specs = [pl.BlockSpec((None, d, N_GATE), lambda i: (layer, 0, 0)),
               pl.BlockSpec((None, N_BRANCH, W_MIX, d), lambda i: (layer, 0, 0, 0)),
               pl.BlockSpec((None, d, d), lambda i: (layer, 0, 0))]
    return pl.pallas_call(
        functools.partial(_merge_kernel, precise=precise),
        grid=(m // tm,),
        in_specs=[br_spec] * N_BRANCH
                 + [pl.BlockSpec((tm, d), lambda i: (i, 0))]
                 + w_specs * (2 if precise else 1)
                 + [pl.BlockSpec((tm, d), lambda i: (i, 0)), mod.spec(2)],
        out_specs=pl.BlockSpec((tm, d), lambda i: (i, 0)),
        out_shape=jax.ShapeDtypeStruct((m, d), F32),
        compiler_params=_params("parallel"),
        name="merge_out",
    )(*branches, h, *weights, x, mod.arr)


def _route(scores, biased):
    lane_i = _iota(scores.shape, 1)
    real = lane_i < N_EXPERTS
    pos = lane_i % EXPERTS_PER_GROUP
    lane = lane_i.astype(F32)
    group = (lane_i // EXPERTS_PER_GROUP).astype(F32)
    neg = -jnp.inf
    n = scores.shape[1]
    mates = [biased]
    for k in range(1, EXPERTS_PER_GROUP):
        fwd = pltpu.roll(biased, n - k, axis=1)
        back = pltpu.roll(biased, EXPERTS_PER_GROUP - k, axis=1)
        mates.append(jnp.where(pos + k < EXPERTS_PER_GROUP, fwd, back))
    group_score = None
    for a in range(EXPERTS_PER_GROUP):
        for b in range(a + 1, EXPERTS_PER_GROUP):
            pair = mates[a] + mates[b]
            group_score = pair if group_score is None else jnp.maximum(group_score, pair)
    group_score = jnp.where(real, group_score, neg)
    best_score = jnp.max(group_score, axis=1, keepdims=True)
    best = jnp.min(jnp.where(group_score == best_score, group, float(n)), axis=1, keepdims=True)
    cand = jnp.where(real & (group == best), biased, neg)
    top0 = jnp.max(cand, axis=1, keepdims=True)
    idx0 = jnp.min(jnp.where(cand == top0, lane, float(n)), axis=1, keepdims=True)
    cand = jnp.where(lane == idx0, neg, cand)
    top1 = jnp.max(cand, axis=1, keepdims=True)
    idx1 = jnp.min(jnp.where(cand == top1, lane, float(n)), axis=1, keepdims=True)
    chosen = (lane == idx0) | (lane == idx1)
    sel = jnp.where(chosen, scores, 0.0)
    return sel / jnp.sum(sel, axis=1, keepdims=True), best


def _moe_kernel(x_ref, g_ref, sh_ref, sc_ref, gate_ref, rw_ref, rb_ref, w1_ref, w3_ref, w2_ref,
                gf_ref, o_ref, w1b_ref, w3b_ref, w2b_ref, h_scr, comb_scr, acc_scr, *, final_norm):
    e = pl.program_id(1)

    @pl.when(e == 0)
    def _():
        h = _norm_mod(x_ref[...], g_ref[...], sc_ref[...], sh_ref[...])
        h_scr[...] = h
        scores = jax.nn.sigmoid(_dot3(h, *_split2(rw_ref[...])))
        comb_scr[...] = _route(scores, scores + rb_ref[...])[0]
        acc_scr[...] = jnp.zeros_like(acc_scr)

    h = h_scr[...]
    w1_hi, w1_lo = _split2(w1_ref[...])
    w3_hi, w3_lo = _split2(w3_ref[...])
    w2_hi, w2_lo = _split2(w2_ref[...])
    w1b_ref[...] = w1_hi
    w3b_ref[...] = w3_hi
    w2b_ref[...] = w2_hi
    a = _dot3(h, w1_hi, w1_lo)
    b = _dot3(h, w3_hi, w3_lo)
    y = _dot3(_silu(a) * b, w2_hi, w2_lo)
    lane = _iota(comb_scr.shape, 1)
    w_e = jnp.sum(jnp.where(lane == e, comb_scr[...], 0.0), axis=1, keepdims=True)
    acc_scr[...] = acc_scr[...] + w_e * y

    @pl.when(e == N_EXPERTS - 1)
    def _():
        out = x_ref[...] + gate_ref[...] * acc_scr[...]
        if final_norm:
            ms = jnp.mean(out * out, axis=-1, keepdims=True)
            out = out * lax.rsqrt(ms + NORM_EPS) * gf_ref[...]
        o_ref[...] = out


def moe(x, g, mod, router_w_pad, router_b_pad, w1, w3, w2, layer, tm, final_gain=None):
    m, d = x.shape
    assert m == tm
    f = w1.shape[-1]
    n_e = w1.shape[1]
    expert = lambda i, e: (layer, e, 0, 0)
    per_layer = lambda i, e: (e, 0, 0)
    final_norm = final_gain is not None
    gf = (final_gain if final_norm else g).reshape(1, d)
    return pl.pallas_call(
        functools.partial(_moe_kernel, final_norm=final_norm),
        grid=(m // tm, N_EXPERTS),
        in_specs=[pl.BlockSpec((tm, d), lambda i, e: (i, 0)),
                  pl.BlockSpec((1, d), lambda i, e: (0, 0)),
                  mod.spec(3), mod.spec(4), mod.spec(5),
                  pl.BlockSpec((d, LANES), lambda i, e: (0, 0)),
                  pl.BlockSpec((1, LANES), lambda i, e: (0, 0)),
                  pl.BlockSpec((None, None, d, f), expert),
                  pl.BlockSpec((None, None, d, f), expert),
                  pl.BlockSpec((None, None, f, d), expert),
                  pl.BlockSpec((1, d), lambda i, e: (0, 0))],
        out_specs=[pl.BlockSpec((tm, d), lambda i, e: (i, 0)),
                   pl.BlockSpec((None, d, f), per_layer),
                   pl.BlockSpec((None, d, f), per_layer),
                   pl.BlockSpec((None, f, d), per_layer)],
        out_shape=[jax.ShapeDtypeStruct((m, d), F32),
                   jax.ShapeDtypeStruct((n_e, d, f), BF16),
                   jax.ShapeDtypeStruct((n_e, d, f), BF16),
                   jax.ShapeDtypeStruct((n_e, f, d), BF16)],
        scratch_shapes=[pltpu.VMEM((tm, d), F32), pltpu.VMEM((tm, LANES), F32),
                        pltpu.VMEM((tm, d), F32)],
        compiler_params=_params("arbitrary", "arbitrary"),
        name="moe",
    )(x, g.reshape(1, d), mod.arr, mod.arr, mod.arr, router_w_pad, router_b_pad, w1, w3, w2, gf)


_MOE_SUB = 128
_MOE_EXPERTS_PER_STEP = 2
N_GROUPS = N_EXPERTS // EXPERTS_PER_GROUP


def _moe_grouped_kernel(x_ref, g_ref, sh_ref, sc_ref, gate_ref, rw_ref, rb_ref, w1_ref, w3_ref,
                        w2_ref, gf_ref, o_ref, hs_scr, ys_scr, cs_scr, pt_scr, meta_ref, *,
                        final_norm):
    step = pl.program_id(1)
    per_step = w1_ref.shape[0]
    tm = x_ref.shape[0]
    slots = hs_scr.shape[0]
    sub = _MOE_SUB

    @pl.when(step == 0)
    def _():
        h = _norm_mod(x_ref[...], g_ref[...], sc_ref[...], sh_ref[...])
        hb = h.astype(BF16)
        scores = jax.nn.sigmoid(_dot(hb, rw_ref[...].astype(BF16)))
        comb, best = _route(scores, scores + rb_ref[...])
        lane = _iota((tm, LANES), 1).astype(F32)
        member = jnp.where(lane == best, 1.0, 0.0)
        before = (_iota((tm, tm), 1) < _iota((tm, tm), 0)).astype(BF16)
        rank = _dot(before, member.astype(BF16))
        count = rank[tm - 1:tm, :] + member[tm - 1:tm, :]
        room = jnp.floor((count + (sub - 1)) * (1.0 / sub)) * sub
        lane_row = _iota((1, LANES), 1)
        start = jnp.zeros((1, LANES), F32)
        for k in range(1, N_GROUPS):
            start = start + jnp.where(lane_row >= k, pltpu.roll(room, k, axis=1), 0.0)
        dest = jnp.sum(member * (start + rank), axis=1, keepdims=True)
        place = jnp.where(_iota((tm, slots), 1).astype(F32) == dest, 1.0, 0.0).astype(BF16)
        pt_scr[...] = place
        hs_scr[...] = _dot_tn(place, hb).astype(BF16)
        c_hi, c_lo = _split2(comb)
        cs_scr[...] = _dot_tn(place, c_hi) + _dot_tn(place, c_lo)
        ys_scr[...] = jnp.zeros_like(ys_scr)
        for k in range(N_GROUPS):
            pick = lane_row == k
            meta_ref[k] = jnp.sum(jnp.where(pick, start, 0.0)).astype(jnp.int32)
            meta_ref[N_GROUPS + k] = jnp.sum(jnp.where(pick, room, 0.0)).astype(jnp.int32) // sub

    group = (step * per_step) // EXPERTS_PER_GROUP
    first = meta_ref[group]
    lane = _iota((sub, LANES), 1)

    def body(s, carry):
        r0 = pl.multiple_of(first + s * sub, sub)
        rows = hs_scr[pl.ds(r0, sub), :]
        comb = cs_scr[pl.ds(r0, sub), :]
        acc = ys_scr[pl.ds(r0, sub), :]
        for k in range(per_step):
            a = _dot(rows, w1_ref[k])
            b = _dot(rows, w3_ref[k])
            y = _dot((_silu(a) * b).astype(BF16), w2_ref[k])
            w_e = jnp.sum(jnp.where(lane == step * per_step + k, comb, 0.0), axis=1, keepdims=True)
            acc = acc + w_e * y
        ys_scr[pl.ds(r0, sub), :] = acc
        return carry

    lax.fori_loop(0, meta_ref[N_GROUPS + group], body, 0)

    @pl.when(step == pl.num_programs(1) - 1)
    def _():
        y_hi, y_lo = _split2(ys_scr[...])
        place = pt_scr[...]
        out = x_ref[...] + gate_ref[...] * (_dot(place, y_hi) + _dot(place, y_lo))
        if final_norm:
            ms = jnp.mean(out * out, axis=-1, keepdims=True)
            out = out * lax.rsqrt(ms + NORM_EPS) * gf_ref[...]
        o_ref[...] = out


def moe_grouped(x, g, mod, router_w_pad, router_b_pad, w1, w3, w2, tm, final_gain=None):
    m, d = x.shape
    f = w1.shape[-1]
    slots = tm + N_GROUPS * _MOE_SUB
    per_step = _MOE_EXPERTS_PER_STEP
    assert EXPERTS_PER_GROUP % per_step == 0
    expert = lambda i, e: (e, 0, 0)
    final_norm = final_gain is not None
    gf = (final_gain if final_norm else g).reshape(1, d)
    return pl.pallas_call(
        functools.partial(_moe_grouped_kernel, final_norm=final_norm),
        grid=(m // tm, N_EXPERTS // per_step),
        in_specs=[pl.BlockSpec((tm, d), lambda i, e: (i, 0)),
                  pl.BlockSpec((1, d), lambda i, e: (0, 0)),
                  mod.spec(3), mod.spec(4), mod.spec(5),
                  pl.BlockSpec((d, LANES), lambda i, e: (0, 0)),
                  pl.BlockSpec((1, LANES), lambda i, e: (0, 0)),
                  pl.BlockSpec((per_step, d, f), expert),
                  pl.BlockSpec((per_step, d, f), expert),
                  pl.BlockSpec((per_step, f, d), expert),
                  pl.BlockSpec((1, d), lambda i, e: (0, 0))],
        out_specs=pl.BlockSpec((tm, d), lambda i, e: (i, 0)),
        out_shape=jax.ShapeDtypeStruct((m, d), F32),
        scratch_shapes=[pltpu.VMEM((slots, d), BF16), pltpu.VMEM((slots, d), F32),
                        pltpu.VMEM((slots, LANES), F32), pltpu.VMEM((tm, slots), BF16),
                        pltpu.SMEM((2 * N_GROUPS,), jnp.int32)],
        compiler_params=_params("parallel", "arbitrary"),
        name="moe_grouped",
    )(x, g.reshape(1, d), mod.arr, mod.arr, mod.arr, router_w_pad, router_b_pad, w1, w3, w2, gf)


def _rope_tables(pos):
    half = HEAD_DIM // 2
    inv = ROPE_BASE ** (-jnp.arange(half, dtype=F32) / half)
    ang = pos.astype(F32)[:, None] * inv[None, :]
    cos, sin = jnp.cos(ang), jnp.sin(ang)
    cos_row = jnp.tile(jnp.concatenate([cos, cos], axis=1), (1, N_HEADS))
    sin_row = jnp.tile(jnp.concatenate([-sin, sin], axis=1), (1, N_HEADS))
    return cos_row, sin_row


def _retention_tables(pos, chunk):
    lg = jnp.log1p(-(2.0 ** (-5.0 - jnp.arange(N_HEADS, dtype=F32))))
    idx = jnp.arange(chunk, dtype=F32)
    diff = idx[:, None] - idx[None, :]
    inner = jnp.where(diff >= 0, jnp.exp(lg[:, None, None] * jnp.maximum(diff, 0.0)), 0.0)
    q_decay = jnp.exp(lg[None, :] * (idx[:, None] + 1.0))
    k_decay = jnp.exp(lg[None, :] * (chunk - 1.0 - idx[:, None]))
    chunk_decay = jnp.exp(lg * chunk)
    lanes = lambda a: jnp.repeat(a, HEAD_DIM, axis=-1)
    cos, sin = _rope_tables(pos)
    return {"cos": cos, "sin": sin, "inner": inner, "q_decay": lanes(q_decay),
            "k_decay": lanes(k_decay), "chunk_decay": lanes(chunk_decay[None, :]),
            "gamma": chunk_decay}


def _pack_w_kernel(w_ref, wm_ref, wg_ref, wm_lo_ref, wg_lo_ref):
    rows = w_ref.shape[0]

    def put(hi_ref, lo_ref, cols, val):
        hi, lo = _split2(val)
        hi_ref[:, cols] = hi
        lo_ref[:, cols] = lo

    put(wm_ref, wm_lo_ref, slice(0, N_MAIN), w_ref[:, 0:N_MAIN])
    lane = _iota((rows, W_MIX), 1)
    put(wm_ref, wm_lo_ref, slice(N_MAIN, GATE_COL0),
        jnp.where(lane < N_HEADS, w_ref[:, N_MAIN:GATE_COL0], 0.0))
    step = 1024
    for c in range(0, N_GATE, step):
        width = min(step + LANES, N_GATE + N_HEADS - c)
        win = w_ref[:, N_MAIN + c:N_MAIN + c + width]
        put(wg_ref, wg_lo_ref, slice(c, c + step), win[:, N_HEADS:N_HEADS + step])


def pack_w_in(w_in):
    depth, d, n_in = w_in.shape
    assert n_in == N_MAIN + N_HEADS + N_GATE
    tr = 256
    main = pl.BlockSpec((None, tr, GATE_COL0), lambda l, r: (l, r, 0))
    gate = pl.BlockSpec((None, tr, N_GATE), lambda l, r: (l, r, 0))
    main_shape = jax.ShapeDtypeStruct((depth, d, GATE_COL0), BF16)
    gate_shape = jax.ShapeDtypeStruct((depth, d, N_GATE), BF16)
    return pl.pallas_call(
        _pack_w_kernel,
        grid=(depth, d // tr),
        in_specs=[pl.BlockSpec((None, tr, n_in), lambda l, r: (l, r, 0))],
        out_specs=[main, gate, main, gate],
        out_shape=[main_shape, gate_shape, main_shape, gate_shape],
        compiler_params=_params("parallel", "parallel"),
        name="pack_w_in",
    )(w_in)


def _split_w_kernel(w_ref, hi_ref, lo_ref):
    hi, lo = _split2(w_ref[...])
    hi_ref[...] = hi
    lo_ref[...] = lo


def split_weights(w):
    cols = w.shape[-1]
    flat = w.reshape(-1, cols)
    tr = 256
    spec = pl.BlockSpec((tr, cols), lambda i: (i, 0))
    shape = jax.ShapeDtypeStruct(flat.shape, BF16)
    hi, lo = pl.pallas_call(
        _split_w_kernel,
        grid=(flat.shape[0] // tr,),
        in_specs=[spec],
        out_specs=[spec, spec],
        out_shape=[shape, shape],
        compiler_params=_params("parallel"),
        name="split_weights",
    )(flat)
    return hi.reshape(w.shape), lo.reshape(w.shape)


def _block_diag_pool(w_pool_l):
    g, c, _ = w_pool_l.shape
    out = jnp.zeros((g * c, g * c), w_pool_l.dtype)
    for i in range(g):
        out = out.at[i * c:(i + 1) * c, i * c:(i + 1) * c].set(w_pool_l[i])
    return out


def _pad_lanes(row, n):
    return jnp.pad(row, ((0, 0), (0, n - row.shape[1])))


def _diag_blocks(s_bd):
    return jnp.stack([s_bd[:, h * HEAD_DIM:(h + 1) * HEAD_DIM, h * HEAD_DIM:(h + 1) * HEAD_DIM]
                      for h in range(N_HEADS)], axis=1)


def kernel(x_prompt, x_sample, cache_sb_kv, cache_fox_kv, cache_fox_logf, state_pool, state_ret,
           page_table, c_prompt, c_sample, w_ada, b_ada, norm_mix, w_in, w_pool, pool_scale,
           fox_bias, w_branch, w_out, norm_ffn, router_w, router_b, w1, w3, w2, norm_final):
    bp, tp, d = x_prompt.shape
    db, ts, _ = x_sample.shape
    assert ts == 1 and d == D_MODEL
    depth = w_in.shape[0]
    n_phys, _, _, page, _, _ = cache_sb_kv.shape
    n_pages = page_table.shape[1]
    past_len = n_pages * page
    mp = bp * tp

    mod_all = adaln(jnp.concatenate([c_prompt, c_sample], axis=0), w_ada, b_ada)

    sb_cache = jnp.transpose(cache_sb_kv, (0, 1, 2, 4, 5, 3)).reshape(n_phys, depth, 2, W_MIX, page)
    fox_cache = jnp.transpose(cache_fox_kv, (0, 1, 2, 4, 5, 3)).reshape(n_phys, depth, 2, W_MIX, page)
    logf_t = jnp.swapaxes(cache_fox_logf, 2, 3)
    state_t = jnp.transpose(state_ret, (1, 2, 3, 4, 0)).reshape(depth * N_HEADS, HEAD_DIM, HEAD_DIM, db)
    w_main, w_gate, w_main_lo, w_gate_lo = pack_w_in(w_in)
    router_w_pad = _pad_lanes(router_w, LANES)
    router_b_pad = _pad_lanes(router_b[None, :], LANES)
    tabs_p = _retention_tables(jnp.arange(tp), RET_CHUNK if tp % RET_CHUNK == 0 else tp)
    tabs_s = _retention_tables(past_len + jnp.arange(ts), ts)
    wb, wb_lo = split_weights(w_branch)
    wo, wo_lo = split_weights(w_out)
    w_merge_p = (w_gate, wb, wo)
    w_merge_s = w_merge_p + (w_gate_lo, wb_lo, wo_lo)

    tm_p = min(1024, tp)
    tm_mrg = min(512, tp)
    xp = x_prompt.reshape(mp, d)
    xs = x_sample.reshape(db, d)
    outs_p = {k: [] for k in ("logf", "pool", "ret")}
    outs_s = {k: [] for k in ("sb", "fox", "logf", "pool", "ret")}
    kv_prev = None

    for l in range(depth):
        w_bd_f32 = _block_diag_pool(w_pool[l])
        w_bd = w_bd_f32.astype(BF16)
        scale_row = pool_scale[l][None, :]
        bias_lanes = _pad_lanes(fox_bias[l][None, :], LANES)
        bias_row = _pad_lanes(fox_bias[l][None, :], W_MIX)

        final_gain = norm_final if l == depth - 1 else None

        mod_s = Mod(mod_all[l, bp:], 1, db)
        proj_s, h_s = in_proj(xs, norm_mix[l], mod_s, w_main, l, db, w_lo=w_main_lo)
        buf_t = jnp.swapaxes(state_pool[:, l], 0, 1)
        pool_s = pool_sample(buf_t, proj_s, w_bd_f32, scale_row, past_len)
        sb_s, fox_s, logf_s = decode_attn(page_table, proj_s, sb_cache, fox_cache, logf_t, bias_row, l)
        ret_s, state_new = ret_sample(proj_s, state_t, l, tabs_s)
        xs = merge_out((pool_s, sb_s, ret_s, fox_s), h_s, w_merge_s, l, xs, mod_s, db)
        xs, w1b, w3b, w2b = moe(xs, norm_ffn[l], mod_s, router_w_pad, router_b_pad, w1, w3, w2, l,
                                db, final_gain)

        mod_in = Mod(mod_all[l, :bp], tp, tm_p)
        mod_mrg = Mod(mod_all[l, :bp], tp, tm_mrg)
        kv_sb, kv_fox, proj, h = in_proj(xp, norm_mix[l], mod_in, w_main, l, tm_p,
                                         kv_seq=(bp, tp), kv_prev=kv_prev)
        kv_prev = (kv_sb, kv_fox)
        pool_o = pool_prompt(proj, w_bd, scale_row, bp, tp)
        sb_o = sb_prompt(proj, bp, tp)
        ret_o, ret_state = ret_prompt(proj, tabs_p, bp, tp)
        logf, cum, cum_t = logf_cum_prompt(proj, bias_lanes, bp, tp)
        fox_o = fox_prompt(proj, cum, cum_t, bp, tp)
        xp = merge_out((pool_o, sb_o, ret_o, fox_o), h, w_merge_p, l, xp, mod_mrg, tm_mrg)
        xp = moe_grouped(xp, norm_ffn[l], mod_in, router_w_pad, router_b_pad, w1b, w3b, w2b, tm_p,
                         final_gain)
        p3 = proj.reshape(bp, tp, GATE_COL0)
        outs_p["logf"].append(logf.reshape(bp, tp, LANES)[:, :, :N_HEADS])
        outs_p["pool"].append(p3[:, tp - POOL_BUF:, :W_MIX])
        outs_p["ret"].append(_diag_blocks(ret_state))

        def kv_s(cb, proj_s=proj_s):
            return jnp.stack([proj_s[:, cb * W_MIX:(cb + 1) * W_MIX],
                              proj_s[:, (cb + 1) * W_MIX:(cb + 2) * W_MIX]], axis=1)

        outs_s["sb"].append(kv_s(COL_SBK))
        outs_s["fox"].append(kv_s(COL_FK))
        outs_s["logf"].append(logf_s[:, :N_HEADS])
        outs_s["pool"].append(jnp.concatenate([state_pool[:, l, 1:], proj_s[:, None, :W_MIX]], axis=1))
        outs_s["ret"].append(state_new)

    y_p = xp.reshape(bp, tp, d)
    y_s = xs.reshape(db, ts, d)

    def heads(a, t):
        return a.reshape(a.shape[0], depth, 2, t, N_HEADS, HEAD_DIM)

    def from_transposed(kv):
        return heads(jnp.transpose(kv, (2, 0, 1, 4, 3)), tp)

    return (y_p, y_s,
            from_transposed(kv_prev[0]),
            heads(jnp.stack(outs_s["sb"], axis=1), ts),
            from_transposed(kv_prev[1]),
            heads(jnp.stack(outs_s["fox"], axis=1), ts),
            jnp.stack(outs_p["logf"], axis=1),
            jnp.stack(outs_s["logf"], axis=1).reshape(db, depth, ts, N_HEADS),
            jnp.stack(outs_p["pool"], axis=1),
            jnp.stack(outs_s["pool"], axis=1),
            jnp.stack(outs_p["ret"], axis=1),
            jnp.transpose(jnp.stack(outs_s["ret"], axis=0), (4, 0, 1, 2, 3)))
```
